```python
import jax, jax.numpy as jnp
from jax import lax
import numpy as np

D_MODEL = 4096
BATCH = 4
SEQ = 2048
DEPTH = 4

CHUNK = 64
N_MIXERS = 3
ROPE_THETA = 10000.0
NORM_EPS = 1e-6
D_FF = 6144
FFN_RES = 0.5
MLA_HEADS = 32
MLA_Q_LORA = 1024
MLA_KV_LORA = 512
MLA_NOPE = 128
MLA_ROPE = 64
MLA_V = 128
ATTN_Q_BLOCK = 128
DSA_HEADS = 32
DSA_KV_HEADS = 8
DSA_HEAD_DIM = 128
IDX_HEADS = 32
IDX_DIM = 128
IDX_ROPE = 64
DSA_TOPK_MAX = 256
RWKV_HEAD = 64
RWKV_HEADS = D_MODEL // RWKV_HEAD
DECAY_LORA = 128
AAA_LORA = 128
GATE_LORA = 480
GN_EPS = 64e-5

kernel_name = 'hybrid_mla_dsa_rwkv7_macaron'


def rms_norm(x, g):
    xf = x.astype(jnp.float32)
    y = xf * lax.rsqrt(jnp.mean(xf * xf, axis=-1, keepdims=True) + NORM_EPS)
    return (y * g.astype(jnp.float32)).astype(x.dtype)


def rope(x, pos):
    d = x.shape[-1]
    inv = jnp.power(ROPE_THETA, -jnp.arange(0, d, 2, dtype=jnp.float32) / d)
    ang = pos.astype(jnp.float32)[..., None] * inv
    ang = ang.reshape(ang.shape[:2] + (1,) * (x.ndim - 3) + (d // 2,))
    cos, sin = jnp.cos(ang), jnp.sin(ang)
    xf = x.astype(jnp.float32)
    x1, x2 = xf[..., : d // 2], xf[..., d // 2:]
    return jnp.concatenate([x1 * cos - x2 * sin, x2 * cos + x1 * sin], axis=-1).astype(x.dtype)


def swiglu(h, w_gu, w_down):
    gate, up = jnp.split(h @ w_gu, 2, axis=-1)
    return (jax.nn.silu(gate) * up) @ w_down


def mla_mixer(h, pos, w_in, q_norm, w_uq, kv_norm, w_ukv, w_o):
    B, T, _ = h.shape
    lat = h @ w_in
    c_q, c_kv, k_rope = jnp.split(lat, [MLA_Q_LORA, MLA_Q_LORA + MLA_KV_LORA], axis=-1)
    q = (rms_norm(c_q, q_norm) @ w_uq).reshape(B, T, MLA_HEADS, MLA_NOPE + MLA_ROPE)
    q_nope, q_rope = q[..., :MLA_NOPE], rope(q[..., MLA_NOPE:], pos)
    k_rope = rope(k_rope, pos)
    kv = (rms_norm(c_kv, kv_norm) @ w_ukv).reshape(B, T, MLA_HEADS, MLA_NOPE + MLA_V)
    k_nope, v = kv[..., :MLA_NOPE], kv[..., MLA_NOPE:]
    scale = (MLA_NOPE + MLA_ROPE) ** -0.5
    outs = []
    for start in range(0, T, ATTN_Q_BLOCK):
        end = start + ATTN_Q_BLOCK
        s = (jnp.einsum('bqhd,bkhd->bhqk', q_nope[:, start:end], k_nope[:, :end])
             + jnp.einsum('bqhr,bkr->bhqk', q_rope[:, start:end], k_rope[:, :end]))
        s = s.astype(jnp.float32) * scale
        q_chunk = np.arange(start, end) // CHUNK
        k_chunk = np.arange(end) // CHUNK
        s = jnp.where(k_chunk[None, :] <= q_chunk[:, None], s, -jnp.inf)
        p = jax.nn.softmax(s, axis=-1).astype(v.dtype)
        outs.append(jnp.einsum('bhqk,bkhd->bqhd', p, v[:, :end]))
    o = jnp.concatenate(outs, axis=1).reshape(B, T, MLA_HEADS * MLA_V)
    return o @ w_o


def dsa_mixer(h, pos, w_in, w_o):
    B, T, _ = h.shape
    G, HD = DSA_KV_HEADS, DSA_HEAD_DIM
    rep = DSA_HEADS // G
    topk = min(DSA_TOPK_MAX, T // 4)
    sizes = [DSA_HEADS * HD, G * HD, G * HD, IDX_HEADS * IDX_DIM, IDX_DIM]
    splits = [int(s) for s in np.cumsum(sizes)]
    q, k, v, qi, ki, wi = jnp.split(h @ w_in, splits, axis=-1)
    q = rope(q.reshape(B, T, DSA_HEADS, HD), pos)
    k = rope(k.reshape(B, T, G, HD), pos)
    v = v.reshape(B, T, G, HD)
    qi = qi.reshape(B, T, IDX_HEADS, IDX_DIM)
    qi = jnp.concatenate([rope(qi[..., :IDX_ROPE], pos), qi[..., IDX_ROPE:]], axis=-1)
    ki = jnp.concatenate([rope(ki[..., :IDX_ROPE], pos), ki[..., IDX_ROPE:]], axis=-1)
    wi = wi * (IDX_HEADS ** -0.5 * IDX_DIM ** -0.5)
    key_chunk = jnp.arange(T) // CHUNK
    gather = jax.vmap(lambda a, i: a[i])

    def chunk_attn(c):
        st = c * CHUNK
        qi_c = lax.dynamic_slice_in_dim(qi, st, CHUNK, axis=1)
        wi_c = lax.dynamic_slice_in_dim(wi, st, CHUNK, axis=1)
        q_c = lax.dynamic_slice_in_dim(q, st, CHUNK, axis=1)
        dots = jax.nn.relu(jnp.einsum('bqhd,bsd->bqhs', qi_c, ki))
        score = jnp.einsum('bqh,bqhs->bqs', wi_c, dots).astype(jnp.float32)
        score = jnp.where((key_chunk <= c)[None, None, :], score, -jnp.inf)
        _, idx = lax.top_k(score, topk)
        valid = idx < (c + 1) * CHUNK
        k_sel = gather(k, idx)
        v_sel = gather(v, idx)
        qg = q_c.reshape(B, CHUNK, G, rep, HD)
        s = jnp.einsum('bqgrd,bqkgd->bqgrk', qg, k_sel).astype(jnp.float32) * HD ** -0.5
        s = jnp.where(valid[:, :, None, None, :], s, -jnp.inf)
        p = jax.nn.softmax(s, axis=-1).astype(v.dtype)
        o = jnp.einsum('bqgrk,bqkgd->bqgrd', p, v_sel)
        return o.reshape(B, CHUNK, DSA_HEADS * HD)

    o = lax.map(chunk_attn, jnp.arange(T // CHUNK))
    o = jnp.moveaxis(o, 0, 1).reshape(B, T, DSA_HEADS * HD)
    return o @ w_o


def wkv7_scan(r, w, k, v, a, b):
    B, T, H, N = r.shape

    def step(S, inp):
        r_t, w_t, k_t, v_t, a_t, b_t = inp
        sa = jnp.einsum('bhvk,bhk->bhv', S, a_t)
        S = (S * w_t[:, :, None, :] + sa[..., None] * b_t[:, :, None, :]
             + v_t[..., None] * k_t[:, :, None, :])
        return S, jnp.einsum('bhvk,bhk->bhv', S, r_t)

    xs = tuple(jnp.moveaxis(t.astype(jnp.float32), 1, 0) for t in (r, w, k, v, a, b))
    S0 = jnp.zeros((B, H, N, N), jnp.float32)
    _, y = lax.scan(step, S0, xs)
    return jnp.moveaxis(y, 0, 1)


def rwkv7_mixer(h, mu, w_r, w_k, w_v, w_o, w0, w1, w2, a0, a1, a2, g1, g2,
                k_k, k_a, r_k, lnx_w, lnx_b):
    B, T, C = h.shape
    H, N = RWKV_HEADS, RWKV_HEAD
    heads = lambda t: t.reshape(B, T, H, N)
    xx = jnp.pad(h, ((0, 0), (1, 0), (0, 0)))[:, :-1] - h
    mu_r, mu_w, mu_k, mu_v, mu_a, mu_g = mu
    xr, xw, xk = h + xx * mu_r, h + xx * mu_w, h + xx * mu_k
    xv, xa, xg = h + xx * mu_v, h + xx * mu_a, h + xx * mu_g
    r = xr @ w_r
    w_log = -jax.nn.softplus(-(w0 + jnp.tanh(xw @ w1) @ w2)) - 0.5
    k = xk @ w_k
    v = xv @ w_v
    a = jax.nn.sigmoid(a0 + (xa @ a1) @ a2)
    g = jax.nn.sigmoid(xg @ g1) @ g2
    kk = heads(k * k_k).astype(jnp.float32)
    kk = kk / jnp.maximum(jnp.sqrt(jnp.sum(kk * kk, axis=-1, keepdims=True)), 1e-12)
    k = k * (1 + (a - 1) * k_a)
    decay = jnp.exp(-jnp.exp(w_log.astype(jnp.float32)))
    y = wkv7_scan(heads(r), heads(decay), heads(k), heads(v), -kk,
                  kk * heads(a).astype(jnp.float32))
    mean = jnp.mean(y, axis=-1, keepdims=True)
    var = jnp.mean(jnp.square(y - mean), axis=-1, keepdims=True)
    y = ((y - mean) * lax.rsqrt(var + GN_EPS)).reshape(B, T, C) * lnx_w + lnx_b
    bonus = jnp.sum(heads(r) * heads(k) * r_k, axis=-1, keepdims=True) * heads(v)
    y = y + bonus.reshape(B, T, C)
    return (y * g).astype(h.dtype) @ w_o


def setup_inputs(seed: int = 0) -> dict:
    key = jax.random.key(seed)
    ks = list(jax.random.split(key, 96))
    D, F = D_MODEL, D_FF

    def nrm(shape, scale):
        return jax.random.normal(ks.pop(), shape, jnp.float32) * scale

    def gain(shape):
        return 1.0 + nrm(shape, 0.05)

    def ffn_params(i):
        return {'ffn_norm_%d' % i: gain((2, D)),
                'ffn_w_gu_%d' % i: nrm((2, D, 2 * F), D ** -0.5),
                'ffn_w_down_%d' % i: nrm((2, F, D), F ** -0.5),
                'mix_norm_%d' % i: gain((D,))}

    def mla_params(i):
        p = 'mla%d_' % i
        return {p + 'w_in': nrm((D, MLA_Q_LORA + MLA_KV_LORA + MLA_ROPE), D ** -0.5),
                p + 'q_norm': gain((MLA_Q_LORA,)),
                p + 'w_uq': nrm((MLA_Q_LORA, MLA_HEADS * (MLA_NOPE + MLA_ROPE)), MLA_Q_LORA ** -0.5),
                p + 'kv_norm': gain((MLA_KV_LORA,)),
                p + 'w_ukv': nrm((MLA_KV_LORA, MLA_HEADS * (MLA_NOPE + MLA_V)), MLA_KV_LORA ** -0.5),
                p + 'w_o': nrm((MLA_HEADS * MLA_V, D), (MLA_HEADS * MLA_V) ** -0.5)}

    def dsa_params(i):
        p = 'dsa%d_' % i
        width = (DSA_HEADS * DSA_HEAD_DIM + 2 * DSA_KV_HEADS * DSA_HEAD_DIM
                 + IDX_HEADS * IDX_DIM + IDX_DIM + IDX_HEADS)
        return {p + 'w_in': nrm((D, width), D ** -0.5),
                p + 'w_o': nrm((DSA_HEADS * DSA_HEAD_DIM, D), (DSA_HEADS * DSA_HEAD_DIM) ** -0.5)}

    def rwkv_params(i):
        p = 'rwkv%d_' % i
        return {p + 'mu': jax.random.uniform(ks.pop(), (6, D), jnp.float32),
                p + 'w_r': nrm((D, D), D ** -0.5),
                p + 'w_k': nrm((D, D), D ** -0.5),
                p + 'w_v': nrm((D, D), D ** -0.5),
                p + 'w_o': nrm((D, D), D ** -0.5),
                p + 'w0': -1.0 + nrm((D,), 0.5),
                p + 'w1': nrm((D, DECAY_LORA), D ** -0.5),
                p + 'w2': nrm((DECAY_LORA, D), 0.1 * DECAY_LORA ** -0.5),
                p + 'a0': nrm((D,), 0.1),
                p + 'a1': nrm((D, AAA_LORA), D ** -0.5),
                p + 'a2': nrm((AAA_LORA, D), 0.1 * AAA_LORA ** -0.5),
                p + 'g1': nrm((D, GATE_LORA), D ** -0.5),
                p + 'g2': nrm((GATE_LORA, D), GATE_LORA ** -0.5),
                p + 'k_k': 0.85 + nrm((D,), 0.05),
                p + 'k_a': gain((D,)),
                p + 'r_k': nrm((RWKV_HEADS, RWKV_HEAD), 0.1),
                p + 'lnx_w': gain((D,)),
                p + 'lnx_b': nrm((D,), 0.01)}

    x = jax.random.normal(ks.pop(), (BATCH, SEQ, D), jnp.float32)
    start = jax.random.randint(ks.pop(), (BATCH, 1), 0, 4096, dtype=jnp.int32)
    positions = start + jnp.arange(SEQ, dtype=jnp.int32)[None, :]
    inputs = {'x': x, 'positions': positions}
    inputs.update(ffn_params(0)); inputs.update(mla_params(0))
    inputs.update(ffn_params(1)); inputs.update(dsa_params(1))
    inputs.update(ffn_params(2)); inputs.update(rwkv_params(2))
    inputs.update(ffn_params(3)); inputs.update(mla_params(3))
    inputs['final_norm'] = gain((D,))
    return inputs


def reference(x, positions,
              ffn_norm_0, ffn_w_gu_0, ffn_w_down_0, mix_norm_0,
              mla0_w_in, mla0_q_norm, mla0_w_uq, mla0_kv_norm, mla0_w_ukv, mla0_w_o,
              ffn_norm_1, ffn_w_gu_1, ffn_w_down_1, mix_norm_1,
              dsa1_w_in, dsa1_w_o,
              ffn_norm_2, ffn_w_gu_2, ffn_w_down_2, mix_norm_2,
              rwkv2_mu, rwkv2_w_r, rwkv2_w_k, rwkv2_w_v, rwkv2_w_o, rwkv2_w0, rwkv2_w1, rwkv2_w2,
              rwkv2_a0, rwkv2_a1, rwkv2_a2, rwkv2_g1, rwkv2_g2, rwkv2_k_k, rwkv2_k_a, rwkv2_r_k,
              rwkv2_lnx_w, rwkv2_lnx_b,
              ffn_norm_3, ffn_w_gu_3, ffn_w_down_3, mix_norm_3,
              mla3_w_in, mla3_q_norm, mla3_w_uq, mla3_kv_norm, mla3_w_ukv, mla3_w_o,
              final_norm):
    mixers = [
        lambda h: mla_mixer(h, positions, mla0_w_in, mla0_q_norm, mla0_w_uq,
                            mla0_kv_norm, mla0_w_ukv, mla0_w_o),
        lambda h: dsa_mixer(h, positions, dsa1_w_in, dsa1_w_o),
        lambda h: rwkv7_mixer(h, rwkv2_mu, rwkv2_w_r, rwkv2_w_k, rwkv2_w_v, rwkv2_w_o,
                              rwkv2_w0, rwkv2_w1, rwkv2_w2, rwkv2_a0, rwkv2_a1, rwkv2_a2,
                              rwkv2_g1, rwkv2_g2, rwkv2_k_k, rwkv2_k_a, rwkv2_r_k,
                              rwkv2_lnx_w, rwkv2_lnx_b),
        lambda h: mla_mixer(h, positions, mla3_w_in, mla3_q_norm, mla3_w_uq,
                            mla3_kv_norm, mla3_w_ukv, mla3_w_o),
    ]
    ffns = [(ffn_norm_0, ffn_w_gu_0, ffn_w_down_0, mix_norm_0),
            (ffn_norm_1, ffn_w_gu_1, ffn_w_down_1, mix_norm_1),
            (ffn_norm_2, ffn_w_gu_2, ffn_w_down_2, mix_norm_2),
            (ffn_norm_3, ffn_w_gu_3, ffn_w_down_3, mix_norm_3)]
    for i in range(DEPTH):
        f_norm, w_gu, w_down, m_norm = ffns[i]
        x = x + FFN_RES * swiglu(rms_norm(x, f_norm[0]), w_gu[0], w_down[0])
        x = x + mixers[i](rms_norm(x, m_norm))
        x = x + FFN_RES * swiglu(rms_norm(x, f_norm[1]), w_gu[1], w_down[1])
    return rms_norm(x, final_norm)
```

```python
import functools

import numpy as np
import jax
import jax.numpy as jnp
from jax import lax
from jax.experimental import pallas as pl
from jax.experimental.pallas import tpu as pltpu

F32 = jnp.float32
BF16 = jnp.bfloat16
I32 = jnp.int32

CHUNK = 64
CHUNK_SHIFT = CHUNK.bit_length() - 1
ROPE_THETA = 10000.0
NORM_EPS = 1e-6
FFN_RES = 0.5
MLA_HEADS = 32
MLA_Q_LORA = 1024
MLA_KV_LORA = 512
MLA_NOPE = 128
MLA_ROPE = 64
MLA_V = 128
DSA_HEADS = 32
DSA_KV_HEADS = 8
DSA_HEAD_DIM = 128
IDX_HEADS = 32
IDX_DIM = 128
IDX_ROPE = 64
DSA_TOPK_MAX = 256
RWKV_HEAD = 64
GATE_LORA = 480
GN_EPS = 64e-5

LANES = 128
VMEM_LIMIT = 56 * 1024 * 1024
INT_MIN = -(2 ** 31)
NEG_BIG = -1e30

_NT = (((1,), (1,)), ((), ()))


def _cparams(*sem):
    return pltpu.CompilerParams(dimension_semantics=sem, vmem_limit_bytes=VMEM_LIMIT)


def _sigmoid(x):
    return 1.0 / (1.0 + jnp.exp(-x))


def _rmsnorm_kernel(x_ref, g_ref, o_ref):
    x = x_ref[...].astype(F32)
    ms = jnp.mean(x * x, axis=-1, keepdims=True)
    o_ref[...] = (x * lax.rsqrt(ms + NORM_EPS) * g_ref[...]).astype(o_ref.dtype)


def rmsnorm(x, g, *, width=None, col_block=0, out_dtype=BF16, rows=256):
    n = x.shape[0]
    w = x.shape[1] if width is None else width
    return pl.pallas_call(
        _rmsnorm_kernel,
        grid=(n // rows,),
        in_specs=[pl.BlockSpec((rows, w), lambda i: (i, col_block)),
                  pl.BlockSpec((1, w), lambda i: (0, 0))],
        out_specs=pl.BlockSpec((rows, w), lambda i: (i, 0)),
        out_shape=jax.ShapeDtypeStruct((n, w), out_dtype),
        compiler_params=_cparams("parallel"),
    )(x, g.reshape(1, w).astype(F32))


def _mm_kernel(*refs, act, alpha, has_res):
    if has_res:
        a_ref, w_ref, r_ref, o_ref = refs
    else:
        a_ref, w_ref, o_ref = refs
    acc = jnp.dot(a_ref[...], w_ref[...], preferred_element_type=F32)
    if act == "tanh":
        acc = jnp.tanh(acc)
    elif act == "sigmoid":
        acc = _sigmoid(acc)
    if has_res:
        acc = r_ref[...] + alpha * acc
    o_ref[...] = acc.astype(o_ref.dtype)


def _mm_tiles(m, k, n):
    bm = 1024 if k <= 4096 else 512
    bm = min(bm, m)
    bn = n
    for cand in (1024, 512, 256, 128):
        if n % cand == 0 and (cand * k * 2 * 2 + bm * k * 2 * 2 + bm * cand * 4 * 5) <= VMEM_LIMIT * 0.85:
            bn = cand
            break
    return bm, bn


def matmul(a, w, *, out_dtype=F32, act=None, res=None, alpha=1.0, bm=None, bn=None):
    m, k = a.shape
    n = w.shape[1]
    tbm, tbn = _mm_tiles(m, k, n)
    bm = tbm if bm is None else bm
    bn = tbn if bn is None else bn
    in_specs = [pl.BlockSpec((bm, k), lambda i, j: (i, 0)),
                pl.BlockSpec((k, bn), lambda i, j: (0, j))]
    args = [a, w]
    if res is not None:
        in_specs.append(pl.BlockSpec((bm, bn), lambda i, j: (i, j)))
        args.append(res)
    return pl.pallas_call(
        functools.partial(_mm_kernel, act=act, alpha=alpha, has_res=res is not None),
        grid=(m // bm, n // bn),
        in_specs=in_specs,
        out_specs=pl.BlockSpec((bm, bn), lambda i, j: (i, j)),
        out_shape=jax.ShapeDtypeStruct((m, n), out_dtype),
        compiler_params=_cparams("parallel", "arbitrary"),
    )(*args)


def _swiglu_kernel(a_ref, wg_ref, wu_ref, o_ref):
    a = a_ref[...]
    g = jnp.dot(a, wg_ref[...], preferred_element_type=F32)
    u = jnp.dot(a, wu_ref[...], preferred_element_type=F32)
    o_ref[...] = (g * _sigmoid(g) * u).astype(o_ref.dtype)


def swiglu_up(a, w_gu, *, bm=1024, bf=512):
    m, k = a.shape
    bm = min(bm, m)
    f = w_gu.shape[1] // 2
    nf = f // bf
    return pl.pallas_call(
        _swiglu_kernel,
        grid=(m // bm, nf),
        in_specs=[pl.BlockSpec((bm, k), lambda i, j: (i, 0)),
                  pl.BlockSpec((k, bf), lambda i, j: (0, j)),
                  pl.BlockSpec((k, bf), lambda i, j: (0, j + nf))],
        out_specs=pl.BlockSpec((bm, bf), lambda i, j: (i, j)),
        out_shape=jax.ShapeDtypeStruct((m, f), BF16),
        compiler_params=_cparams("parallel", "arbitrary"),
    )(a, w_gu, w_gu)


def ffn_half_step(x, g, w_gu, w_down):
    h = rmsnorm(x, g)
    act = swiglu_up(h, w_gu.astype(BF16))
    return matmul(act, w_down.astype(BF16), res=x, alpha=FFN_RES, bm=512, bn=512)


def _rope_lane_consts(rot_dim):
    lane = np.arange(LANES)
    half = rot_dim // 2
    if rot_dim == LANES:
        freq_idx = lane % half
        active = np.ones(LANES, bool)
    else:
        freq_idx = lane % half
        active = (lane % 64) < half
    inv = np.where(active, ROPE_THETA ** (-(2.0 * freq_idx) / rot_dim), 0.0)
    sign = np.where(active, np.where(lane < 64, -1.0, 1.0), 0.0)
    return (jnp.asarray(inv, F32).reshape(1, LANES), jnp.asarray(sign, F32).reshape(1, LANES))


def _rope_tab_kernel(pos_ref, inv_ref, sign_ref, c_ref, s_ref):
    ang = pos_ref[...].astype(F32) * inv_ref[...]
    c_ref[...] = jnp.cos(ang)
    s_ref[...] = jnp.sin(ang) * sign_ref[...]


def rope_tables(pos_col, rot_dim, rows=512):
    n = pos_col.shape[0]
    inv, sign = _rope_lane_consts(rot_dim)
    tab = jax.ShapeDtypeStruct((n, LANES), F32)
    return pl.pallas_call(
        _rope_tab_kernel,
        grid=(n // rows,),
        in_specs=[pl.BlockSpec((rows, 1), lambda i: (i, 0)),
                  pl.BlockSpec((1, LANES), lambda i: (0, 0)),
                  pl.BlockSpec((1, LANES), lambda i: (0, 0))],
        out_specs=[pl.BlockSpec((rows, LANES), lambda i: (i, 0))] * 2,
        out_shape=[tab, tab],
        compiler_params=_cparams("parallel"),
    )(pos_col, inv, sign)


def _rope(x, c, s):
    return x * c + pltpu.roll(x, 64, 1) * s


def _split_rope_perm():
    q = LANES // 4
    return np.concatenate([np.arange(0, q), np.arange(2 * q, 3 * q),
                           np.arange(q, 2 * q), np.arange(3 * q, 4 * q)])


def _mla_attn_kernel(q_ref, kn_ref, v_ref, kr_ref, cq_ref, sq_ref, ck_ref, sk_ref,
                     o_ref, kfull, *, tq, scale):
    h = pl.program_id(1)
    qi = pl.program_id(2)

    @pl.when(jnp.logical_and(h == 0, qi == 0))
    def _():
        kfull[:, LANES:] = _rope(kr_ref[...], ck_ref[...], sk_ref[...]).astype(BF16)

    @pl.when(qi == 0)
    def _():
        kfull[:, :LANES] = kn_ref[...]

    qr = _rope(q_ref[:, LANES:], cq_ref[...], sq_ref[...])
    q = jnp.concatenate([q_ref[:, :LANES], qr], axis=1).astype(BF16)
    q_chunk = (qi * tq + lax.broadcasted_iota(I32, (tq, tq), 0)) >> CHUNK_SHIFT
    k_iota = lax.broadcasted_iota(I32, (tq, tq), 1)

    def body(j, carry):
        m, l, acc = carry
        off = pl.multiple_of(j * tq, tq)
        kb = kfull[pl.ds(off, tq), :]
        s = lax.dot_general(q, kb, _NT, preferred_element_type=F32) * scale
        s = jnp.where(((off + k_iota) >> CHUNK_SHIFT) <= q_chunk, s, NEG_BIG)
        m_new = jnp.maximum(m, jnp.max(s, axis=-1, keepdims=True))
        alpha = jnp.exp(m - m_new)
        p = jnp.exp(s - m_new)
        l = alpha * l + jnp.sum(p, axis=-1, keepdims=True)
        acc = alpha * acc + jnp.dot(p.astype(BF16), v_ref[pl.ds(off, tq), :],
                                    preferred_element_type=F32)
        return m_new, l, acc

    init = (jnp.full((tq, 1), NEG_BIG, F32), jnp.zeros((tq, 1), F32),
            jnp.zeros((tq, MLA_V), F32))
    _, l, acc = lax.fori_loop(0, qi + 1, body, init)
    o_ref[...] = (acc / l).astype(o_ref.dtype)


def mla_mixer(h, tabs64, bsz, seq, w_in, q_norm, w_uq, kv_norm, w_ukv, w_o, x_res):
    n = h.shape[0]
    hq = MLA_HEADS
    lane = np.arange(LANES)
    rope_on = (lane % 64) < (MLA_ROPE // 2)
    rope_src = np.where(rope_on, (lane // 64) * (MLA_ROPE // 2) + lane % (MLA_ROPE // 2), 0)
    base = MLA_Q_LORA + MLA_KV_LORA
    kr_cols = w_in[:, base + rope_src] * jnp.asarray(rope_on, F32)
    w_in_p = jnp.concatenate([w_in[:, :base], kr_cols], axis=1).astype(BF16)
    hd = MLA_NOPE + MLA_ROPE
    idx = np.concatenate([np.concatenate([hh * hd + np.arange(MLA_NOPE),
                                          hh * hd + MLA_NOPE + rope_src]) for hh in range(hq)])
    msk = np.tile(np.concatenate([np.ones(MLA_NOPE, bool), rope_on]), hq)
    w_uq_p = (w_uq[:, idx] * jnp.asarray(msk, F32)).astype(BF16)
    w_ukv_p = w_ukv.reshape(MLA_KV_LORA, hq, 2, MLA_NOPE).transpose(0, 2, 1, 3)
    w_ukv_p = w_ukv_p.reshape(MLA_KV_LORA, 2 * hq * MLA_NOPE).astype(BF16)

    lat = matmul(h, w_in_p, bm=512, bn=w_in_p.shape[1])
    cq = rmsnorm(lat, q_norm, width=MLA_Q_LORA, col_block=0)
    ckv = rmsnorm(lat, kv_norm, width=MLA_KV_LORA, col_block=MLA_Q_LORA // MLA_KV_LORA)
    q = matmul(cq, w_uq_p)
    kv = matmul(ckv, w_ukv_p, out_dtype=BF16)

    tq = 256
    nq = seq // tq
    c64, s64 = tabs64
    kr_blk = base // LANES
    o = pl.pallas_call(
        functools.partial(_mla_attn_kernel, tq=tq, scale=float(hd) ** -0.5),
        grid=(bsz, hq, nq),
        in_specs=[pl.BlockSpec((tq, 2 * LANES), lambda b, hh, i: (b * nq + i, hh)),
                  pl.BlockSpec((seq, LANES), lambda b, hh, i: (b, hh)),
                  pl.BlockSpec((seq, LANES), lambda b, hh, i: (b, hq + hh)),
                  pl.BlockSpec((seq, LANES), lambda b, hh, i: (b, kr_blk)),
                  pl.BlockSpec((tq, LANES), lambda b, hh, i: (b * nq + i, 0)),
                  pl.BlockSpec((tq, LANES), lambda b, hh, i: (b * nq + i, 0)),
                  pl.BlockSpec((seq, LANES), lambda b, hh, i: (b, 0)),
                  pl.BlockSpec((seq, LANES), lambda b, hh, i: (b, 0))],
        out_specs=pl.BlockSpec((tq, LANES), lambda b, hh, i: (b * nq + i, hh)),
        out_shape=jax.ShapeDtypeStruct((n, hq * MLA_V), BF16),
        scratch_shapes=[pltpu.VMEM((seq, 2 * LANES), BF16)],
        compiler_params=_cparams("arbitrary", "arbitrary", "arbitrary"),
    )(q, kv, kv, lat, c64, s64, c64, s64)
    return matmul(o, w_o.astype(BF16), res=x_res, alpha=1.0)


def _rope_heads_kernel(x_ref, c_ref, s_ref, o_ref, *, groups):
    c = c_ref[...]
    s = s_ref[...]
    for g in range(groups):
        x = x_ref[:, g * LANES:(g + 1) * LANES]
        o_ref[0, g] = _rope(x, c, s).astype(o_ref.dtype)


def rope_heads(x, tabs, bsz, seq, *, col0, heads, groups, rows=512):
    nt = seq // rows
    cb = col0 // (LANES * groups)
    c, s = tabs
    return pl.pallas_call(
        functools.partial(_rope_heads_kernel, groups=groups),
        grid=(bsz, nt, heads // groups),
        in_specs=[pl.BlockSpec((rows, LANES * groups), lambda b, t, j: (b * nt + t, cb + j)),
                  pl.BlockSpec((rows, LANES), lambda b, t, j: (b * nt + t, 0)),
                  pl.BlockSpec((rows, LANES), lambda b, t, j: (b * nt + t, 0))],
        out_specs=pl.BlockSpec((1, groups, rows, LANES), lambda b, t, j: (b, j, t, 0)),
        out_shape=jax.ShapeDtypeStruct((bsz, heads, seq, LANES), BF16),
        compiler_params=_cparams("parallel", "parallel", "arbitrary"),
    )(x, c, s)


def _transpose_heads_kernel(x_ref, o_ref, *, groups):
    for g in range(groups):
        o_ref[0, g, 0] = x_ref[:, g * LANES:(g + 1) * LANES].T.astype(o_ref.dtype)


def transpose_heads(x, bsz, seq, *, col0, heads, tk):
    nt = seq // tk
    cb = col0 // (LANES * heads)
    return pl.pallas_call(
        functools.partial(_transpose_heads_kernel, groups=heads),
        grid=(bsz, nt),
        in_specs=[pl.BlockSpec((tk, LANES * heads), lambda b, t: (b * nt + t, cb))],
        out_specs=pl.BlockSpec((1, heads, 1, LANES, tk), lambda b, t: (b, 0, t, 0, 0)),
        out_shape=jax.ShapeDtypeStruct((bsz, heads, nt, LANES, tk), BF16),
        compiler_params=_cparams("parallel", "arbitrary"),
    )(x)


def _dsa_kernel(qi_ref, ki_ref, wi_ref, q_ref, k_ref, vt_ref, o_ref, key_scr,
                *, tq, tk, topk, rep, idx_scale, scale):
    qt = pl.program_id(1)
    nblk = ((qt + 1) * tq + tk - 1) // tk
    q_chunk = (qt * tq + lax.broadcasted_iota(I32, (tk, tq), 1)) >> CHUNK_SHIFT
    k_iota = lax.broadcasted_iota(I32, (tk, tq), 0)

    def idx_body(j, _):
        off = pl.multiple_of(j * tk, tk)
        kib = ki_ref[0, 0, pl.ds(off, tk), :]
        acc = jnp.zeros((tk, tq), F32)
        for hh in range(IDX_HEADS):
            d = lax.dot_general(kib, qi_ref[0, hh], _NT, preferred_element_type=F32)
            acc = acc + (wi_ref[0, hh:hh + 1, :] * idx_scale) * jnp.maximum(d, 0.0)
        bits = pltpu.bitcast(acc, I32)
        key = bits ^ ((bits >> 31) & 0x7FFFFFFF)
        valid = ((off + k_iota) >> CHUNK_SHIFT) <= q_chunk
        key_scr[pl.ds(off, tk), :] = jnp.where(valid, key, INT_MIN)
        return 0

    lax.fori_loop(0, nblk, idx_body, 0)

    def count_ge(cand):
        def cb(j, c):
            blk = key_scr[pl.ds(pl.multiple_of(j * tk, tk), tk), :]
            hit = jnp.where(blk >= cand, 1, 0).astype(I32)
            return c + jnp.sum(hit.reshape(tk // 8, 8, tq), axis=0)
        c8 = lax.fori_loop(0, nblk, cb, jnp.zeros((8, tq), I32))
        return jnp.sum(c8, axis=0, keepdims=True)

    thr = jnp.full((1, tq), INT_MIN, I32)
    thr = jnp.where(count_ge(jnp.zeros((1, tq), I32)) >= topk, 0, thr)

    def bit_body(i, t):
        cand = t + jnp.left_shift(jnp.int32(1), 30 - i)
        return jnp.where(count_ge(cand) >= topk, cand, t)

    thr = lax.fori_loop(0, 31, bit_body, thr)
    thr = jnp.maximum(thr, INT_MIN + 1)

    for g in range(DSA_KV_HEADS):
        qg = q_ref[0, g * rep:(g + 1) * rep].reshape(rep * tq, DSA_HEAD_DIM)

        def att_body(j, carry, g=g, qg=qg):
            m, l, acc = carry
            off = pl.multiple_of(j * tk, tk)
            kb = k_ref[0, g, pl.ds(off, tk), :]
            s = lax.dot_general(kb, qg, _NT, preferred_element_type=F32) * scale
            sel = key_scr[pl.ds(off, tk), :] >= thr
            s = jnp.where(jnp.concatenate([sel] * rep, axis=1), s, NEG_BIG)
            m_new = jnp.maximum(m, jnp.max(s, axis=0, keepdims=True))
            alpha = jnp.exp(m - m_new)
            p = jnp.exp(s - m_new)
            l = alpha * l + jnp.sum(p, axis=0, keepdims=True)
            acc = alpha * acc + jnp.dot(vt_ref[0, g, j], p.astype(BF16),
                                        preferred_element_type=F32)
            return m_new, l, acc

        init = (jnp.full((1, rep * tq), NEG_BIG, F32), jnp.zeros((1, rep * tq), F32),
                jnp.zeros((DSA_HEAD_DIM, rep * tq), F32))
        _, l, acc = lax.fori_loop(0, nblk, att_body, init)
        ot = acc / l
        for r in range(rep):
            hh = g * rep + r
            o_ref[:, hh * DSA_HEAD_DIM:(hh + 1) * DSA_HEAD_DIM] = (
                ot[:, r * tq:(r + 1) * tq].T.astype(o_ref.dtype))


def dsa_mixer(h, tabs64, tabs128, bsz, seq, w_in, w_o, x_res):
    n = h.shape[0]
    hq, g, hd = DSA_HEADS, DSA_KV_HEADS, DSA_HEAD_DIM
    rep = hq // g
    topk = min(DSA_TOPK_MAX, seq // 4)
    n_q, n_kv, n_qi = hq * hd, g * hd, IDX_HEADS * IDX_DIM
    main = n_q + 2 * n_kv + n_qi
    perm = _split_rope_perm()
    qi_idx = np.concatenate([n_q + 2 * n_kv + hh * IDX_DIM + perm for hh in range(IDX_HEADS)])
    col_idx = np.concatenate([np.arange(n_q + 2 * n_kv), qi_idx])
    w_main = w_in[:, col_idx].astype(BF16)
    small_idx = np.concatenate([main + perm, main + IDX_DIM + np.arange(IDX_HEADS)])
    w_small = jnp.pad(w_in[:, small_idx], ((0, 0), (0, 2 * LANES - small_idx.size))).astype(BF16)

    proj = matmul(h, w_main)
    small = matmul(h, w_small, bn=2 * LANES)

    tq, tk = 128, 256
    q_hm = rope_heads(proj, tabs128, bsz, seq, col0=0, heads=hq, groups=8)
    k_hm = rope_heads(proj, tabs128, bsz, seq, col0=n_q, heads=g, groups=8)
    vt = transpose_heads(proj, bsz, seq, col0=n_q + n_kv, heads=g, tk=tk)
    qi_hm = rope_heads(proj, tabs64, bsz, seq, col0=n_q + 2 * n_kv, heads=IDX_HEADS, groups=8)
    ki_hm = rope_heads(small, tabs64, bsz, seq, col0=0, heads=1, groups=1)
    wi_t = small[:, LANES:LANES + IDX_HEADS].reshape(bsz, seq, IDX_HEADS).transpose(0, 2, 1)

    nq = seq // tq
    o = pl.pallas_call(
        functools.partial(_dsa_kernel, tq=tq, tk=tk, topk=topk, rep=rep,
                          idx_scale=float(IDX_HEADS) ** -0.5 * float(IDX_DIM) ** -0.5,
                          scale=float(hd) ** -0.5),
        grid=(bsz, nq),
        in_specs=[pl.BlockSpec((1, IDX_HEADS, tq, LANES), lambda b, i: (b, 0, i, 0)),
                  pl.BlockSpec((1, 1, seq, LANES), lambda b, i: (b, 0, 0, 0)),
                  pl.BlockSpec((1, IDX_HEADS, tq), lambda b, i: (b, 0, i)),
                  pl.BlockSpec((1, hq, tq, LANES), lambda b, i: (b, 0, i, 0)),
                  pl.BlockSpec((1, g, seq, LANES), lambda b, i: (b, 0, 0, 0)),
                  pl.BlockSpec((1, g, seq // tk, LANES, tk), lambda b, i: (b, 0, 0, 0, 0))],
        out_specs=pl.BlockSpec((tq, hq * hd), lambda b, i: (b * nq + i, 0)),
        out_shape=jax.ShapeDtypeStruct((n, hq * hd), BF16),
        scratch_shapes=[pltpu.VMEM((seq, tq), I32)],
        compiler_params=_cparams("parallel", "arbitrary"),
    )(qi_hm, ki_hm, wi_t, q_hm, k_hm, vt)
    return matmul(o, w_o.astype(BF16), res=x_res, alpha=1.0)


def _mix_kernel(h_ref, hp_ref, mu_ref, *o_refs):
    h = h_ref[...].astype(F32)
    xx = hp_ref[...].astype(F32) - h
    for i, o_ref in enumerate(o_refs):
        o_ref[...] = (h + xx * mu_ref[i:i + 1, :]).astype(o_ref.dtype)


def token_shift_mix(h, h_prev, mu, rows=128):
    n, d = h.shape
    k = mu.shape[0]
    blk = pl.BlockSpec((rows, d), lambda i: (i, 0))
    return pl.pallas_call(
        _mix_kernel,
        grid=(n // rows,),
        in_specs=[blk, blk, pl.BlockSpec((k, d), lambda i: (0, 0))],
        out_specs=[blk] * k,
        out_shape=[jax.ShapeDtypeStruct((n, d), BF16)] * k,
        compiler_params=_cparams("parallel"),
    )(h, h_prev, mu)


def _split3(x):
    hi = x.astype(BF16)
    r1 = x - hi.astype(F32)
    mid = r1.astype(BF16)
    lo = (r1 - mid.astype(F32)).astype(BF16)
    return hi, mid, lo


def _dot3_right(x, w_bf16):
    hi, mid, lo = _split3(x)
    d = functools.partial(jnp.dot, preferred_element_type=F32)
    return d(hi, w_bf16) + d(mid, w_bf16) + d(lo, w_bf16)


def _dot3_left(w_bf16, x):
    hi, mid, lo = _split3(x)
    d = functools.partial(jnp.dot, preferred_element_type=F32)
    return d(w_bf16, hi) + d(w_bf16, mid) + d(w_bf16, lo)


def _wkv_kernel(r_ref, k_ref, v_ref, wl_ref, al_ref, g_ref,
                w0_ref, a0_ref, kk_ref, ka_ref, rk_ref, lnw_ref, lnb_ref,
                o_ref, state, *, L, pairs):
    c_idx = pl.program_id(2)

    @pl.when(c_idx == 0)
    def _():
        state[...] = jnp.zeros_like(state)

    hn = RWKV_HEAD
    hshift = hn.bit_length() - 1
    li = lax.broadcasted_iota(I32, (LANES, LANES), 0)
    lj = lax.broadcasted_iota(I32, (LANES, LANES), 1)
    same_head = (li >> hshift) == (lj >> hshift)
    seg_avg = jnp.where(same_head, 1.0 / hn, 0.0).astype(BF16)
    ti = lax.broadcasted_iota(I32, (L, L), 0)
    tj = lax.broadcasted_iota(I32, (L, L), 1)
    tril_incl = jnp.where(ti >= tj, 1.0, 0.0).astype(BF16)
    hr = lax.broadcasted_iota(I32, (L, LANES), 0)
    hc = lax.broadcasted_iota(I32, (L, LANES), 1)
    left = hc < hn
    strict = hr > (hc & (hn - 1))
    incl = hr >= (hc & (hn - 1))

    r_all = r_ref[...]
    k_all = k_ref[...]
    v_all = v_ref[...]
    z = w0_ref[...] + wl_ref[...]
    u = -z
    softplus = jnp.maximum(u, 0.0) + jnp.log(1.0 + jnp.exp(-jnp.abs(u)))
    w_log = -softplus - 0.5
    lw = -jnp.exp(w_log)
    a_sig = _sigmoid(a0_ref[...] + al_ref[...])
    k2_all = k_all * (1.0 + (a_sig - 1.0) * ka_ref[...])
    kk_all = k_all * kk_ref[...]
    csum = _dot3_left(tril_incl, lw)
    c_last = csum[L - 1:L, :]
    e_c = jnp.exp(csum)
    e_cprev = jnp.exp(csum - lw)
    e_neg = jnp.exp(-csum)
    e_rem = jnp.exp(c_last - csum)
    gam_all = jnp.exp(c_last)

    bdot = functools.partial(jnp.dot, preferred_element_type=F32)
    for p in range(pairs):
        sl = slice(p * LANES, (p + 1) * LANES)
        r, k2, v = r_all[:, sl], k2_all[:, sl], v_all[:, sl]
        kk = kk_all[:, sl]
        nrm = jnp.sqrt(hn * _dot3_right(kk * kk, seg_avg))
        kk = kk / jnp.maximum(nrm, 1e-12)
        bvec = kk * a_sig[:, sl]
        at = -kk * e_cprev[:, sl]
        rt = r * e_c[:, sl]
        bt = (bvec * e_neg[:, sl]).astype(BF16)
        kt = (k2 * e_neg[:, sl]).astype(BF16)
        bh = (bvec * e_rem[:, sl]).astype(BF16)
        kh = (k2 * e_rem[:, sl]).astype(BF16)

        a0 = jnp.where(left, at, 0.0).astype(BF16)
        a1 = jnp.where(left, 0.0, at).astype(BF16)
        r0 = jnp.where(left, rt, 0.0).astype(BF16)
        r1 = jnp.where(left, 0.0, rt).astype(BF16)
        v0 = jnp.where(left, v, 0.0)
        v1 = jnp.where(left, 0.0, v)
        vsw = jnp.concatenate([v1, v0], axis=0).astype(BF16)

        o0 = lax.dot_general(jnp.concatenate([a0, r0], axis=0), jnp.concatenate([bt, kt], axis=0),
                             _NT, preferred_element_type=F32)
        o1 = lax.dot_general(jnp.concatenate([a1, r1], axis=0), jnp.concatenate([kt, bt], axis=0),
                             _NT, preferred_element_type=F32)
        u0 = jnp.where(strict, o0[:L], 0.0)
        u1 = jnp.where(strict, o1[:L], 0.0)
        y0 = jnp.where(incl, o0[L:], 0.0)
        y1 = jnp.where(incl, o1[L:], 0.0)
        m_ab = jnp.concatenate([jnp.where(left, u0, 0.0), jnp.where(left, 0.0, u1)], axis=0)
        m_ak = jnp.concatenate([jnp.where(left, 0.0, u0), jnp.where(left, u1, 0.0)], axis=0)
        m_rb = jnp.concatenate([jnp.where(left, y0, 0.0), jnp.where(left, 0.0, y1)], axis=0)
        m_rk = jnp.concatenate([jnp.where(left, 0.0, y0), jnp.where(left, y1, 0.0)], axis=0)

        st = state[p]
        arst = lax.dot_general(jnp.concatenate([a0, a1, r0, r1], axis=0), st.astype(BF16),
                               _NT, preferred_element_type=F32)
        x = arst[:2 * L] + bdot(m_ak.astype(BF16), vsw)
        mp = m_ab
        steps = (L - 1).bit_length()
        for it in range(steps):
            mpb = mp.astype(BF16)
            x = x + bdot(mpb, x.astype(BF16))
            if it + 1 < steps:
                mp = bdot(mpb, mpb)
        ys = arst[2 * L:] + bdot(jnp.concatenate([m_rb, m_rk], axis=1).astype(BF16),
                                 jnp.concatenate([x.astype(BF16), vsw], axis=0))
        y = ys[:L] + ys[L:]
        sa = x[:L] + x[L:]
        sv_t = jnp.concatenate([sa, v], axis=0).T.astype(BF16)
        upd = bdot(sv_t, jnp.concatenate([bh, kh], axis=0))
        state[p] = st * gam_all[:, sl] + jnp.where(same_head, upd, 0.0)

        mean = _dot3_right(y, seg_avg)
        yc = y - mean
        var = _dot3_right(yc * yc, seg_avg)
        yn = yc * lax.rsqrt(var + GN_EPS) * lnw_ref[:, sl] + lnb_ref[:, sl]
        bonus = hn * _dot3_right(r * k2 * rk_ref[:, sl], seg_avg) * v
        o_ref[:, sl] = ((yn + bonus) * g_ref[:, sl]).astype(o_ref.dtype)


def rwkv7_mixer(h, bsz, seq, mu, w_r, w_k, w_v, w_o, w0, w1, w2, a0, a1, a2, g1, g2,
                k_k, k_a, r_k, lnx_w, lnx_b, x_res):
    n, d = h.shape
    h3 = h.reshape(bsz, seq, d)
    h_prev = jnp.pad(h3, ((0, 0), (1, 0), (0, 0)))[:, :-1].reshape(n, d)
    xr, xw, xk, xv, xa, xg = token_shift_mix(h, h_prev, mu)
    r = matmul(xr, w_r.astype(BF16))
    k = matmul(xk, w_k.astype(BF16))
    v = matmul(xv, w_v.astype(BF16))
    tw = matmul(xw, w1.astype(BF16), out_dtype=BF16, act="tanh", bn=w1.shape[1])
    wl = matmul(tw, w2.astype(BF16))
    ta = matmul(xa, a1.astype(BF16), out_dtype=BF16, bn=a1.shape[1])
    al = matmul(ta, a2.astype(BF16))
    gpad = (-GATE_LORA) % LANES
    g1p = jnp.pad(g1, ((0, 0), (0, gpad))).astype(BF16)
    g2p = jnp.pad(g2, ((0, gpad), (0, 0))).astype(BF16)
    tg = matmul(xg, g1p, out_dtype=BF16, act="sigmoid", bn=g1p.shape[1])
    gate = matmul(tg, g2p)

    L, pairs = CHUNK, 4
    nc = seq // L
    wide = LANES * pairs
    row = lambda a: a.reshape(1, d).astype(F32)
    dat = pl.BlockSpec((L, wide), lambda b, j, c: (b * nc + c, j))
    par = pl.BlockSpec((1, wide), lambda b, j, c: (0, j))
    yg = pl.pallas_call(
        functools.partial(_wkv_kernel, L=L, pairs=pairs),
        grid=(bsz, d // wide, nc),
        in_specs=[dat] * 6 + [par] * 7,
        out_specs=dat,
        out_shape=jax.ShapeDtypeStruct((n, d), BF16),
        scratch_shapes=[pltpu.VMEM((pairs, LANES, LANES), F32)],
        compiler_params=_cparams("parallel", "parallel", "arbitrary"),
    )(r, k, v, wl, al, gate, row(w0), row(a0), row(k_k), row(k_a), row(r_k), row(lnx_w), row(lnx_b))
    return matmul(yg, w_o.astype(BF16), res=x_res, alpha=1.0)


def kernel(x, positions, ffn_norm_0, ffn_w_gu_0, ffn_w_down_0, mix_norm_0, mla0_w_in, mla0_q_norm, mla0_w_uq, mla0_kv_norm, mla0_w_ukv, mla0_w_o, ffn_norm_1, ffn_w_gu_1, ffn_w_down_1, mix_norm_1, dsa1_w_in, dsa1_w_o, ffn_norm_2, ffn_w_gu_2, ffn_w_down_2, mix_norm_2, rwkv2_mu, rwkv2_w_r, rwkv2_w_k, rwkv2_w_v, rwkv2_w_o, rwkv2_w0, rwkv2_w1, rwkv2_w2, rwkv2_a0, rwkv2_a1, rwkv2_a2, rwkv2_g1, rwkv2_g2, rwkv2_k_k, rwkv2_k_a, rwkv2_r_k, rwkv2_lnx_w, rwkv2_lnx_b, ffn_norm_3, ffn_w_gu_3, ffn_w_down_3, mix_norm_3, mla3_w_in, mla3_q_norm, mla3_w_uq, mla3_kv_norm, mla3_w_ukv, mla3_w_o, final_norm):
    bsz, seq, d = x.shape
    n = bsz * seq
    xf = x.reshape(n, d)
    pos_col = positions.reshape(n, 1)
    tabs64 = rope_tables(pos_col, MLA_ROPE)
    tabs128 = rope_tables(pos_col, DSA_HEAD_DIM)

    def mla(hn, xr, w_in, q_norm, w_uq, kv_norm, w_ukv, w_o):
        return mla_mixer(hn, tabs64, bsz, seq, w_in, q_norm, w_uq, kv_norm, w_ukv, w_o, xr)

    mixers = [
        lambda hn, xr: mla(hn, xr, mla0_w_in, mla0_q_norm, mla0_w_uq, mla0_kv_norm, mla0_w_ukv, mla0_w_o),
        lambda hn, xr: dsa_mixer(hn, tabs64, tabs128, bsz, seq, dsa1_w_in, dsa1_w_o, xr),
        lambda hn, xr: rwkv7_mixer(hn, bsz, seq, rwkv2_mu, rwkv2_w_r, rwkv2_w_k, rwkv2_w_v, rwkv2_w_o,
                                   rwkv2_w0, rwkv2_w1, rwkv2_w2, rwkv2_a0, rwkv2_a1, rwkv2_a2,
                                   rwkv2_g1, rwkv2_g2, rwkv2_k_k, rwkv2_k_a,
                                   rwkv2_r_k.reshape(-1), rwkv2_lnx_w, rwkv2_lnx_b, xr),
        lambda hn, xr: mla(hn, xr, mla3_w_in, mla3_q_norm, mla3_w_uq, mla3_kv_norm, mla3_w_ukv, mla3_w_o),
    ]
    ffns = [(ffn_norm_0, ffn_w_gu_0, ffn_w_down_0, mix_norm_0),
            (ffn_norm_1, ffn_w_gu_1, ffn_w_down_1, mix_norm_1),
            (ffn_norm_2, ffn_w_gu_2, ffn_w_down_2, mix_norm_2),
            (ffn_norm_3, ffn_w_gu_3, ffn_w_down_3, mix_norm_3)]
    for i in range(4):
        f_norm, w_gu, w_down, m_norm = ffns[i]
        xf = ffn_half_step(xf, f_norm[0], w_gu[0], w_down[0])
        mix_in_dtype = F32 if i == 2 else BF16
        xf = mixers[i](rmsnorm(xf, m_norm, out_dtype=mix_in_dtype), xf)
        xf = ffn_half_step(xf, f_norm[1], w_gu[1], w_down[1])
    return rmsnorm(xf, final_norm, out_dtype=F32).reshape(bsz, seq, d)
```

```python
import functools

import numpy as np
import jax
import jax.numpy as jnp
from jax import lax
from jax.experimental import pallas as pl
from jax.experimental.pallas import tpu as pltpu

F32 = jnp.float32
BF16 = jnp.bfloat16
I32 = jnp.int32

CHUNK = 64
CHUNK_SHIFT = CHUNK.bit_length() - 1
ROPE_THETA = 10000.0
NORM_EPS = 1e-6
FFN_RES = 0.5
MLA_HEADS = 32
MLA_Q_LORA = 1024
MLA_KV_LORA = 512
MLA_NOPE = 128
MLA_ROPE = 64
MLA_V = 128
DSA_HEADS = 32
DSA_KV_HEADS = 8
DSA_HEAD_DIM = 128
IDX_HEADS = 32
IDX_DIM = 128
IDX_ROPE = 64
DSA_TOPK_MAX = 256
RWKV_HEAD = 64
GATE_LORA = 480
GN_EPS = 64e-5

LANES = 128
VMEM_LIMIT = 56 * 1024 * 1024
INT_MIN = -(2 ** 31)
NEG_BIG = -1e30

_NT = (((1,), (1,)), ((), ()))


def _cparams(*sem):
    return pltpu.CompilerParams(dimension_semantics=sem, vmem_limit_bytes=VMEM_LIMIT)


def _sigmoid(x):
    return 1.0 / (1.0 + jnp.exp(-x))


def _rmsnorm_kernel(x_ref, g_ref, o_ref):
    x = x_ref[...].astype(F32)
    ms = jnp.mean(x * x, axis=-1, keepdims=True)
    o_ref[...] = (x * lax.rsqrt(ms + NORM_EPS) * g_ref[...]).astype(o_ref.dtype)


def rmsnorm(x, g, *, width=None, col_block=0, out_dtype=BF16, rows=256):
    n = x.shape[0]
    w = x.shape[1] if width is None else width
    return pl.pallas_call(
        _rmsnorm_kernel,
        grid=(n // rows,),
        in_specs=[pl.BlockSpec((rows, w), lambda i: (i, col_block)),
                  pl.BlockSpec((1, w), lambda i: (0, 0))],
        out_specs=pl.BlockSpec((rows, w), lambda i: (i, 0)),
        out_shape=jax.ShapeDtypeStruct((n, w), out_dtype),
        compiler_params=_cparams("parallel"),
    )(x, g.reshape(1, w).astype(F32))


def _mm_kernel(*refs, act, alpha, has_res):
    if has_res:
        a_ref, w_ref, r_ref, o_ref = refs
    else:
        a_ref, w_ref, o_ref = refs
    acc = jnp.dot(a_ref[...], w_ref[...], preferred_element_type=F32)
    if act == "tanh":
        acc = jnp.tanh(acc)
    elif act == "sigmoid":
        acc = _sigmoid(acc)
    if has_res:
        acc = r_ref[...] + alpha * acc
    o_ref[...] = acc.astype(o_ref.dtype)


def _mm_tiles(m, k, n):
    bm = 1024 if k <= 4096 else 512
    bm = min(bm, m)
    bn = n
    for cand in (1024, 512, 256, 128):
        if n % cand == 0 and (cand * k * 2 * 2 + bm * k * 2 * 2 + bm * cand * 4 * 5) <= VMEM_LIMIT * 0.85:
            bn = cand
            break
    return bm, bn


def matmul(a, w, *, out_dtype=F32, act=None, res=None, alpha=1.0, bm=None, bn=None, w_index=None):
    m, k = a.shape
    n = w.shape[-1]
    tbm, tbn = _mm_tiles(m, k, n)
    bm = tbm if bm is None else bm
    bn = tbn if bn is None else bn
    if w.ndim == 3:
        w_spec = pl.BlockSpec((None, k, bn), lambda i, j: (w_index, 0, j))
    else:
        w_spec = pl.BlockSpec((k, bn), lambda i, j: (0, j))
    in_specs = [pl.BlockSpec((bm, k), lambda i, j: (i, 0)), w_spec]
    args = [a, w]
    if res is not None:
        in_specs.append(pl.BlockSpec((bm, bn), lambda i, j: (i, j)))
        args.append(res)
    return pl.pallas_call(
        functools.partial(_mm_kernel, act=act, alpha=alpha, has_res=res is not None),
        grid=(m // bm, n // bn),
        in_specs=in_specs,
        out_specs=pl.BlockSpec((bm, bn), lambda i, j: (i, j)),
        out_shape=jax.ShapeDtypeStruct((m, n), out_dtype),
        compiler_params=_cparams("parallel", "arbitrary"),
    )(*args)


def _swiglu_kernel(a_ref, wg_ref, wu_ref, o_ref):
    a = a_ref[...]
    g = jnp.dot(a, wg_ref[...], preferred_element_type=F32)
    u = jnp.dot(a, wu_ref[...], preferred_element_type=F32)
    o_ref[...] = (g * _sigmoid(g) * u).astype(o_ref.dtype)


def swiglu_up(a, w_gu, half, *, bm=1024, bf=512):
    m, k = a.shape
    bm = min(bm, m)
    f = w_gu.shape[-1] // 2
    nf = f // bf
    return pl.pallas_call(
        _swiglu_kernel,
        grid=(m // bm, nf),
        in_specs=[pl.BlockSpec((bm, k), lambda i, j: (i, 0)),
                  pl.BlockSpec((None, k, bf), lambda i, j: (half, 0, j)),
                  pl.BlockSpec((None, k, bf), lambda i, j: (half, 0, j + nf))],
        out_specs=pl.BlockSpec((bm, bf), lambda i, j: (i, j)),
        out_shape=jax.ShapeDtypeStruct((m, f), BF16),
        compiler_params=_cparams("parallel", "arbitrary"),
    )(a, w_gu, w_gu)


def ffn_half_step(x, g, w_gu, w_down, half):
    h = rmsnorm(x, g)
    act = swiglu_up(h, w_gu, half)
    return matmul(act, w_down, res=x, alpha=FFN_RES, bm=min(1024, x.shape[0]), bn=512, w_index=half)


def _rope_lane_consts(rot_dim):
    lane = np.arange(LANES)
    half = rot_dim // 2
    if rot_dim == LANES:
        freq_idx = lane % half
        active = np.ones(LANES, bool)
    else:
        freq_idx = lane % half
        active = (lane % 64) < half
    inv = np.where(active, ROPE_THETA ** (-(2.0 * freq_idx) / rot_dim), 0.0)
    sign = np.where(active, np.where(lane < 64, -1.0, 1.0), 0.0)
    return (jnp.asarray(inv, F32).reshape(1, LANES), jnp.asarray(sign, F32).reshape(1, LANES))


def _rope_tab_kernel(pos_ref, inv_ref, sign_ref, c_ref, s_ref):
    ang = pos_ref[...].astype(F32) * inv_ref[...]
    c_ref[...] = jnp.cos(ang)
    s_ref[...] = jnp.sin(ang) * sign_ref[...]


def rope_tables(pos_col, rot_dim, rows=512):
    n = pos_col.shape[0]
    inv, sign = _rope_lane_consts(rot_dim)
    tab = jax.ShapeDtypeStruct((n, LANES), F32)
    return pl.pallas_call(
        _rope_tab_kernel,
        grid=(n // rows,),
        in_specs=[pl.BlockSpec((rows, 1), lambda i: (i, 0)),
                  pl.BlockSpec((1, LANES), lambda i: (0, 0)),
                  pl.BlockSpec((1, LANES), lambda i: (0, 0))],
        out_specs=[pl.BlockSpec((rows, LANES), lambda i: (i, 0))] * 2,
        out_shape=[tab, tab],
        compiler_params=_cparams("parallel"),
    )(pos_col, inv, sign)


def _rope(x, c, s):
    return x * c + pltpu.roll(x, 64, 1) * s


def _split_rope_perm():
    q = LANES // 4
    return np.concatenate([np.arange(0, q), np.arange(2 * q, 3 * q),
                           np.arange(q, 2 * q), np.arange(3 * q, 4 * q)])


def _mla_attn_kernel(q_ref, kn_ref, v_ref, kr_ref, cq_ref, sq_ref, ck_ref, sk_ref,
                     o_ref, kfull, krs, *, tq, nq, hb, scale):
    hg = pl.program_id(1)
    qi = pl.program_id(2)

    @pl.when(jnp.logical_and(hg == 0, qi == 0))
    def _():
        krs[...] = _rope(kr_ref[...], ck_ref[...], sk_ref[...]).astype(BF16)

    @pl.when(qi == 0)
    def _():
        for hd in range(hb):
            kfull[hd, :, :LANES] = kn_ref[:, hd * LANES:(hd + 1) * LANES]
            kfull[hd, :, LANES:] = krs[...]

    cq = cq_ref[...]
    sq = sq_ref[...]
    diag_ok = ((lax.broadcasted_iota(I32, (tq, tq), 1) >> CHUNK_SHIFT)
               <= (lax.broadcasted_iota(I32, (tq, tq), 0) >> CHUNK_SHIFT))

    for i in range(nq):
        @pl.when(qi == i)
        def _(i=i):
            kv_len = (i + 1) * tq
            for hd in range(hb):
                base = hd * 2 * LANES
                qr = _rope(q_ref[:, base + LANES:base + 2 * LANES], cq, sq)
                q = jnp.concatenate([q_ref[:, base:base + LANES], qr], axis=1).astype(BF16)
                s = lax.dot_general(q, kfull[hd, :kv_len, :], _NT,
                                    preferred_element_type=F32) * scale
                s_diag = jnp.where(diag_ok, s[:, kv_len - tq:], NEG_BIG)
                s = s_diag if i == 0 else jnp.concatenate([s[:, :kv_len - tq], s_diag], axis=1)
                m = jnp.max(s, axis=-1, keepdims=True)
                p = jnp.exp(s - m)
                l = jnp.sum(p, axis=-1, keepdims=True)
                o = jnp.dot(p.astype(BF16), v_ref[:kv_len, hd * LANES:(hd + 1) * LANES],
                            preferred_element_type=F32)
                o_ref[:, hd * LANES:(hd + 1) * LANES] = (o / l).astype(o_ref.dtype)


def mla_mixer(h, tabs64, bsz, seq, w_in, q_norm, w_uq, kv_norm, w_ukv, w_o, x_res):
    n = h.shape[0]
    hq = MLA_HEADS
    lane = np.arange(LANES)
    rope_on = (lane % 64) < (MLA_ROPE // 2)
    rope_src = np.where(rope_on, (lane // 64) * (MLA_ROPE // 2) + lane % (MLA_ROPE // 2), 0)
    base = MLA_Q_LORA + MLA_KV_LORA
    kr_cols = w_in[:, base + rope_src] * jnp.asarray(rope_on, F32)
    w_in_p = jnp.concatenate([w_in[:, :base], kr_cols], axis=1).astype(BF16)
    hd = MLA_NOPE + MLA_ROPE
    idx = np.concatenate([np.concatenate([hh * hd + np.arange(MLA_NOPE),
                                          hh * hd + MLA_NOPE + rope_src]) for hh in range(hq)])
    msk = np.tile(np.concatenate([np.ones(MLA_NOPE, bool), rope_on]), hq)
    w_uq_p = (w_uq[:, idx] * jnp.asarray(msk, F32)).astype(BF16)
    w_ukv_p = w_ukv.reshape(MLA_KV_LORA, hq, 2, MLA_NOPE).transpose(0, 2, 1, 3)
    w_ukv_p = w_ukv_p.reshape(MLA_KV_LORA, 2 * hq * MLA_NOPE).astype(BF16)

    lat = matmul(h, w_in_p, bm=512, bn=w_in_p.shape[1])
    cq = rmsnorm(lat, q_norm, width=MLA_Q_LORA, col_block=0)
    ckv = rmsnorm(lat, kv_norm, width=MLA_KV_LORA, col_block=MLA_Q_LORA // MLA_KV_LORA)
    q = matmul(cq, w_uq_p)
    kv = matmul(ckv, w_ukv_p, out_dtype=BF16)

    tq, hb = 256, 4
    nq = seq // tq
    ng = hq // hb
    c64, s64 = tabs64
    kr_blk = base // LANES
    o = pl.pallas_call(
        functools.partial(_mla_attn_kernel, tq=tq, nq=nq, hb=hb, scale=float(hd) ** -0.5),
        grid=(bsz, ng, nq),
        in_specs=[pl.BlockSpec((tq, 2 * LANES * hb), lambda b, hh, i: (b * nq + i, hh)),
                  pl.BlockSpec((seq, LANES * hb), lambda b, hh, i: (b, hh)),
                  pl.BlockSpec((seq, LANES * hb), lambda b, hh, i: (b, ng + hh)),
                  pl.BlockSpec((seq, LANES), lambda b, hh, i: (b, kr_blk)),
                  pl.BlockSpec((tq, LANES), lambda b, hh, i: (b * nq + i, 0)),
                  pl.BlockSpec((tq, LANES), lambda b, hh, i: (b * nq + i, 0)),
                  pl.BlockSpec((seq, LANES), lambda b, hh, i: (b, 0)),
                  pl.BlockSpec((seq, LANES), lambda b, hh, i: (b, 0))],
        out_specs=pl.BlockSpec((tq, LANES * hb), lambda b, hh, i: (b * nq + i, hh)),
        out_shape=jax.ShapeDtypeStruct((n, hq * MLA_V), BF16),
        scratch_shapes=[pltpu.VMEM((hb, seq, 2 * LANES), BF16),
                        pltpu.VMEM((seq, LANES), BF16)],
        compiler_params=_cparams("arbitrary", "arbitrary", "arbitrary"),
    )(q, kv, kv, lat, c64, s64, c64, s64)
    return matmul(o, w_o.astype(BF16), res=x_res, alpha=1.0)


def _rope_heads_kernel(x_ref, c_ref, s_ref, o_ref, *, groups):
    c = c_ref[...]
    s = s_ref[...]
    for g in range(groups):
        x = x_ref[:, g * LANES:(g + 1) * LANES]
        o_ref[0, g] = _rope(x, c, s).astype(o_ref.dtype)


def rope_heads(x, tabs, bsz, seq, *, col0, heads, groups, rows=512):
    nt = seq // rows
    cb = col0 // (LANES * groups)
    c, s = tabs
    return pl.pallas_call(
        functools.partial(_rope_heads_kernel, groups=groups),
        grid=(bsz, nt, heads // groups),
        in_specs=[pl.BlockSpec((rows, LANES * groups), lambda b, t, j: (b * nt + t, cb + j)),
                  pl.BlockSpec((rows, LANES), lambda b, t, j: (b * nt + t, 0)),
                  pl.BlockSpec((rows, LANES), lambda b, t, j: (b * nt + t, 0))],
        out_specs=pl.BlockSpec((1, groups, rows, LANES), lambda b, t, j: (b, j, t, 0)),
        out_shape=jax.ShapeDtypeStruct((bsz, heads, seq, LANES), BF16),
        compiler_params=_cparams("parallel", "parallel", "arbitrary"),
    )(x, c, s)


def _transpose_heads_kernel(x_ref, o_ref, *, groups):
    for g in range(groups):
        o_ref[0, g] = x_ref[:, g * LANES:(g + 1) * LANES].T.astype(o_ref.dtype)


def transpose_heads(x, bsz, seq, *, col0, heads, rows=256):
    nt = seq // rows
    cb = col0 // (LANES * heads)
    return pl.pallas_call(
        functools.partial(_transpose_heads_kernel, groups=heads),
        grid=(bsz, nt),
        in_specs=[pl.BlockSpec((rows, LANES * heads), lambda b, t: (b * nt + t, cb))],
        out_specs=pl.BlockSpec((1, heads, LANES, rows), lambda b, t: (b, 0, 0, t)),
        out_shape=jax.ShapeDtypeStruct((bsz, heads, LANES, seq), BF16),
        compiler_params=_cparams("parallel", "arbitrary"),
    )(x)


def _dsa_kernel(qi_ref, ki_ref, wi_ref, q_ref, k_ref, vt_ref, o_ref, key_scr, ot_scr,
                *, tq, tk, nq, topk, rep, idx_scale, scale):
    qt = pl.program_id(1)
    q_chunk = (qt * tq + lax.broadcasted_iota(I32, (tk, tq), 1)) >> CHUNK_SHIFT
    k_iota = lax.broadcasted_iota(I32, (tk, tq), 0)
    groups = DSA_KV_HEADS

    for nb in range(nq * tq // tk):
        @pl.when((qt * tq) // tk == nb)
        def _(nb=nb):
            nblk = nb + 1
            kv_len = nblk * tk

            def idx_body(j, _):
                off = pl.multiple_of(j * tk, tk)
                kib = ki_ref[0, 0, pl.ds(off, tk), :]
                acc = jnp.zeros((tk, tq), F32)
                for hh in range(IDX_HEADS):
                    d = lax.dot_general(kib, qi_ref[0, hh], _NT, preferred_element_type=F32)
                    acc = acc + (wi_ref[0, hh:hh + 1, :] * idx_scale) * jnp.maximum(d, 0.0)
                bits = pltpu.bitcast(acc, I32)
                key = bits ^ ((bits >> 31) & 0x7FFFFFFF)
                valid = ((off + k_iota) >> CHUNK_SHIFT) <= q_chunk
                key_scr[pl.ds(off, tk), :] = jnp.where(valid, key, INT_MIN)
                return 0

            lax.fori_loop(0, nblk, idx_body, 0)

            def count_ge(cand):
                hit = jnp.where(key_scr[:kv_len, :] >= cand, 1, 0).astype(I32)
                c8 = jnp.sum(hit.reshape(kv_len // 8, 8, tq), axis=0)
                return jnp.sum(c8, axis=0, keepdims=True)

            thr = jnp.full((1, tq), INT_MIN, I32)
            thr = jnp.where(count_ge(jnp.zeros((1, tq), I32)) >= topk, 0, thr)

            def bit_body(i, t):
                cand = t + jnp.left_shift(jnp.int32(1), 30 - i)
                return jnp.where(count_ge(cand) >= topk, cand, t)

            thr = lax.fori_loop(0, 31, bit_body, thr)
            thr = jnp.maximum(thr, INT_MIN + 1)

            def group_body(g, _):
                qg = q_ref[0, pl.ds(g * rep, rep)].reshape(rep * tq, DSA_HEAD_DIM)
                s = lax.dot_general(k_ref[0, g, :kv_len, :], qg, _NT,
                                    preferred_element_type=F32) * scale
                sel = key_scr[:kv_len, :] >= thr
                s = jnp.where(jnp.concatenate([sel] * rep, axis=1), s, NEG_BIG)
                m = jnp.max(s, axis=0, keepdims=True)
                p = jnp.exp(s - m)
                l = jnp.sum(p, axis=0, keepdims=True)
                acc = jnp.dot(vt_ref[0, g, :, :kv_len], p.astype(BF16),
                              preferred_element_type=F32)
                ot_scr[g] = acc / l
                return 0

            lax.fori_loop(0, groups, group_body, 0)

    for g in range(groups):
        for r in range(rep):
            hh = g * rep + r
            o_ref[:, hh * DSA_HEAD_DIM:(hh + 1) * DSA_HEAD_DIM] = (
                ot_scr[g, :, r * tq:(r + 1) * tq].T.astype(o_ref.dtype))


def dsa_mixer(h, tabs64, tabs128, bsz, seq, w_in, w_o, x_res):
    n = h.shape[0]
    hq, g, hd = DSA_HEADS, DSA_KV_HEADS, DSA_HEAD_DIM
    rep = hq // g
    topk = min(DSA_TOPK_MAX, seq // 4)
    n_q, n_kv, n_qi = hq * hd, g * hd, IDX_HEADS * IDX_DIM
    n_qkv = n_q + 2 * n_kv
    main = n_qkv + n_qi
    d_in = w_in.shape[0]
    quarter = LANES // 4
    w_qkv = w_in[:, :n_qkv].astype(BF16)
    w_qi = w_in[:, n_qkv:main].reshape(d_in, IDX_HEADS, 2, 2, quarter).transpose(0, 1, 3, 2, 4)
    w_qi = w_qi.reshape(d_in, n_qi).astype(BF16)
    perm = _split_rope_perm()
    small_idx = np.concatenate([main + perm, main + IDX_DIM + np.arange(IDX_HEADS)])
    w_small = jnp.pad(w_in[:, small_idx], ((0, 0), (0, 2 * LANES - small_idx.size))).astype(BF16)

    qkv = matmul(h, w_qkv)
    qi = matmul(h, w_qi)
    small = matmul(h, w_small, bn=2 * LANES)

    tq, tk = 128, 256
    q_hm = rope_heads(qkv, tabs128, bsz, seq, col0=0, heads=hq, groups=8)
    k_hm = rope_heads(qkv, tabs128, bsz, seq, col0=n_q, heads=g, groups=8)
    vt = transpose_heads(qkv, bsz, seq, col0=n_q + n_kv, heads=g)
    qi_hm = rope_heads(qi, tabs64, bsz, seq, col0=0, heads=IDX_HEADS, groups=8)
    ki_hm = rope_heads(small, tabs64, bsz, seq, col0=0, heads=1, groups=1)
    wi_t = small[:, LANES:LANES + IDX_HEADS].reshape(bsz, seq, IDX_HEADS).transpose(0, 2, 1)

    nq = seq // tq
    o = pl.pallas_call(
        functools.partial(_dsa_kernel, tq=tq, tk=tk, nq=nq, topk=topk, rep=rep,
                          idx_scale=float(IDX_HEADS) ** -0.5 * float(IDX_DIM) ** -0.5,
                          scale=float(hd) ** -0.5),
        grid=(bsz, nq),
        in_specs=[pl.BlockSpec((1, IDX_HEADS, tq, LANES), lambda b, i: (b, 0, i, 0)),
                  pl.BlockSpec((1, 1, seq, LANES), lambda b, i: (b, 0, 0, 0)),
                  pl.BlockSpec((1, IDX_HEADS, tq), lambda b, i: (b, 0, i)),
                  pl.BlockSpec((1, hq, tq, LANES), lambda b, i: (b, 0, i, 0)),
                  pl.BlockSpec((1, g, seq, LANES), lambda b, i: (b, 0, 0, 0)),
                  pl.BlockSpec((1, g, LANES, seq), lambda b, i: (b, 0, 0, 0))],
        out_specs=pl.BlockSpec((tq, hq * hd), lambda b, i: (b * nq + i, 0)),
        out_shape=jax.ShapeDtypeStruct((n, hq * hd), BF16),
        scratch_shapes=[pltpu.VMEM((seq, tq), I32),
                        pltpu.VMEM((g, hd, rep * tq), F32)],
        compiler_params=_cparams("parallel", "arbitrary"),
    )(qi_hm, ki_hm, wi_t, q_hm, k_hm, vt)
    return matmul(o, w_o.astype(BF16), res=x_res, alpha=1.0)


def _mix_kernel(h_ref, hp_ref, mu_ref, *o_refs):
    h = h_ref[...].astype(F32)
    xx = hp_ref[...].astype(F32) - h
    for i, o_ref in enumerate(o_refs):
        o_ref[...] = (h + xx * mu_ref[i:i + 1, :]).astype(o_ref.dtype)


def token_shift_mix(h, h_prev, mu, rows=128):
    n, d = h.shape
    k = mu.shape[0]
    blk = pl.BlockSpec((rows, d), lambda i: (i, 0))
    return pl.pallas_call(
        _mix_kernel,
        grid=(n // rows,),
        in_specs=[blk, blk, pl.BlockSpec((k, d), lambda i: (0, 0))],
        out_specs=[blk] * k,
        out_shape=[jax.ShapeDtypeStruct((n, d), BF16)] * k,
        compiler_params=_cparams("parallel"),
    )(h, h_prev, mu)


def _split3(x):
    hi = x.astype(BF16)
    r1 = x - hi.astype(F32)
    mid = r1.astype(BF16)
    lo = (r1 - mid.astype(F32)).astype(BF16)
    return hi, mid, lo


def _dot3_right(x, w_bf16):
    hi, mid, lo = _split3(x)
    d = functools.partial(jnp.dot, preferred_element_type=F32)
    return d(hi, w_bf16) + d(mid, w_bf16) + d(lo, w_bf16)


def _dot3_left(w_bf16, x):
    hi, mid, lo = _split3(x)
    d = functools.partial(jnp.dot, preferred_element_type=F32)
    return d(w_bf16, hi) + d(w_bf16, mid) + d(w_bf16, lo)


def _wkv_kernel(r_ref, k_ref, v_ref, wl_ref, al_ref, g_ref,
                w0_ref, a0_ref, kk_ref, ka_ref, rk_ref, lnw_ref, lnb_ref,
                o_ref, state, *, L, pairs):
    c_idx = pl.program_id(2)

    @pl.when(c_idx == 0)
    def _():
        state[...] = jnp.zeros_like(state)

    hn = RWKV_HEAD
    hshift = hn.bit_length() - 1
    li = lax.broadcasted_iota(I32, (LANES, LANES), 0)
    lj = lax.broadcasted_iota(I32, (LANES, LANES), 1)
    same_head = (li >> hshift) == (lj >> hshift)
    seg_avg = jnp.where(same_head, 1.0 / hn, 0.0).astype(BF16)
    ti = lax.broadcasted_iota(I32, (L, L), 0)
    tj = lax.broadcasted_iota(I32, (L, L), 1)
    tril_incl = jnp.where(ti >= tj, 1.0, 0.0).astype(BF16)
    hr = lax.broadcasted_iota(I32, (L, LANES), 0)
    hc = lax.broadcasted_iota(I32, (L, LANES), 1)
    left = hc < hn
    strict = hr > (hc & (hn - 1))
    incl = hr >= (hc & (hn - 1))

    r_all = r_ref[...]
    k_all = k_ref[...]
    v_all = v_ref[...]
    z = w0_ref[...] + wl_ref[...]
    u = -z
    softplus = jnp.maximum(u, 0.0) + jnp.log(1.0 + jnp.exp(-jnp.abs(u)))
    w_log = -softplus - 0.5
    lw = -jnp.exp(w_log)
    a_sig = _sigmoid(a0_ref[...] + al_ref[...])
    k2_all = k_all * (1.0 + (a_sig - 1.0) * ka_ref[...])
    kk_all = k_all * kk_ref[...]
    csum = _dot3_left(tril_incl, lw)
    c_last = csum[L - 1:L, :]
    e_c = jnp.exp(csum)
    e_cprev = jnp.exp(csum - lw)
    e_neg = jnp.exp(-csum)
    e_rem = jnp.exp(c_last - csum)
    gam_all = jnp.exp(c_last)

    bdot = functools.partial(jnp.dot, preferred_element_type=F32)
    cat = jnp.concatenate
    prs = range(pairs)
    sls = [slice(p * LANES, (p + 1) * LANES) for p in prs]
    ksq = [_dot3_right(kk_all[:, sl] * kk_all[:, sl], seg_avg) for sl in sls]
    kk = [kk_all[:, sl] / jnp.maximum(jnp.sqrt(hn * ksq[p]), 1e-12) for p, sl in enumerate(sls)]
    bvec = [kk[p] * a_sig[:, sl] for p, sl in enumerate(sls)]
    at = [-kk[p] * e_cprev[:, sl] for p, sl in enumerate(sls)]
    rt = [r_all[:, sl] * e_c[:, sl] for sl in sls]
    bt = [(bvec[p] * e_neg[:, sl]).astype(BF16) for p, sl in enumerate(sls)]
    kt = [(k2_all[:, sl] * e_neg[:, sl]).astype(BF16) for sl in sls]
    a0 = [jnp.where(left, at[p], 0.0).astype(BF16) for p in prs]
    a1 = [jnp.where(left, 0.0, at[p]).astype(BF16) for p in prs]
    r0 = [jnp.where(left, rt[p], 0.0).astype(BF16) for p in prs]
    r1 = [jnp.where(left, 0.0, rt[p]).astype(BF16) for p in prs]
    vsw = [cat([jnp.where(left, 0.0, v_all[:, sl]), jnp.where(left, v_all[:, sl], 0.0)],
               axis=0).astype(BF16) for sl in sls]

    o0 = [lax.dot_general(cat([a0[p], r0[p]], axis=0), cat([bt[p], kt[p]], axis=0),
                          _NT, preferred_element_type=F32) for p in prs]
    o1 = [lax.dot_general(cat([a1[p], r1[p]], axis=0), cat([kt[p], bt[p]], axis=0),
                          _NT, preferred_element_type=F32) for p in prs]
    st = [state[p] for p in prs]
    arst = [lax.dot_general(cat([a0[p], a1[p], r0[p], r1[p]], axis=0), st[p].astype(BF16),
                            _NT, preferred_element_type=F32) for p in prs]
    u0 = [jnp.where(strict, o0[p][:L], 0.0) for p in prs]
    u1 = [jnp.where(strict, o1[p][:L], 0.0) for p in prs]
    m_ab = [cat([jnp.where(left, u0[p], 0.0), jnp.where(left, 0.0, u1[p])], axis=0) for p in prs]
    m_ak = [cat([jnp.where(left, 0.0, u0[p]), jnp.where(left, u1[p], 0.0)], axis=0) for p in prs]
    x = [arst[p][:2 * L] + bdot(m_ak[p].astype(BF16), vsw[p]) for p in prs]
    mp = m_ab
    steps = (L - 1).bit_length()
    for it in range(steps):
        mpb = [mp[p].astype(BF16) for p in prs]
        x = [x[p] + bdot(mpb[p], x[p].astype(BF16)) for p in prs]
        if it + 1 < steps:
            mp = [bdot(mpb[p], mpb[p]) for p in prs]
    y0 = [jnp.where(incl, o0[p][L:], 0.0) for p in prs]
    y1 = [jnp.where(incl, o1[p][L:], 0.0) for p in prs]
    m_rb = [cat([jnp.where(left, y0[p], 0.0), jnp.where(left, 0.0, y1[p])], axis=0) for p in prs]
    m_rk = [cat([jnp.where(left, 0.0, y0[p]), jnp.where(left, y1[p], 0.0)], axis=0) for p in prs]
    ys = [arst[p][2 * L:] + bdot(cat([m_rb[p], m_rk[p]], axis=1).astype(BF16),
                                 cat([x[p].astype(BF16), vsw[p]], axis=0)) for p in prs]
    y = [ys[p][:L] + ys[p][L:] for p in prs]
    sv_t = [cat([x[p][:L] + x[p][L:], v_all[:, sl]], axis=0).T.astype(BF16)
            for p, sl in enumerate(sls)]
    bkh = [cat([(bvec[p] * e_rem[:, sl]).astype(BF16), (k2_all[:, sl] * e_rem[:, sl]).astype(BF16)],
               axis=0) for p, sl in enumerate(sls)]
    upd = [bdot(sv_t[p], bkh[p]) for p in prs]
    for p, sl in enumerate(sls):
        state[p] = st[p] * gam_all[:, sl] + jnp.where(same_head, upd[p], 0.0)

    mean = [_dot3_right(y[p], seg_avg) for p in prs]
    yc = [y[p] - mean[p] for p in prs]
    var = [_dot3_right(yc[p] * yc[p], seg_avg) for p in prs]
    rk_sum = [_dot3_right(r_all[:, sl] * k2_all[:, sl] * rk_ref[:, sl], seg_avg) for sl in sls]
    out = [(yc[p] * lax.rsqrt(var[p] + GN_EPS) * lnw_ref[:, sl] + lnb_ref[:, sl]
            + hn * rk_sum[p] * v_all[:, sl]) * g_ref[:, sl] for p, sl in enumerate(sls)]
    o_ref[...] = cat(out, axis=1).astype(o_ref.dtype)


def rwkv7_mixer(h, bsz, seq, mu, w_r, w_k, w_v, w_o, w0, w1, w2, a0, a1, a2, g1, g2,
                k_k, k_a, r_k, lnx_w, lnx_b, x_res):
    n, d = h.shape
    h3 = h.reshape(bsz, seq, d)
    h_prev = jnp.pad(h3, ((0, 0), (1, 0), (0, 0)))[:, :-1].reshape(n, d)
    xr, xw, xk, xv, xa, xg = token_shift_mix(h, h_prev, mu)
    r = matmul(xr, w_r.astype(BF16))
    k = matmul(xk, w_k.astype(BF16))
    v = matmul(xv, w_v.astype(BF16))
    tw = matmul(xw, w1.astype(BF16), out_dtype=BF16, act="tanh", bn=w1.shape[1])
    wl = matmul(tw, w2.astype(BF16))
    ta = matmul(xa, a1.astype(BF16), out_dtype=BF16, bn=a1.shape[1])
    al = matmul(ta, a2.astype(BF16))
    gpad = (-GATE_LORA) % LANES
    g1p = jnp.pad(g1, ((0, 0), (0, gpad))).astype(BF16)
    g2p = jnp.pad(g2, ((0, gpad), (0, 0))).astype(BF16)
    tg = matmul(xg, g1p, out_dtype=BF16, act="sigmoid", bn=g1p.shape[1])
    gate = matmul(tg, g2p)

    L, pairs = CHUNK, 8
    nc = seq // L
    wide = LANES * pairs
    row = lambda a: a.reshape(1, d).astype(F32)
    dat = pl.BlockSpec((L, wide), lambda b, j, c: (b * nc + c, j))
    par = pl.BlockSpec((1, wide), lambda b, j, c: (0, j))
    yg = pl.pallas_call(
        functools.partial(_wkv_kernel, L=L, pairs=pairs),
        grid=(bsz, d // wide, nc),
        in_specs=[dat] * 6 + [par] * 7,
        out_specs=dat,
        out_shape=jax.ShapeDtypeStruct((n, d), BF16),
        scratch_shapes=[pltpu.VMEM((pairs, LANES, LANES), F32)],
        compiler_params=_cparams("parallel", "parallel", "arbitrary"),
    )(r, k, v, wl, al, gate, row(w0), row(a0), row(k_k), row(k_a), row(r_k), row(lnx_w), row(lnx_b))
    return matmul(yg, w_o.astype(BF16), res=x_res, alpha=1.0)


def kernel(x, positions, ffn_norm_0, ffn_w_gu_0, ffn_w_down_0, mix_norm_0, mla0_w_in, mla0_q_norm, mla0_w_uq, mla0_kv_norm, mla0_w_ukv, mla0_w_o, ffn_norm_1, ffn_w_gu_1, ffn_w_down_1, mix_norm_1, dsa1_w_in, dsa1_w_o, ffn_norm_2, ffn_w_gu_2, ffn_w_down_2, mix_norm_2, rwkv2_mu, rwkv2_w_r, rwkv2_w_k, rwkv2_w_v, rwkv2_w_o, rwkv2_w0, rwkv2_w1, rwkv2_w2, rwkv2_a0, rwkv2_a1, rwkv2_a2, rwkv2_g1, rwkv2_g2, rwkv2_k_k, rwkv2_k_a, rwkv2_r_k, rwkv2_lnx_w, rwkv2_lnx_b, ffn_norm_3, ffn_w_gu_3, ffn_w_down_3, mix_norm_3, mla3_w_in, mla3_q_norm, mla3_w_uq, mla3_kv_norm, mla3_w_ukv, mla3_w_o, final_norm):
    bsz, seq, d = x.shape
    n = bsz * seq
    xf = x.reshape(n, d)
    pos_col = positions.reshape(n, 1)
    tabs64 = rope_tables(pos_col, MLA_ROPE)
    tabs128 = rope_tables(pos_col, DSA_HEAD_DIM)

    def mla(hn, xr, w_in, q_norm, w_uq, kv_norm, w_ukv, w_o):
        return mla_mixer(hn, tabs64, bsz, seq, w_in, q_norm, w_uq, kv_norm, w_ukv, w_o, xr)

    mixers = [
        lambda hn, xr: mla(hn, xr, mla0_w_in, mla0_q_norm, mla0_w_uq, mla0_kv_norm, mla0_w_ukv, mla0_w_o),
        lambda hn, xr: dsa_mixer(hn, tabs64, tabs128, bsz, seq, dsa1_w_in, dsa1_w_o, xr),
        lambda hn, xr: rwkv7_mixer(hn, bsz, seq, rwkv2_mu, rwkv2_w_r, rwkv2_w_k, rwkv2_w_v, rwkv2_w_o,
                                   rwkv2_w0, rwkv2_w1, rwkv2_w2, rwkv2_a0, rwkv2_a1, rwkv2_a2,
                                   rwkv2_g1, rwkv2_g2, rwkv2_k_k, rwkv2_k_a,
                                   rwkv2_r_k.reshape(-1), rwkv2_lnx_w, rwkv2_lnx_b, xr),
        lambda hn, xr: mla(hn, xr, mla3_w_in, mla3_q_norm, mla3_w_uq, mla3_kv_norm, mla3_w_ukv, mla3_w_o),
    ]
    ffns = [(ffn_norm_0, ffn_w_gu_0, ffn_w_down_0, mix_norm_0),
            (ffn_norm_1, ffn_w_gu_1, ffn_w_down_1, mix_norm_1),
            (ffn_norm_2, ffn_w_gu_2, ffn_w_down_2, mix_norm_2),
            (ffn_norm_3, ffn_w_gu_3, ffn_w_down_3, mix_norm_3)]
    for i in range(4):
        f_norm, w_gu, w_down, m_norm = ffns[i]
        w_gu = w_gu.astype(BF16)
        w_down = w_down.astype(BF16)
        xf = ffn_half_step(xf, f_norm[0], w_gu, w_down, 0)
        mix_in_dtype = F32 if i == 2 else BF16
        xf = mixers[i](rmsnorm(xf, m_norm, out_dtype=mix_in_dtype), xf)
        xf = ffn_half_step(xf, f_norm[1], w_gu, w_down, 1)
    return rmsnorm(xf, final_norm, out_dtype=F32).reshape(bsz, seq, d)
```

```python
import functools

import numpy as np
import jax
import jax.numpy as jnp
from jax import lax
from jax.experimental import pallas as pl
from jax.experimental.pallas import tpu as pltpu

F32 = jnp.float32
BF16 = jnp.bfloat16
I32 = jnp.int32

CHUNK = 64
CHUNK_SHIFT = CHUNK.bit_length() - 1
ROPE_THETA = 10000.0
NORM_EPS = 1e-6
FFN_RES = 0.5
MLA_HEADS = 32
MLA_Q_LORA = 1024
MLA_KV_LORA = 512
MLA_NOPE = 128
MLA_ROPE = 64
MLA_V = 128
DSA_HEADS = 32
DSA_KV_HEADS = 8
DSA_HEAD_DIM = 128
IDX_HEADS = 32
IDX_DIM = 128
IDX_ROPE = 64
DSA_TOPK_MAX = 256
RWKV_HEAD = 64
GATE_LORA = 480
GN_EPS = 64e-5

LANES = 128
VMEM_LIMIT = 56 * 1024 * 1024
INT_MIN = -(2 ** 31)
NEG_BIG = -1e30
LOG2E = 1.4426950408889634

_NT = (((1,), (1,)), ((), ()))


def _cparams(*sem):
    return pltpu.CompilerParams(dimension_semantics=sem, vmem_limit_bytes=VMEM_LIMIT)


def _sigmoid(x):
    return 1.0 / (1.0 + jnp.exp(-x))


def _rmsnorm_kernel(x_ref, g_ref, o_ref):
    x = x_ref[...].astype(F32)
    ms = jnp.mean(x * x, axis=-1, keepdims=True)
    o_ref[...] = (x * lax.rsqrt(ms + NORM_EPS) * g_ref[...]).astype(o_ref.dtype)


def rmsnorm(x, g, *, width=None, col_block=0, out_dtype=BF16, rows=256):
    n = x.shape[0]
    w = x.shape[1] if width is None else width
    return pl.pallas_call(
        _rmsnorm_kernel,
        grid=(n // rows,),
        in_specs=[pl.BlockSpec((rows, w), lambda i: (i, col_block)),
                  pl.BlockSpec((1, w), lambda i: (0, 0))],
        out_specs=pl.BlockSpec((rows, w), lambda i: (i, 0)),
        out_shape=jax.ShapeDtypeStruct((n, w), out_dtype),
        compiler_params=_cparams("parallel"),
    )(x, g.reshape(1, w).astype(F32))


def _row_rstd(ss_ref, d_norm):
    return lax.rsqrt(jnp.sum(ss_ref[...], axis=-1, keepdims=True) * (1.0 / d_norm) + NORM_EPS)


def _lane_partial_sumsq(x):
    sq = x * x
    part = sq[:, :LANES]
    for c in range(1, x.shape[1] // LANES):
        part = part + sq[:, c * LANES:(c + 1) * LANES]
    return part


def _mm_kernel(*refs, act, alpha, has_res, d_norm, emit_norm):
    refs = list(refs)
    a_ref, w_ref = refs[:2]
    del refs[:2]
    ss_ref = refs.pop(0) if d_norm else None
    r_ref = refs.pop(0) if has_res else None
    o_ref = refs.pop(0)
    acc = jnp.dot(a_ref[...], w_ref[...], preferred_element_type=F32)
    if d_norm:
        acc = acc * _row_rstd(ss_ref, d_norm)
    if act == "tanh":
        acc = jnp.tanh(acc)
    elif act == "sigmoid":
        acc = _sigmoid(acc)
    if has_res:
        acc = r_ref[...] + alpha * acc
    o_ref[...] = acc.astype(o_ref.dtype)
    if emit_norm:
        xb_ref, sso_ref = refs
        xb_ref[...] = acc.astype(BF16)
        part = _lane_partial_sumsq(acc)

        @pl.when(pl.program_id(1) == 0)
        def _():
            sso_ref[...] = part

        @pl.when(pl.program_id(1) != 0)
        def _():
            sso_ref[...] += part


def _mm_tiles(m, k, n):
    bm = 1024 if k <= 4096 else 512
    bm = min(bm, m)
    bn = n
    for cand in (1024, 512, 256, 128):
        if n % cand == 0 and (cand * k * 2 * 2 + bm * k * 2 * 2 + bm * cand * 4 * 5) <= VMEM_LIMIT * 0.85:
            bn = cand
            break
    return bm, bn


def matmul(a, w, *, out_dtype=F32, act=None, res=None, alpha=1.0, bm=None, bn=None, w_index=None,
           row_ss=None, emit_norm=False):
    m, k = a.shape
    n = w.shape[-1]
    tbm, tbn = _mm_tiles(m, k, n)
    bm = tbm if bm is None else bm
    bn = tbn if bn is None else bn
    if w.ndim == 3:
        w_spec = pl.BlockSpec((None, k, bn), lambda i, j: (w_index, 0, j))
    else:
        w_spec = pl.BlockSpec((k, bn), lambda i, j: (0, j))
    row_spec = pl.BlockSpec((bm, LANES), lambda i, j: (i, 0))
    tile_spec = pl.BlockSpec((bm, bn), lambda i, j: (i, j))
    in_specs = [pl.BlockSpec((bm, k), lambda i, j: (i, 0)), w_spec]
    args = [a, w]
    if row_ss is not None:
        in_specs.append(row_spec)
        args.append(row_ss)
    if res is not None:
        in_specs.append(tile_spec)
        args.append(res)
    out_specs, out_shape = tile_spec, jax.ShapeDtypeStruct((m, n), out_dtype)
    if emit_norm:
        out_specs = [tile_spec, tile_spec, row_spec]
        out_shape = [out_shape, jax.ShapeDtypeStruct((m, n), BF16),
                     jax.ShapeDtypeStruct((m, LANES), F32)]
    return pl.pallas_call(
        functools.partial(_mm_kernel, act=act, alpha=alpha, has_res=res is not None,
                          d_norm=k if row_ss is not None else 0, emit_norm=emit_norm),
        grid=(m // bm, n // bn),
        in_specs=in_specs,
        out_specs=out_specs,
        out_shape=out_shape,
        compiler_params=_cparams("parallel", "arbitrary"),
    )(*args)


def _swiglu_kernel(a_ref, ss_ref, wg_ref, wu_ref, o_ref, *, d_norm):
    a = a_ref[...]
    rstd = _row_rstd(ss_ref, d_norm)
    g = jnp.dot(a, wg_ref[...], preferred_element_type=F32) * rstd
    u = jnp.dot(a, wu_ref[...], preferred_element_type=F32) * rstd
    o_ref[...] = (g * _sigmoid(g) * u).astype(o_ref.dtype)


def swiglu_up(a, row_ss, w_gu, half, *, bm=1024, bf=512):
    m, k = a.shape
    bm = min(bm, m)
    f = w_gu.shape[-1] // 2
    nf = f // bf
    return pl.pallas_call(
        functools.partial(_swiglu_kernel, d_norm=k),
        grid=(m // bm, nf),
        in_specs=[pl.BlockSpec((bm, k), lambda i, j: (i, 0)),
                  pl.BlockSpec((bm, LANES), lambda i, j: (i, 0)),
                  pl.BlockSpec((None, k, bf), lambda i, j: (half, 0, j)),
                  pl.BlockSpec((None, k, bf), lambda i, j: (half, 0, j + nf))],
        out_specs=pl.BlockSpec((bm, bf), lambda i, j: (i, j)),
        out_shape=jax.ShapeDtypeStruct((m, f), BF16),
        compiler_params=_cparams("parallel", "arbitrary"),
    )(a, row_ss, w_gu, w_gu)


def ffn_half_step(xs, w_gu, w_down, half):
    x, xb, ss = xs
    act = swiglu_up(xb, ss, w_gu, half)
    return matmul(act, w_down, res=x, alpha=FFN_RES, bm=min(1024, x.shape[0]), bn=512,
                  w_index=half, emit_norm=True)


def _norm_prep_kernel(x_ref, xb_ref, ss_ref):
    x = x_ref[...]
    xb_ref[...] = x.astype(BF16)
    ss_ref[...] = _lane_partial_sumsq(x)


def norm_prep(x, rows=256):
    n, d = x.shape
    xb, ss = pl.pallas_call(
        _norm_prep_kernel,
        grid=(n // rows,),
        in_specs=[pl.BlockSpec((rows, d), lambda i: (i, 0))],
        out_specs=[pl.BlockSpec((rows, d), lambda i: (i, 0)),
                   pl.BlockSpec((rows, LANES), lambda i: (i, 0))],
        out_shape=[jax.ShapeDtypeStruct((n, d), BF16), jax.ShapeDtypeStruct((n, LANES), F32)],
        compiler_params=_cparams("parallel"),
    )(x)
    return x, xb, ss


def _rope_lane_consts(rot_dim):
    lane = np.arange(LANES)
    half = rot_dim // 2
    if rot_dim == LANES:
        freq_idx = lane % half
        active = np.ones(LANES, bool)
    else:
        freq_idx = lane % half
        active = (lane % 64) < half
    inv = np.where(active, ROPE_THETA ** (-(2.0 * freq_idx) / rot_dim), 0.0)
    sign = np.where(active, np.where(lane < 64, -1.0, 1.0), 0.0)
    return (jnp.asarray(inv, F32).reshape(1, LANES), jnp.asarray(sign, F32).reshape(1, LANES))


def _rope_tab_kernel(pos_ref, inv_ref, sign_ref, c_ref, s_ref):
    ang = pos_ref[...].astype(F32) * inv_ref[...]
    c_ref[...] = jnp.cos(ang)
    s_ref[...] = jnp.sin(ang) * sign_ref[...]


def rope_tables(pos_col, rot_dim, rows=512):
    n = pos_col.shape[0]
    inv, sign = _rope_lane_consts(rot_dim)
    tab = jax.ShapeDtypeStruct((n, LANES), F32)
    return pl.pallas_call(
        _rope_tab_kernel,
        grid=(n // rows,),
        in_specs=[pl.BlockSpec((rows, 1), lambda i: (i, 0)),
                  pl.BlockSpec((1, LANES), lambda i: (0, 0)),
                  pl.BlockSpec((1, LANES), lambda i: (0, 0))],
        out_specs=[pl.BlockSpec((rows, LANES), lambda i: (i, 0))] * 2,
        out_shape=[tab, tab],
        compiler_params=_cparams("parallel"),
    )(pos_col, inv, sign)


def _rope(x, c, s):
    return x * c + pltpu.roll(x, 64, 1) * s


def _split_rope_perm():
    q = LANES // 4
    return np.concatenate([np.arange(0, q), np.arange(2 * q, 3 * q),
                           np.arange(q, 2 * q), np.arange(3 * q, 4 * q)])


def _mla_attn_kernel(q_ref, kn_ref, v_ref, kr_ref, cq_ref, sq_ref, ck_ref, sk_ref,
                     o_ref, kfull, vfull, krs, *, tq, nq, hb, scale):
    hg = pl.program_id(1)
    qi = pl.program_id(2)

    @pl.when(jnp.logical_and(hg == 0, qi == 0))
    def _():
        krs[...] = _rope(kr_ref[...], ck_ref[...], sk_ref[...]).astype(BF16)

    @pl.when(qi == 0)
    def _():
        for hd in range(hb):
            kfull[hd, :, :LANES] = kn_ref[:, hd * LANES:(hd + 1) * LANES]
            kfull[hd, :, LANES:] = krs[...]
            vfull[hd, :, :LANES] = v_ref[:, hd * LANES:(hd + 1) * LANES]
            vfull[hd, :, LANES:] = jnp.ones((vfull.shape[1], LANES), BF16)

    cq = cq_ref[...]
    sq = sq_ref[...]
    diag_ok = ((lax.broadcasted_iota(I32, (tq, tq), 1) >> CHUNK_SHIFT)
               <= (lax.broadcasted_iota(I32, (tq, tq), 0) >> CHUNK_SHIFT))

    for i in range(nq):
        @pl.when(qi == i)
        def _(i=i):
            kv_len = (i + 1) * tq
            for hd in range(hb):
                base = hd * 2 * LANES
                qr = _rope(q_ref[:, base + LANES:base + 2 * LANES], cq, sq)
                q = jnp.concatenate([q_ref[:, base:base + LANES], qr], axis=1)
                q = (q * (scale * LOG2E)).astype(BF16)
                s = lax.dot_general(q, kfull[hd, :kv_len, :], _NT,
                                    preferred_element_type=F32)
                s_diag = jnp.where(diag_ok, s[:, kv_len - tq:], NEG_BIG)
                s = s_diag if i == 0 else jnp.concatenate([s[:, :kv_len - tq], s_diag], axis=1)
                p = jnp.exp2(s - jnp.max(s, axis=-1, keepdims=True))
                ol = jnp.dot(p.astype(BF16), vfull[hd, :kv_len, :],
                             preferred_element_type=F32)
                o_ref[:, hd * LANES:(hd + 1) * LANES] = (
                    ol[:, :LANES] / ol[:, LANES:LANES + 1]).astype(o_ref.dtype)


def mla_mixer(xs, gain, tabs64, bsz, seq, w_in, q_norm, w_uq, kv_norm, w_ukv, w_o):
    x_res, xb, ss = xs
    n = xb.shape[0]
    hq = MLA_HEADS
    w_in = gain[:, None] * w_in
    lane = np.arange(LANES)
    rope_on = (lane % 64) < (MLA_ROPE // 2)
    rope_src = np.where(rope_on, (lane // 64) * (MLA_ROPE // 2) + lane % (MLA_ROPE // 2), 0)
    base = MLA_Q_LORA + MLA_KV_LORA
    kr_cols = w_in[:, base + rope_src] * jnp.asarray(rope_on, F32)
    w_in_p = jnp.concatenate([w_in[:, :base], kr_cols], axis=1).astype(BF16)
    hd = MLA_NOPE + MLA_ROPE
    idx = np.concatenate([np.concatenate([hh * hd + np.arange(MLA_NOPE),
                                          hh * hd + MLA_NOPE + rope_src]) for hh in range(hq)])
    msk = np.tile(np.concatenate([np.ones(MLA_NOPE, bool), rope_on]), hq)
    w_uq_p = (w_uq[:, idx] * jnp.asarray(msk, F32)).astype(BF16)
    w_ukv_p = w_ukv.reshape(MLA_KV_LORA, hq, 2, MLA_NOPE).transpose(0, 2, 1, 3)
    w_ukv_p = w_ukv_p.reshape(MLA_KV_LORA, 2 * hq * MLA_NOPE).astype(BF16)

    lat = matmul(xb, w_in_p, bm=512, bn=w_in_p.shape[1], row_ss=ss)
    cq = rmsnorm(lat, q_norm, width=MLA_Q_LORA, col_block=0)
    ckv = rmsnorm(lat, kv_norm, width=MLA_KV_LORA, col_block=MLA_Q_LORA // MLA_KV_LORA)
    q = matmul(cq, w_uq_p)
    kv = matmul(ckv, w_ukv_p, out_dtype=BF16)

    tq, hb = 256, 4
    nq = seq // tq
    ng = hq // hb
    c64, s64 = tabs64
    kr_blk = base // LANES
    o = pl.pallas_call(
        functools.partial(_mla_attn_kernel, tq=tq, nq=nq, hb=hb, scale=float(hd) ** -0.5),
        grid=(bsz, ng, nq),
        in_specs=[pl.BlockSpec((tq, 2 * LANES * hb), lambda b, hh, i: (b * nq + i, hh)),
                  pl.BlockSpec((seq, LANES * hb), lambda b, hh, i: (b, hh)),
                  pl.BlockSpec((seq, LANES * hb), lambda b, hh, i: (b, ng + hh)),
                  pl.BlockSpec((seq, LANES), lambda b, hh, i: (b, kr_blk)),
                  pl.BlockSpec((tq, LANES), lambda b, hh, i: (b * nq + i, 0)),
                  pl.BlockSpec((tq, LANES), lambda b, hh, i: (b * nq + i, 0)),
                  pl.BlockSpec((seq, LANES), lambda b, hh, i: (b, 0)),
                  pl.BlockSpec((seq, LANES), lambda b, hh, i: (b, 0))],
        out_specs=pl.BlockSpec((tq, LANES * hb), lambda b, hh, i: (b * nq + i, hh)),
        out_shape=jax.ShapeDtypeStruct((n, hq * MLA_V), BF16),
        scratch_shapes=[pltpu.VMEM((hb, seq, 2 * LANES), BF16),
                        pltpu.VMEM((hb, seq, 2 * LANES), BF16),
                        pltpu.VMEM((seq, LANES), BF16)],
        compiler_params=_cparams("arbitrary", "arbitrary", "arbitrary"),
    )(q, kv, kv, lat, c64, s64, c64, s64)
    return matmul(o, w_o.astype(BF16), res=x_res, alpha=1.0, emit_norm=True)


def _rope_heads_kernel(x_ref, c_ref, s_ref, o_ref, *, groups, out_scale):
    c = c_ref[...]
    s = s_ref[...]
    for g in range(groups):
        x = x_ref[:, g * LANES:(g + 1) * LANES]
        o_ref[0, g] = (_rope(x, c, s) * out_scale).astype(o_ref.dtype)


def rope_heads(x, tabs, bsz, seq, *, col0, heads, groups, rows=512, out_scale=1.0):
    nt = seq // rows
    cb = col0 // (LANES * groups)
    c, s = tabs
    return pl.pallas_call(
        functools.partial(_rope_heads_kernel, groups=groups, out_scale=out_scale),
        grid=(bsz, nt, heads // groups),
        in_specs=[pl.BlockSpec((rows, LANES * groups), lambda b, t, j: (b * nt + t, cb + j)),
                  pl.BlockSpec((rows, LANES), lambda b, t, j: (b * nt + t, 0)),
                  pl.BlockSpec((rows, LANES), lambda b, t, j: (b * nt + t, 0))],
        out_specs=pl.BlockSpec((1, groups, rows, LANES), lambda b, t, j: (b, j, t, 0)),
        out_shape=jax.ShapeDtypeStruct((bsz, heads, seq, LANES), BF16),
        compiler_params=_cparams("parallel", "parallel", "arbitrary"),
    )(x, c, s)


ONES_ROWS = 16


def _transpose_heads_kernel(x_ref, o_ref, *, groups):
    rows = x_ref.shape[0]
    for g in range(groups):
        o_ref[0, g, :LANES, :] = x_ref[:, g * LANES:(g + 1) * LANES].T.astype(o_ref.dtype)
        o_ref[0, g, LANES:, :] = jnp.ones((ONES_ROWS, rows), o_ref.dtype)


def transpose_heads(x, bsz, seq, *, col0, heads, rows=256):
    nt = seq // rows
    cb = col0 // (LANES * heads)
    return pl.pallas_call(
        functools.partial(_transpose_heads_kernel, groups=heads),
        grid=(bsz, nt),
        in_specs=[pl.BlockSpec((rows, LANES * heads), lambda b, t: (b * nt + t, cb))],
        out_specs=pl.BlockSpec((1, heads, LANES + ONES_ROWS, rows), lambda b, t: (b, 0, 0, t)),
        out_shape=jax.ShapeDtypeStruct((bsz, heads, LANES + ONES_ROWS, seq), BF16),
        compiler_params=_cparams("parallel", "arbitrary"),
    )(x)


def _dsa_kernel(qi_ref, ki_ref, wi_ref, q_ref, k_ref, vt_ref, o_ref, key_scr, ot_scr,
                *, tq, tk, nq, topk, rep, idx_scale):
    qt = pl.program_id(1)
    q_chunk = (qt * tq + lax.broadcasted_iota(I32, (tk, tq), 1)) >> CHUNK_SHIFT
    k_iota = lax.broadcasted_iota(I32, (tk, tq), 0)
    groups = DSA_KV_HEADS

    for nb in range(nq * tq // tk):
        @pl.when((qt * tq) // tk == nb)
        def _(nb=nb):
            nblk = nb + 1
            kv_len = nblk * tk

            def idx_body(j, _):
                off = pl.multiple_of(j * tk, tk)
                kib = ki_ref[0, 0, pl.ds(off, tk), :]
                acc = jnp.zeros((tk, tq), F32)
                for hh in range(IDX_HEADS):
                    d = lax.dot_general(kib, qi_ref[0, hh], _NT, preferred_element_type=F32)
                    acc = acc + (wi_ref[0, hh:hh + 1, :] * idx_scale) * jnp.maximum(d, 0.0)
                bits = pltpu.bitcast(acc, I32)
                key = bits ^ ((bits >> 31) & 0x7FFFFFFF)
                valid = ((off + k_iota) >> CHUNK_SHIFT) <= q_chunk
                key_scr[pl.ds(off, tk), :] = jnp.where(valid, key, INT_MIN)
                return 0

            lax.fori_loop(0, nblk, idx_body, 0)

            def count_ge(cand):
                hit = jnp.where(key_scr[:kv_len, :] >= cand, 1, 0).astype(I32)
                c8 = jnp.sum(hit.reshape(kv_len // 8, 8, tq), axis=0)
                return jnp.sum(c8, axis=0, keepdims=True)

            thr = jnp.full((1, tq), INT_MIN, I32)
            thr = jnp.where(count_ge(jnp.zeros((1, tq), I32)) >= topk, 0, thr)

            def bit_body(i, t):
                cand = t + jnp.left_shift(jnp.int32(1), 30 - i)
                return jnp.where(count_ge(cand) >= topk, cand, t)

            thr = lax.fori_loop(0, 31, bit_body, thr)
            thr = jnp.maximum(thr, INT_MIN + 1)

            def group_body(g, _):
                qg = q_ref[0, pl.ds(g * rep, rep)].reshape(rep * tq, DSA_HEAD_DIM)
                s = lax.dot_general(k_ref[0, g, :kv_len, :], qg, _NT,
                                    preferred_element_type=F32)
                sel = key_scr[:kv_len, :] >= thr
                s = jnp.where(jnp.concatenate([sel] * rep, axis=1), s, NEG_BIG)
                p = jnp.exp2(s - jnp.max(s, axis=0, keepdims=True))
                acc = jnp.dot(vt_ref[0, g, :, :kv_len], p.astype(BF16),
                              preferred_element_type=F32)
                ot_scr[g] = acc[:DSA_HEAD_DIM] / acc[DSA_HEAD_DIM:DSA_HEAD_DIM + 1]
                return 0

            lax.fori_loop(0, groups, group_body, 0)

    for g in range(groups):
        for r in range(rep):
            hh = g * rep + r
            o_ref[:, hh * DSA_HEAD_DIM:(hh + 1) * DSA_HEAD_DIM] = (
                ot_scr[g, :, r * tq:(r + 1) * tq].T.astype(o_ref.dtype))


def dsa_mixer(xs, gain, tabs64, tabs128, bsz, seq, w_in, w_o):
    x_res, xb, ss = xs
    n = xb.shape[0]
    w_in = gain[:, None] * w_in
    hq, g, hd = DSA_HEADS, DSA_KV_HEADS, DSA_HEAD_DIM
    rep = hq // g
    topk = min(DSA_TOPK_MAX, seq // 4)
    n_q, n_kv, n_qi = hq * hd, g * hd, IDX_HEADS * IDX_DIM
    n_qkv = n_q + 2 * n_kv
    main = n_qkv + n_qi
    d_in = w_in.shape[0]
    quarter = LANES // 4
    w_qkv = w_in[:, :n_qkv].astype(BF16)
    w_qi = w_in[:, n_qkv:main].reshape(d_in, IDX_HEADS, 2, 2, quarter).transpose(0, 1, 3, 2, 4)
    w_qi = w_qi.reshape(d_in, n_qi).astype(BF16)
    perm = _split_rope_perm()
    small_idx = np.concatenate([main + perm, main + IDX_DIM + np.arange(IDX_HEADS)])
    w_small = jnp.pad(w_in[:, small_idx], ((0, 0), (0, 2 * LANES - small_idx.size))).astype(BF16)

    qkv = matmul(xb, w_qkv, row_ss=ss)
    qi = matmul(xb, w_qi, row_ss=ss)
    small = matmul(xb, w_small, bn=2 * LANES, row_ss=ss)

    tq, tk = 128, 256
    q_hm = rope_heads(qkv, tabs128, bsz, seq, col0=0, heads=hq, groups=8,
                      out_scale=float(hd) ** -0.5 * LOG2E)
    k_hm = rope_heads(qkv, tabs128, bsz, seq, col0=n_q, heads=g, groups=8)
    vt = transpose_heads(qkv, bsz, seq, col0=n_q + n_kv, heads=g)
    qi_hm = rope_heads(qi, tabs64, bsz, seq, col0=0, heads=IDX_HEADS, groups=8)
    ki_hm = rope_heads(small, tabs64, bsz, seq, col0=0, heads=1, groups=1)
    wi_t = small[:, LANES:LANES + IDX_HEADS].reshape(bsz, seq, IDX_HEADS).transpose(0, 2, 1)

    nq = seq // tq
    o = pl.pallas_call(
        functools.partial(_dsa_kernel, tq=tq, tk=tk, nq=nq, topk=topk, rep=rep,
                          idx_scale=float(IDX_HEADS) ** -0.5 * float(IDX_DIM) ** -0.5),
        grid=(bsz, nq),
        in_specs=[pl.BlockSpec((1, IDX_HEADS, tq, LANES), lambda b, i: (b, 0, i, 0)),
                  pl.BlockSpec((1, 1, seq, LANES), lambda b, i: (b, 0, 0, 0)),
                  pl.BlockSpec((1, IDX_HEADS, tq), lambda b, i: (b, 0, i)),
                  pl.BlockSpec((1, hq, tq, LANES), lambda b, i: (b, 0, i, 0)),
                  pl.BlockSpec((1, g, seq, LANES), lambda b, i: (b, 0, 0, 0)),
                  pl.BlockSpec((1, g, LANES + ONES_ROWS, seq), lambda b, i: (b, 0, 0, 0))],
        out_specs=pl.BlockSpec((tq, hq * hd), lambda b, i: (b * nq + i, 0)),
        out_shape=jax.ShapeDtypeStruct((n, hq * hd), BF16),
        scratch_shapes=[pltpu.VMEM((seq, tq), I32),
                        pltpu.VMEM((g, hd, rep * tq), F32)],
        compiler_params=_cparams("parallel", "arbitrary"),
    )(qi_hm, ki_hm, wi_t, q_hm, k_hm, vt)
    return matmul(o, w_o.astype(BF16), res=x_res, alpha=1.0, emit_norm=True)


def _mix_kernel(h_ref, hp_ref, mu_ref, *o_refs):
    h = h_ref[...].astype(F32)
    xx = hp_ref[...].astype(F32) - h
    for i, o_ref in enumerate(o_refs):
        o_ref[...] = (h + xx * mu_ref[i:i + 1, :]).astype(o_ref.dtype)


def token_shift_mix(h, h_prev, mu, rows=128):
    n, d = h.shape
    k = mu.shape[0]
    blk = pl.BlockSpec((rows, d), lambda i: (i, 0))
    return pl.pallas_call(
        _mix_kernel,
        grid=(n // rows,),
        in_specs=[blk, blk, pl.BlockSpec((k, d), lambda i: (0, 0))],
        out_specs=[blk] * k,
        out_shape=[jax.ShapeDtypeStruct((n, d), BF16)] * k,
        compiler_params=_cparams("parallel"),
    )(h, h_prev, mu)


def _split3(x):
    hi = x.astype(BF16)
    r1 = x - hi.astype(F32)
    mid = r1.astype(BF16)
    lo = (r1 - mid.astype(F32)).astype(BF16)
    return hi, mid, lo


def _dot3_right(x, w_bf16):
    hi, mid, lo = _split3(x)
    d = functools.partial(jnp.dot, preferred_element_type=F32)
    return d(hi, w_bf16) + d(mid, w_bf16) + d(lo, w_bf16)


def _dot3_left(w_bf16, x):
    hi, mid, lo = _split3(x)
    d = functools.partial(jnp.dot, preferred_element_type=F32)
    return d(w_bf16, hi) + d(w_bf16, mid) + d(w_bf16, lo)


def _wkv_kernel(r_ref, k_ref, v_ref, wl_ref, al_ref, g_ref,
                w0_ref, a0_ref, kk_ref, ka_ref, rk_ref, lnw_ref, lnb_ref,
                o_ref, state, *, L, pairs):
    c_idx = pl.program_id(2)

    @pl.when(c_idx == 0)
    def _():
        state[...] = jnp.zeros_like(state)

    hn = RWKV_HEAD
    hshift = hn.bit_length() - 1
    li = lax.broadcasted_iota(I32, (LANES, LANES), 0)
    lj = lax.broadcasted_iota(I32, (LANES, LANES), 1)
    same_head = (li >> hshift) == (lj >> hshift)
    seg_avg = jnp.where(same_head, 1.0 / hn, 0.0).astype(BF16)
    ti = lax.broadcasted_iota(I32, (L, L), 0)
    tj = lax.broadcasted_iota(I32, (L, L), 1)
    tril_incl = jnp.where(ti >= tj, 1.0, 0.0).astype(BF16)
    hr = lax.broadcasted_iota(I32, (L, LANES), 0)
    hc = lax.broadcasted_iota(I32, (L, LANES), 1)
    left = hc < hn
    strict = hr > (hc & (hn - 1))
    incl = hr >= (hc & (hn - 1))

    r_all = r_ref[...]
    k_all = k_ref[...]
    v_all = v_ref[...]
    z = w0_ref[...] + wl_ref[...]
    u = -z
    softplus = jnp.maximum(u, 0.0) + jnp.log(1.0 + jnp.exp(-jnp.abs(u)))
    w_log = -softplus - 0.5
    lw = -jnp.exp(w_log)
    a_sig = _sigmoid(a0_ref[...] + al_ref[...])
    k2_all = k_all * (1.0 + (a_sig - 1.0) * ka_ref[...])
    kk_all = k_all * kk_ref[...]
    csum = _dot3_left(tril_incl, lw)
    c_last = csum[L - 1:L, :]
    e_c = jnp.exp(csum)
    e_cprev = jnp.exp(csum - lw)
    e_neg = jnp.exp(-csum)
    e_rem = jnp.exp(c_last - csum)
    gam_all = jnp.exp(c_last)

    bdot = functools.partial(jnp.dot, preferred_element_type=F32)
    cat = jnp.concatenate
    prs = range(pairs)
    sls = [slice(p * LANES, (p + 1) * LANES) for p in prs]
    ksq = [_dot3_right(kk_all[:, sl] * kk_all[:, sl], seg_avg) for sl in sls]
    kk = [kk_all[:, sl] / jnp.maximum(jnp.sqrt(hn * ksq[p]), 1e-12) for p, sl in enumerate(sls)]
    bvec = [kk[p] * a_sig[:, sl] for p, sl in enumerate(sls)]
    at = [-kk[p] * e_cprev[:, sl] for p, sl in enumerate(sls)]
    rt = [r_all[:, sl] * e_c[:, sl] for sl in sls]
    bt = [(bvec[p] * e_neg[:, sl]).astype(BF16) for p, sl in enumerate(sls)]
    kt = [(k2_all[:, sl] * e_neg[:, sl]).astype(BF16) for sl in sls]
    a0 = [jnp.where(left, at[p], 0.0).astype(BF16) for p in prs]
    a1 = [jnp.where(left, 0.0, at[p]).astype(BF16) for p in prs]
    r0 = [jnp.where(left, rt[p], 0.0).astype(BF16) for p in prs]
    r1 = [jnp.where(left, 0.0, rt[p]).astype(BF16) for p in prs]
    vsw = [cat([jnp.where(left, 0.0, v_all[:, sl]), jnp.where(left, v_all[:, sl], 0.0)],
               axis=0).astype(BF16) for sl in sls]

    o0 = [lax.dot_general(cat([a0[p], r0[p]], axis=0), cat([bt[p], kt[p]], axis=0),
                          _NT, preferred_element_type=F32) for p in prs]
    o1 = [lax.dot_general(cat([a1[p], r1[p]], axis=0), cat([kt[p], bt[p]], axis=0),
                          _NT, preferred_element_type=F32) for p in prs]
    st = [state[p] for p in prs]
    arst = [lax.dot_general(cat([a0[p], a1[p], r0[p], r1[p]], axis=0), st[p].astype(BF16),
                            _NT, preferred_element_type=F32) for p in prs]
    u0 = [jnp.where(strict, o0[p][:L], 0.0) for p in prs]
    u1 = [jnp.where(strict, o1[p][:L], 0.0) for p in prs]
    m_ab = [cat([jnp.where(left, u0[p], 0.0), jnp.where(left, 0.0, u1[p])], axis=0) for p in prs]
    m_ak = [cat([jnp.where(left, 0.0, u0[p]), jnp.where(left, u1[p], 0.0)], axis=0) for p in prs]
    x = [arst[p][:2 * L] + bdot(m_ak[p].astype(BF16), vsw[p]) for p in prs]
    mp = m_ab
    steps = (L - 1).bit_length()
    for it in range(steps):
        mpb = [mp[p].astype(BF16) for p in prs]
        x = [x[p] + bdot(mpb[p], x[p].astype(BF16)) for p in prs]
        if it + 1 < steps:
            mp = [bdot(mpb[p], mpb[p]) for p in prs]
    y0 = [jnp.where(incl, o0[p][L:], 0.0) for p in prs]
    y1 = [jnp.where(incl, o1[p][L:], 0.0) for p in prs]
    m_rb = [cat([jnp.where(left, y0[p], 0.0), jnp.where(left, 0.0, y1[p])], axis=0) for p in prs]
    m_rk = [cat([jnp.where(left, 0.0, y0[p]), jnp.where(left, y1[p], 0.0)], axis=0) for p in prs]
    ys = [arst[p][2 * L:] + bdot(cat([m_rb[p], m_rk[p]], axis=1).astype(BF16),
                                 cat([x[p].astype(BF16), vsw[p]], axis=0)) for p in prs]
    y = [ys[p][:L] + ys[p][L:] for p in prs]
    sv_t = [cat([x[p][:L] + x[p][L:], v_all[:, sl]], axis=0).T.astype(BF16)
            for p, sl in enumerate(sls)]
    bkh = [cat([(bvec[p] * e_rem[:, sl]).astype(BF16), (k2_all[:, sl] * e_rem[:, sl]).astype(BF16)],
               axis=0) for p, sl in enumerate(sls)]
    upd = [bdot(sv_t[p], bkh[p]) for p in prs]
    for p, sl in enumerate(sls):
        state[p] = st[p] * gam_all[:, sl] + jnp.where(same_head, upd[p], 0.0)

    mean = [_dot3_right(y[p], seg_avg) for p in prs]
    yc = [y[p] - mean[p] for p in prs]
    var = [_dot3_right(yc[p] * yc[p], seg_avg) for p in prs]
    rk_sum = [_dot3_right(r_all[:, sl] * k2_all[:, sl] * rk_ref[:, sl], seg_avg) for sl in sls]
    out = [(yc[p] * lax.rsqrt(var[p] + GN_EPS) * lnw_ref[:, sl] + lnb_ref[:, sl]
            + hn * rk_sum[p] * v_all[:, sl]) * g_ref[:, sl] for p, sl in enumerate(sls)]
    o_ref[...] = cat(out, axis=1).astype(o_ref.dtype)


def rwkv7_mixer(h, bsz, seq, mu, w_r, w_k, w_v, w_o, w0, w1, w2, a0, a1, a2, g1, g2,
                k_k, k_a, r_k, lnx_w, lnx_b, x_res):
    n, d = h.shape
    h3 = h.reshape(bsz, seq, d)
    h_prev = jnp.pad(h3, ((0, 0), (1, 0), (0, 0)))[:, :-1].reshape(n, d)
    xr, xw, xk, xv, xa, xg = token_shift_mix(h, h_prev, mu)
    r = matmul(xr, w_r.astype(BF16))
    k = matmul(xk, w_k.astype(BF16))
    v = matmul(xv, w_v.astype(BF16))
    tw = matmul(xw, w1.astype(BF16), out_dtype=BF16, act="tanh", bn=w1.shape[1])
    wl = matmul(tw, w2.astype(BF16))
    ta = matmul(xa, a1.astype(BF16), out_dtype=BF16, bn=a1.shape[1])
    al = matmul(ta, a2.astype(BF16))
    gpad = (-GATE_LORA) % LANES
    g1p = jnp.pad(g1, ((0, 0), (0, gpad))).astype(BF16)
    g2p = jnp.pad(g2, ((0, gpad), (0, 0))).astype(BF16)
    tg = matmul(xg, g1p, out_dtype=BF16, act="sigmoid", bn=g1p.shape[1])
    gate = matmul(tg, g2p)

    L, pairs = CHUNK, 8
    nc = seq // L
    wide = LANES * pairs
    row = lambda a: a.reshape(1, d).astype(F32)
    dat = pl.BlockSpec((L, wide), lambda b, j, c: (b * nc + c, j))
    par = pl.BlockSpec((1, wide), lambda b, j, c: (0, j))
    yg = pl.pallas_call(
        functools.partial(_wkv_kernel, L=L, pairs=pairs),
        grid=(bsz, d // wide, nc),
        in_specs=[dat] * 6 + [par] * 7,
        out_specs=dat,
        out_shape=jax.ShapeDtypeStruct((n, d), BF16),
        scratch_shapes=[pltpu.VMEM((pairs, LANES, LANES), F32)],
        compiler_params=_cparams("parallel", "parallel", "arbitrary"),
    )(r, k, v, wl, al, gate, row(w0), row(a0), row(k_k), row(k_a), row(r_k), row(lnx_w), row(lnx_b))
    return matmul(yg, w_o.astype(BF16), res=x_res, alpha=1.0, emit_norm=True)


def kernel(x, positions, ffn_norm_0, ffn_w_gu_0, ffn_w_down_0, mix_norm_0, mla0_w_in, mla0_q_norm, mla0_w_uq, mla0_kv_norm, mla0_w_ukv, mla0_w_o, ffn_norm_1, ffn_w_gu_1, ffn_w_down_1, mix_norm_1, dsa1_w_in, dsa1_w_o, ffn_norm_2, ffn_w_gu_2, ffn_w_down_2, mix_norm_2, rwkv2_mu, rwkv2_w_r, rwkv2_w_k, rwkv2_w_v, rwkv2_w_o, rwkv2_w0, rwkv2_w1, rwkv2_w2, rwkv2_a0, rwkv2_a1, rwkv2_a2, rwkv2_g1, rwkv2_g2, rwkv2_k_k, rwkv2_k_a, rwkv2_r_k, rwkv2_lnx_w, rwkv2_lnx_b, ffn_norm_3, ffn_w_gu_3, ffn_w_down_3, mix_norm_3, mla3_w_in, mla3_q_norm, mla3_w_uq, mla3_kv_norm, mla3_w_ukv, mla3_w_o, final_norm):
    bsz, seq, d = x.shape
    n = bsz * seq
    pos_col = positions.reshape(n, 1)
    tabs64 = rope_tables(pos_col, MLA_ROPE)
    tabs128 = rope_tables(pos_col, DSA_HEAD_DIM)

    def mla(xs, gain, w_in, q_norm, w_uq, kv_norm, w_ukv, w_o):
        return mla_mixer(xs, gain, tabs64, bsz, seq, w_in, q_norm, w_uq, kv_norm, w_ukv, w_o)

    def rwkv(xs, gain):
        hn = rmsnorm(xs[0], gain, out_dtype=F32)
        return rwkv7_mixer(hn, bsz, seq, rwkv2_mu, rwkv2_w_r, rwkv2_w_k, rwkv2_w_v, rwkv2_w_o,
                           rwkv2_w0, rwkv2_w1, rwkv2_w2, rwkv2_a0, rwkv2_a1, rwkv2_a2,
                           rwkv2_g1, rwkv2_g2, rwkv2_k_k, rwkv2_k_a,
                           rwkv2_r_k.reshape(-1), rwkv2_lnx_w, rwkv2_lnx_b, xs[0])

    mixers = [
        lambda xs, gn: mla(xs, gn, mla0_w_in, mla0_q_norm, mla0_w_uq, mla0_kv_norm, mla0_w_ukv, mla0_w_o),
        lambda xs, gn: dsa_mixer(xs, gn, tabs64, tabs128, bsz, seq, dsa1_w_in, dsa1_w_o),
        rwkv,
        lambda xs, gn: mla(xs, gn, mla3_w_in, mla3_q_norm, mla3_w_uq, mla3_kv_norm, mla3_w_ukv, mla3_w_o),
    ]
    ffns = [(ffn_norm_0, ffn_w_gu_0, ffn_w_down_0, mix_norm_0),
            (ffn_norm_1, ffn_w_gu_1, ffn_w_down_1, mix_norm_1),
            (ffn_norm_2, ffn_w_gu_2, ffn_w_down_2, mix_norm_2),
            (ffn_norm_3, ffn_w_gu_3, ffn_w_down_3, mix_norm_3)]
    xs = norm_prep(x.reshape(n, d))
    for i in range(4):
        f_norm, w_gu, w_down, m_norm = ffns[i]
        w_gu = (f_norm[:, :, None] * w_gu).astype(BF16)
        w_down = w_down.astype(BF16)
        xs = ffn_half_step(xs, w_gu, w_down, 0)
        xs = mixers[i](xs, m_norm)
        xs = ffn_half_step(xs, w_gu, w_down, 1)
    return rmsnorm(xs[0], final_norm, out_dtype=F32).reshape(bsz, seq, d)
```

```python
import functools

import numpy as np
import jax
import jax.numpy as jnp
from jax import lax
from jax.experimental import pallas as pl
from jax.experimental.pallas import tpu as pltpu

F32 = jnp.float32
BF16 = jnp.bfloat16
I32 = jnp.int32

CHUNK = 64
CHUNK_SHIFT = CHUNK.bit_length() - 1
ROPE_THETA = 10000.0
NORM_EPS = 1e-6
FFN_RES = 0.5
MLA_HEADS = 32
MLA_Q_LORA = 1024
MLA_KV_LORA = 512
MLA_NOPE = 128
MLA_ROPE = 64
MLA_V = 128
DSA_HEADS = 32
DSA_KV_HEADS = 8
DSA_HEAD_DIM = 128
IDX_HEADS = 32
IDX_DIM = 128
IDX_ROPE = 64
DSA_TOPK_MAX = 256
RWKV_HEAD = 64
GATE_LORA = 480
GN_EPS = 64e-5

LANES = 128
VMEM_LIMIT = 56 * 1024 * 1024
INT_MIN = -(2 ** 31)
NEG_BIG = -1e30
LOG2E = 1.4426950408889634

_NT = (((1,), (1,)), ((), ()))


def _cparams(*sem):
    return pltpu.CompilerParams(dimension_semantics=sem, vmem_limit_bytes=VMEM_LIMIT)


def _sigmoid(x):
    return 1.0 / (1.0 + jnp.exp(-x))


def _rmsnorm_kernel(x_ref, g_ref, o_ref):
    x = x_ref[...].astype(F32)
    ms = jnp.mean(x * x, axis=-1, keepdims=True)
    o_ref[...] = (x * lax.rsqrt(ms + NORM_EPS) * g_ref[...]).astype(o_ref.dtype)


def rmsnorm(x, g, *, width=None, col_block=0, out_dtype=BF16, rows=256):
    n = x.shape[0]
    w = x.shape[1] if width is None else width
    return pl.pallas_call(
        _rmsnorm_kernel,
        grid=(n // rows,),
        in_specs=[pl.BlockSpec((rows, w), lambda i: (i, col_block)),
                  pl.BlockSpec((1, w), lambda i: (0, 0))],
        out_specs=pl.BlockSpec((rows, w), lambda i: (i, 0)),
        out_shape=jax.ShapeDtypeStruct((n, w), out_dtype),
        compiler_params=_cparams("parallel"),
    )(x, g.reshape(1, w).astype(F32))


def _row_rstd(ss_ref, d_norm):
    return lax.rsqrt(jnp.sum(ss_ref[...], axis=-1, keepdims=True) * (1.0 / d_norm) + NORM_EPS)


def _lane_partial_sumsq(x):
    sq = x * x
    part = sq[:, :LANES]
    for c in range(1, x.shape[1] // LANES):
        part = part + sq[:, c * LANES:(c + 1) * LANES]
    return part


def _mm_kernel(*refs, act, alpha, has_res, d_norm, emit_norm):
    refs = list(refs)
    a_ref, w_ref = refs[:2]
    del refs[:2]
    ss_ref = refs.pop(0) if d_norm else None
    r_ref = refs.pop(0) if has_res else None
    o_ref = refs.pop(0)
    acc = jnp.dot(a_ref[...], w_ref[...], preferred_element_type=F32)
    if d_norm:
        acc = acc * _row_rstd(ss_ref, d_norm)
    if act == "tanh":
        acc = jnp.tanh(acc)
    elif act == "sigmoid":
        acc = _sigmoid(acc)
    if has_res:
        acc = r_ref[...] + alpha * acc
    o_ref[...] = acc.astype(o_ref.dtype)
    if emit_norm:
        xb_ref, sso_ref = refs
        xb_ref[...] = acc.astype(BF16)
        part = _lane_partial_sumsq(acc)

        @pl.when(pl.program_id(1) == 0)
        def _():
            sso_ref[...] = part

        @pl.when(pl.program_id(1) != 0)
        def _():
            sso_ref[...] += part


def _mm_tiles(m, k, n):
    bm = 1024 if k <= 4096 else 512
    bm = min(bm, m)
    bn = n
    for cand in (1024, 512, 256, 128):
        if n % cand == 0 and (cand * k * 2 * 2 + bm * k * 2 * 2 + bm * cand * 4 * 5) <= VMEM_LIMIT * 0.85:
            bn = cand
            break
    return bm, bn


def matmul(a, w, *, out_dtype=F32, act=None, res=None, alpha=1.0, bm=None, bn=None, w_index=None,
           row_ss=None, emit_norm=False):
    m, k = a.shape
    n = w.shape[-1]
    tbm, tbn = _mm_tiles(m, k, n)
    bm = tbm if bm is None else bm
    bn = tbn if bn is None else bn
    if w.ndim == 3:
        w_spec = pl.BlockSpec((None, k, bn), lambda i, j: (w_index, 0, j))
    else:
        w_spec = pl.BlockSpec((k, bn), lambda i, j: (0, j))
    row_spec = pl.BlockSpec((bm, LANES), lambda i, j: (i, 0))
    tile_spec = pl.BlockSpec((bm, bn), lambda i, j: (i, j))
    in_specs = [pl.BlockSpec((bm, k), lambda i, j: (i, 0)), w_spec]
    args = [a, w]
    if row_ss is not None:
        in_specs.append(row_spec)
        args.append(row_ss)
    if res is not None:
        in_specs.append(tile_spec)
        args.append(res)
    out_specs, out_shape = tile_spec, jax.ShapeDtypeStruct((m, n), out_dtype)
    if emit_norm:
        out_specs = [tile_spec, tile_spec, row_spec]
        out_shape = [out_shape, jax.ShapeDtypeStruct((m, n), BF16),
                     jax.ShapeDtypeStruct((m, LANES), F32)]
    return pl.pallas_call(
        functools.partial(_mm_kernel, act=act, alpha=alpha, has_res=res is not None,
                          d_norm=k if row_ss is not None else 0, emit_norm=emit_norm),
        grid=(m // bm, n // bn),
        in_specs=in_specs,
        out_specs=out_specs,
        out_shape=out_shape,
        compiler_params=_cparams("parallel", "arbitrary"),
    )(*args)


def _swiglu_kernel(a_ref, ss_ref, wg_ref, wu_ref, o_ref, *, d_norm):
    a = a_ref[...]
    rstd = _row_rstd(ss_ref, d_norm)
    g = jnp.dot(a, wg_ref[...], preferred_element_type=F32) * rstd
    u = jnp.dot(a, wu_ref[...], preferred_element_type=F32) * rstd
    o_ref[...] = (g * _sigmoid(g) * u).astype(o_ref.dtype)


def swiglu_up(a, row_ss, w_gu, half, *, bm=1024, bf=512):
    m, k = a.shape
    bm = min(bm, m)
    f = w_gu.shape[-1] // 2
    nf = f // bf
    return pl.pallas_call(
        functools.partial(_swiglu_kernel, d_norm=k),
        grid=(m // bm, nf),
        in_specs=[pl.BlockSpec((bm, k), lambda i, j: (i, 0)),
                  pl.BlockSpec((bm, LANES), lambda i, j: (i, 0)),
                  pl.BlockSpec((None, k, bf), lambda i, j: (half, 0, j)),
                  pl.BlockSpec((None, k, bf), lambda i, j: (half, 0, j + nf))],
        out_specs=pl.BlockSpec((bm, bf), lambda i, j: (i, j)),
        out_shape=jax.ShapeDtypeStruct((m, f), BF16),
        compiler_params=_cparams("parallel", "arbitrary"),
    )(a, row_ss, w_gu, w_gu)


def ffn_half_step(xs, w_gu, w_down, half):
    x, xb, ss = xs
    act = swiglu_up(xb, ss, w_gu, half)
    return matmul(act, w_down, res=x, alpha=FFN_RES, bm=min(1024, x.shape[0]), bn=512,
                  w_index=half, emit_norm=True)


def _norm_prep_kernel(x_ref, xb_ref, ss_ref):
    x = x_ref[...]
    xb_ref[...] = x.astype(BF16)
    ss_ref[...] = _lane_partial_sumsq(x)


def norm_prep(x, rows=256):
    n, d = x.shape
    xb, ss = pl.pallas_call(
        _norm_prep_kernel,
        grid=(n // rows,),
        in_specs=[pl.BlockSpec((rows, d), lambda i: (i, 0))],
        out_specs=[pl.BlockSpec((rows, d), lambda i: (i, 0)),
                   pl.BlockSpec((rows, LANES), lambda i: (i, 0))],
        out_shape=[jax.ShapeDtypeStruct((n, d), BF16), jax.ShapeDtypeStruct((n, LANES), F32)],
        compiler_params=_cparams("parallel"),
    )(x)
    return x, xb, ss


def _rope_lane_consts(layout):
    lane = np.arange(LANES)
    if layout == "half128":
        rot_dim, freq_idx, active = LANES, lane % 64, np.ones(LANES, bool)
        rolls = {64: np.where(lane < 64, -1.0, 1.0)}
    elif layout == "spread64":
        rot_dim, freq_idx, active = 64, lane % 32, (lane % 64) < 32
        rolls = {64: np.where(active, np.where(lane < 64, -1.0, 1.0), 0.0)}
    else:
        rot_dim, freq_idx, active = 64, lane % 32, lane < 64
        rolls = {32: np.where((lane >= 32) & (lane < 64), 1.0, 0.0),
                 96: np.where(lane < 32, -1.0, 0.0)}
    inv = np.where(active, ROPE_THETA ** (-(2.0 * freq_idx) / rot_dim), 0.0)
    signs = np.stack([rolls[k] for k in sorted(rolls)])
    return jnp.asarray(inv, F32).reshape(1, LANES), jnp.asarray(signs, F32), tuple(sorted(rolls))


def _rope_tab_kernel(pos_ref, inv_ref, sign_ref, c_ref, *s_refs):
    ang = pos_ref[...].astype(F32) * inv_ref[...]
    c_ref[...] = jnp.cos(ang)
    sin = jnp.sin(ang)
    for k, s_ref in enumerate(s_refs):
        s_ref[...] = sin * sign_ref[k:k + 1, :]


def rope_tables(pos_col, layout, rows=512):
    n = pos_col.shape[0]
    inv, signs, rolls = _rope_lane_consts(layout)
    tab = jax.ShapeDtypeStruct((n, LANES), F32)
    tabs = pl.pallas_call(
        _rope_tab_kernel,
        grid=(n // rows,),
        in_specs=[pl.BlockSpec((rows, 1), lambda i: (i, 0)),
                  pl.BlockSpec((1, LANES), lambda i: (0, 0)),
                  pl.BlockSpec((len(rolls), LANES), lambda i: (0, 0))],
        out_specs=[pl.BlockSpec((rows, LANES), lambda i: (i, 0))] * (1 + len(rolls)),
        out_shape=[tab] * (1 + len(rolls)),
        compiler_params=_cparams("parallel"),
    )(pos_col, inv, signs)
    return rolls, tabs


def _rope(x, c, s, rolls=(64,)):
    s = s if isinstance(s, (list, tuple)) else [s]
    y = x * c
    for shift, sk in zip(rolls, s):
        y = y + pltpu.roll(x, shift, 1) * sk
    return y


def _mla_attn_kernel(q_ref, kn_ref, v_ref, kr_ref, cq_ref, sq_ref, ck_ref, sk_ref,
                     o_ref, kfull, vfull, krs, *, tq, nq, hb, scale):
    hg = pl.program_id(1)
    qi = pl.program_id(2)

    @pl.when(jnp.logical_and(hg == 0, qi == 0))
    def _():
        krs[...] = _rope(kr_ref[...], ck_ref[...], sk_ref[...]).astype(BF16)

    @pl.when(qi == 0)
    def _():
        for hd in range(hb):
            kfull[hd, :, :LANES] = kn_ref[:, hd * LANES:(hd + 1) * LANES]
            kfull[hd, :, LANES:] = krs[...]
            vfull[hd, :, :LANES] = v_ref[:, hd * LANES:(hd + 1) * LANES]
            vfull[hd, :, LANES:] = jnp.ones((vfull.shape[1], LANES), BF16)

    cq = cq_ref[...]
    sq = sq_ref[...]
    diag_ok = ((lax.broadcasted_iota(I32, (tq, tq), 1) >> CHUNK_SHIFT)
               <= (lax.broadcasted_iota(I32, (tq, tq), 0) >> CHUNK_SHIFT))

    for i in range(nq):
        @pl.when(qi == i)
        def _(i=i):
            kv_len = (i + 1) * tq
            for hd in range(hb):
                base = hd * 2 * LANES
                qr = _rope(q_ref[:, base + LANES:base + 2 * LANES], cq, sq)
                q = jnp.concatenate([q_ref[:, base:base + LANES], qr], axis=1)
                q = (q * (scale * LOG2E)).astype(BF16)
                s = lax.dot_general(q, kfull[hd, :kv_len, :], _NT,
                                    preferred_element_type=F32)
                s_diag = jnp.where(diag_ok, s[:, kv_len - tq:], NEG_BIG)
                s = s_diag if i == 0 else jnp.concatenate([s[:, :kv_len - tq], s_diag], axis=1)
                p = jnp.exp2(s - jnp.max(s, axis=-1, keepdims=True))
                ol = jnp.dot(p.astype(BF16), vfull[hd, :kv_len, :],
                             preferred_element_type=F32)
                o_ref[:, hd * LANES:(hd + 1) * LANES] = (
                    ol[:, :LANES] / ol[:, LANES:LANES + 1]).astype(o_ref.dtype)


def mla_mixer(xs, gain, tabs64, bsz, seq, w_in, q_norm, w_uq, kv_norm, w_ukv, w_o):
    x_res, xb, ss = xs
    n = xb.shape[0]
    hq = MLA_HEADS
    w_in = gain[:, None] * w_in
    lane = np.arange(LANES)
    rope_on = (lane % 64) < (MLA_ROPE // 2)
    rope_src = np.where(rope_on, (lane // 64) * (MLA_ROPE // 2) + lane % (MLA_ROPE // 2), 0)
    base = MLA_Q_LORA + MLA_KV_LORA
    kr_cols = w_in[:, base + rope_src] * jnp.asarray(rope_on, F32)
    w_in_p = jnp.concatenate([w_in[:, :base], kr_cols], axis=1).astype(BF16)
    hd = MLA_NOPE + MLA_ROPE
    idx = np.concatenate([np.concatenate([hh * hd + np.arange(MLA_NOPE),
                                          hh * hd + MLA_NOPE + rope_src]) for hh in range(hq)])
    msk = np.tile(np.concatenate([np.ones(MLA_NOPE, bool), rope_on]), hq)
    w_uq_p = (w_uq[:, idx] * jnp.asarray(msk, F32)).astype(BF16)
    w_ukv_p = w_ukv.reshape(MLA_KV_LORA, hq, 2, MLA_NOPE).transpose(0, 2, 1, 3)
    w_ukv_p = w_ukv_p.reshape(MLA_KV_LORA, 2 * hq * MLA_NOPE).astype(BF16)

    lat = matmul(xb, w_in_p, bm=512, bn=w_in_p.shape[1], row_ss=ss)
    cq = rmsnorm(lat, q_norm, width=MLA_Q_LORA, col_block=0)
    ckv = rmsnorm(lat, kv_norm, width=MLA_KV_LORA, col_block=MLA_Q_LORA // MLA_KV_LORA)
    q = matmul(cq, w_uq_p)
    kv = matmul(ckv, w_ukv_p, out_dtype=BF16)

    tq, hb = 256, 4
    nq = seq // tq
    ng = hq // hb
    c64, s64 = tabs64[1]
    kr_blk = base // LANES
    o = pl.pallas_call(
        functools.partial(_mla_attn_kernel, tq=tq, nq=nq, hb=hb, scale=float(hd) ** -0.5),
        grid=(bsz, ng, nq),
        in_specs=[pl.BlockSpec((tq, 2 * LANES * hb), lambda b, hh, i: (b * nq + i, hh)),
                  pl.BlockSpec((seq, LANES * hb), lambda b, hh, i: (b, hh)),
                  pl.BlockSpec((seq, LANES * hb), lambda b, hh, i: (b, ng + hh)),
                  pl.BlockSpec((seq, LANES), lambda b, hh, i: (b, kr_blk)),
                  pl.BlockSpec((tq, LANES), lambda b, hh, i: (b * nq + i, 0)),
                  pl.BlockSpec((tq, LANES), lambda b, hh, i: (b * nq + i, 0)),
                  pl.BlockSpec((seq, LANES), lambda b, hh, i: (b, 0)),
                  pl.BlockSpec((seq, LANES), lambda b, hh, i: (b, 0))],
        out_specs=pl.BlockSpec((tq, LANES * hb), lambda b, hh, i: (b * nq + i, hh)),
        out_shape=jax.ShapeDtypeStruct((n, hq * MLA_V), BF16),
        scratch_shapes=[pltpu.VMEM((hb, seq, 2 * LANES), BF16),
                        pltpu.VMEM((hb, seq, 2 * LANES), BF16),
                        pltpu.VMEM((seq, LANES), BF16)],
        compiler_params=_cparams("arbitrary", "arbitrary", "arbitrary"),
    )(q, kv, kv, lat, c64, s64, c64, s64)
    return matmul(o, w_o.astype(BF16), res=x_res, alpha=1.0, emit_norm=True)


def _rope_heads_kernel(x_ref, c_ref, *rest, groups, out_scale, rolls):
    s_refs, o_ref = rest[:-1], rest[-1]
    c = c_ref[...]
    s = [s_ref[...] for s_ref in s_refs]
    for g in range(groups):
        x = x_ref[:, g * LANES:(g + 1) * LANES]
        o_ref[0, g] = (_rope(x, c, s, rolls) * out_scale).astype(o_ref.dtype)


def rope_heads(x, tabs, bsz, seq, *, col0, heads, groups, rows=512, out_scale=1.0):
    nt = seq // rows
    cb = col0 // (LANES * groups)
    rolls, tables = tabs
    tab_spec = pl.BlockSpec((rows, LANES), lambda b, t, j: (b * nt + t, 0))
    return pl.pallas_call(
        functools.partial(_rope_heads_kernel, groups=groups, out_scale=out_scale, rolls=rolls),
        grid=(bsz, nt, heads // groups),
        in_specs=[pl.BlockSpec((rows, LANES * groups), lambda b, t, j: (b * nt + t, cb + j))]
        + [tab_spec] * len(tables),
        out_specs=pl.BlockSpec((1, groups, rows, LANES), lambda b, t, j: (b, j, t, 0)),
        out_shape=jax.ShapeDtypeStruct((bsz, heads, seq, LANES), BF16),
        compiler_params=_cparams("parallel", "parallel", "arbitrary"),
    )(x, *tables)


ONES_ROWS = 16


def _transpose_heads_kernel(x_ref, o_ref, *, groups):
    rows = x_ref.shape[0]
    for g in range(groups):
        o_ref[0, g, :LANES, :] = x_ref[:, g * LANES:(g + 1) * LANES].T.astype(o_ref.dtype)
        o_ref[0, g, LANES:, :] = jnp.ones((ONES_ROWS, rows), o_ref.dtype)


def transpose_heads(x, bsz, seq, *, col0, heads, rows=256):
    nt = seq // rows
    cb = col0 // (LANES * heads)
    return pl.pallas_call(
        functools.partial(_transpose_heads_kernel, groups=heads),
        grid=(bsz, nt),
        in_specs=[pl.BlockSpec((rows, LANES * heads), lambda b, t: (b * nt + t, cb))],
        out_specs=pl.BlockSpec((1, heads, LANES + ONES_ROWS, rows), lambda b, t: (b, 0, 0, t)),
        out_shape=jax.ShapeDtypeStruct((bsz, heads, LANES + ONES_ROWS, seq), BF16),
        compiler_params=_cparams("parallel", "arbitrary"),
    )(x)


def _dsa_kernel(qi_ref, ki_ref, wi_ref, q_ref, k_ref, vt_ref, o_ref, key_scr, ot_scr,
                *, tq, tk, nq, topk, rep, idx_scale):
    qt = pl.program_id(1)
    q_chunk = (qt * tq + lax.broadcasted_iota(I32, (tk, tq), 1)) >> CHUNK_SHIFT
    k_iota = lax.broadcasted_iota(I32, (tk, tq), 0)
    groups = DSA_KV_HEADS

    for nb in range(nq * tq // tk):
        @pl.when((qt * tq) // tk == nb)
        def _(nb=nb):
            nblk = nb + 1
            kv_len = nblk * tk

            def idx_body(j, _):
                off = pl.multiple_of(j * tk, tk)
                kib = ki_ref[0, 0, pl.ds(off, tk), :]
                acc = jnp.zeros((tk, tq), F32)
                for hh in range(IDX_HEADS):
                    d = lax.dot_general(kib, qi_ref[0, hh], _NT, preferred_element_type=F32)
                    acc = acc + (wi_ref[0, hh:hh + 1, :] * idx_scale) * jnp.maximum(d, 0.0)
                bits = pltpu.bitcast(acc, I32)
                key = bits ^ ((bits >> 31) & 0x7FFFFFFF)
                valid = ((off + k_iota) >> CHUNK_SHIFT) <= q_chunk
                key_scr[pl.ds(off, tk), :] = jnp.where(valid, key, INT_MIN)
                return 0

            lax.fori_loop(0, nblk, idx_body, 0)

            def count_ge(cand):
                hit = jnp.where(key_scr[:kv_len, :] >= cand, 1, 0).astype(I32)
                c8 = jnp.sum(hit.reshape(kv_len // 8, 8, tq), axis=0)
                return jnp.sum(c8, axis=0, keepdims=True)

            thr = jnp.full((1, tq), INT_MIN, I32)
            thr = jnp.where(count_ge(jnp.zeros((1, tq), I32)) >= topk, 0, thr)

            def bit_body(i, t):
                cand = t + jnp.left_shift(jnp.int32(1), 30 - i)
                return jnp.where(count_ge(cand) >= topk, cand, t)

            thr = lax.fori_loop(0, 31, bit_body, thr)
            thr = jnp.maximum(thr, INT_MIN + 1)

            def group_body(g, _):
                qg = q_ref[0, pl.ds(g * rep, rep)].reshape(rep * tq, DSA_HEAD_DIM)
                s = lax.dot_general(k_ref[0, g, :kv_len, :], qg, _NT,
                                    preferred_element_type=F32)
                sel = key_scr[:kv_len, :] >= thr
                s = jnp.where(jnp.concatenate([sel] * rep, axis=1), s, NEG_BIG)
                p = jnp.exp2(s - jnp.max(s, axis=0, keepdims=True))
                acc = jnp.dot(vt_ref[0, g, :, :kv_len], p.astype(BF16),
                              preferred_element_type=F32)
                ot_scr[g] = acc[:DSA_HEAD_DIM] / acc[DSA_HEAD_DIM:DSA_HEAD_DIM + 1]
                return 0

            lax.fori_loop(0, groups, group_body, 0)

    for g in range(groups):
        for r in range(rep):
            hh = g * rep + r
            o_ref[:, hh * DSA_HEAD_DIM:(hh + 1) * DSA_HEAD_DIM] = (
                ot_scr[g, :, r * tq:(r + 1) * tq].T.astype(o_ref.dtype))


def dsa_mixer(xs, gain, tabs64, tabs128, bsz, seq, w_in, w_o):
    x_res, xb, ss = xs
    n = xb.shape[0]
    hq, g, hd = DSA_HEADS, DSA_KV_HEADS, DSA_HEAD_DIM
    rep = hq // g
    topk = min(DSA_TOPK_MAX, seq // 4)
    n_q, n_kv, n_qi = hq * hd, g * hd, IDX_HEADS * IDX_DIM
    n_qkv = n_q + 2 * n_kv
    main = n_qkv + n_qi
    n_small = IDX_DIM + IDX_HEADS
    gcol = gain[:, None]
    w_qkv = (gcol * w_in[:, :n_qkv]).astype(BF16)
    w_qi = (gcol * w_in[:, n_qkv:main]).astype(BF16)
    w_small = jnp.pad(gcol * w_in[:, main:main + n_small],
                      ((0, 0), (0, 2 * LANES - n_small))).astype(BF16)

    qkv = matmul(xb, w_qkv, row_ss=ss)
    qi = matmul(xb, w_qi, row_ss=ss)
    small = matmul(xb, w_small, bn=2 * LANES, row_ss=ss)

    tq, tk = 128, 256
    q_hm = rope_heads(qkv, tabs128, bsz, seq, col0=0, heads=hq, groups=8,
                      out_scale=float(hd) ** -0.5 * LOG2E)
    k_hm = rope_heads(qkv, tabs128, bsz, seq, col0=n_q, heads=g, groups=8)
    vt = transpose_heads(qkv, bsz, seq, col0=n_q + n_kv, heads=g)
    qi_hm = rope_heads(qi, tabs64, bsz, seq, col0=0, heads=IDX_HEADS, groups=8)
    ki_hm = rope_heads(small, tabs64, bsz, seq, col0=0, heads=1, groups=1)
    wi_t = small[:, LANES:LANES + IDX_HEADS].reshape(bsz, seq, IDX_HEADS).transpose(0, 2, 1)

    nq = seq // tq
    o = pl.pallas_call(
        functools.partial(_dsa_kernel, tq=tq, tk=tk, nq=nq, topk=topk, rep=rep,
                          idx_scale=float(IDX_HEADS) ** -0.5 * float(IDX_DIM) ** -0.5),
        grid=(bsz, nq),
        in_specs=[pl.BlockSpec((1, IDX_HEADS, tq, LANES), lambda b, i: (b, 0, i, 0)),
                  pl.BlockSpec((1, 1, seq, LANES), lambda b, i: (b, 0, 0, 0)),
                  pl.BlockSpec((1, IDX_HEADS, tq), lambda b, i: (b, 0, i)),
                  pl.BlockSpec((1, hq, tq, LANES), lambda b, i: (b, 0, i, 0)),
                  pl.BlockSpec((1, g, seq, LANES), lambda b, i: (b, 0, 0, 0)),
                  pl.BlockSpec((1, g, LANES + ONES_ROWS, seq), lambda b, i: (b, 0, 0, 0))],
        out_specs=pl.BlockSpec((tq, hq * hd), lambda b, i: (b * nq + i, 0)),
        out_shape=jax.ShapeDtypeStruct((n, hq * hd), BF16),
        scratch_shapes=[pltpu.VMEM((seq, tq), I32),
                        pltpu.VMEM((g, hd, rep * tq), F32)],
        compiler_params=_cparams("parallel", "arbitrary"),
    )(qi_hm, ki_hm, wi_t, q_hm, k_hm, vt)
    return matmul(o, w_o.astype(BF16), res=x_res, alpha=1.0, emit_norm=True)


SUBLANES = 8


def _rms(x, g):
    ms = jnp.mean(x * x, axis=-1, keepdims=True)
    return x * lax.rsqrt(ms + NORM_EPS) * g


def _shift_mix_kernel(x_ref, xp_ref, g_ref, mu_ref, *o_refs):
    g = g_ref[...]
    h = _rms(x_ref[...], g)
    first_tile = pl.program_id(1) == 0
    last_prev = _rms(xp_ref[...], g)[SUBLANES - 1:SUBLANES, :]
    last_prev = jnp.where(first_tile, 0.0, last_prev)
    row = lax.broadcasted_iota(I32, h.shape, 0)
    h_prev = jnp.where(row == 0, last_prev, pltpu.roll(h, 1, 0))
    xx = h_prev - h
    for i, o_ref in enumerate(o_refs):
        o_ref[...] = (h + xx * mu_ref[i:i + 1, :]).astype(o_ref.dtype)


def token_shift_mix(x, gain, mu, bsz, seq, rows=256):
    n, d = x.shape
    k = mu.shape[0]
    nt = seq // rows
    per = rows // SUBLANES
    blk = pl.BlockSpec((rows, d), lambda b, t: (b * nt + t, 0))
    prev = pl.BlockSpec((SUBLANES, d), lambda b, t: (jnp.maximum((b * nt + t) * per - 1, 0), 0))
    return pl.pallas_call(
        _shift_mix_kernel,
        grid=(bsz, nt),
        in_specs=[blk, prev, pl.BlockSpec((1, d), lambda b, t: (0, 0)),
                  pl.BlockSpec((k, d), lambda b, t: (0, 0))],
        out_specs=[blk] * k,
        out_shape=[jax.ShapeDtypeStruct((n, d), BF16)] * k,
        compiler_params=_cparams("parallel", "arbitrary"),
    )(x, x, gain.reshape(1, d).astype(F32), mu)


def _split3(x):
    hi = x.astype(BF16)
    r1 = x - hi.astype(F32)
    mid = r1.astype(BF16)
    lo = (r1 - mid.astype(F32)).astype(BF16)
    return hi, mid, lo


def _dot3_right(x, w_bf16):
    hi, mid, lo = _split3(x)
    d = functools.partial(jnp.dot, preferred_element_type=F32)
    return d(hi, w_bf16) + d(mid, w_bf16) + d(lo, w_bf16)


def _dot3_left(w_bf16, x):
    hi, mid, lo = _split3(x)
    d = functools.partial(jnp.dot, preferred_element_type=F32)
    return d(w_bf16, hi) + d(w_bf16, mid) + d(w_bf16, lo)


def _wkv_kernel(r_ref, k_ref, v_ref, wl_ref, al_ref, g_ref,
                w0_ref, a0_ref, kk_ref, ka_ref, rk_ref, lnw_ref, lnb_ref,
                o_ref, state, *, L, pairs):
    c_idx = pl.program_id(2)

    @pl.when(c_idx == 0)
    def _():
        state[...] = jnp.zeros_like(state)

    hn = RWKV_HEAD
    hshift = hn.bit_length() - 1
    li = lax.broadcasted_iota(I32, (LANES, LANES), 0)
    lj = lax.broadcasted_iota(I32, (LANES, LANES), 1)
    same_head = (li >> hshift) == (lj >> hshift)
    seg_avg = jnp.where(same_head, 1.0 / hn, 0.0).astype(BF16)

    def seg_mean(x):
        lhs = jnp.concatenate([part[:, p * LANES:(p + 1) * LANES]
                               for part in _split3(x) for p in range(pairs)], axis=0)
        res = jnp.dot(lhs, seg_avg, preferred_element_type=F32)
        rows = pairs * L
        tot = res[:rows] + res[rows:2 * rows] + res[2 * rows:]
        return jnp.concatenate([tot[p * L:(p + 1) * L] for p in range(pairs)], axis=1)
    ti = lax.broadcasted_iota(I32, (L, L), 0)
    tj = lax.broadcasted_iota(I32, (L, L), 1)
    tril_incl = jnp.where(ti >= tj, 1.0, 0.0).astype(BF16)
    hr = lax.broadcasted_iota(I32, (L, LANES), 0)
    hc = lax.broadcasted_iota(I32, (L, LANES), 1)
    left = hc < hn
    strict = hr > (hc & (hn - 1))
    incl = hr >= (hc & (hn - 1))

    r_all = r_ref[...]
    k_all = k_ref[...]
    v_all = v_ref[...]
    z = w0_ref[...] + wl_ref[...]
    u = -z
    softplus = jnp.maximum(u, 0.0) + jnp.log(1.0 + jnp.exp(-jnp.abs(u)))
    w_log = -softplus - 0.5
    lw = -jnp.exp(w_log)
    a_sig = _sigmoid(a0_ref[...] + al_ref[...])
    k2_all = k_all * (1.0 + (a_sig - 1.0) * ka_ref[...])
    kk_all = k_all * kk_ref[...]
    csum = _dot3_left(tril_incl, lw)
    c_last = csum[L - 1:L, :]
    e_c = jnp.exp(csum)
    e_cprev = jnp.exp(csum - lw)
    e_neg = jnp.exp(-csum)
    e_rem = jnp.exp(c_last - csum)
    gam_all = jnp.exp(c_last)

    bdot = functools.partial(jnp.dot, preferred_element_type=F32)
    cat = jnp.concatenate
    prs = range(pairs)
    sls = [slice(p * LANES, (p + 1) * LANES) for p in prs]
    ksq = seg_mean(kk_all * kk_all)
    kk_n = kk_all / jnp.maximum(jnp.sqrt(hn * ksq), 1e-12)
    bvec_all = kk_n * a_sig
    at_all = -kk_n * e_cprev
    rt_all = r_all * e_c
    bt = [(bvec_all[:, sl] * e_neg[:, sl]).astype(BF16) for sl in sls]
    kt = [(k2_all[:, sl] * e_neg[:, sl]).astype(BF16) for sl in sls]
    a0 = [jnp.where(left, at_all[:, sl], 0.0).astype(BF16) for sl in sls]
    a1 = [jnp.where(left, 0.0, at_all[:, sl]).astype(BF16) for sl in sls]
    r0 = [jnp.where(left, rt_all[:, sl], 0.0).astype(BF16) for sl in sls]
    r1 = [jnp.where(left, 0.0, rt_all[:, sl]).astype(BF16) for sl in sls]
    vsw = [cat([jnp.where(left, 0.0, v_all[:, sl]), jnp.where(left, v_all[:, sl], 0.0)],
               axis=0).astype(BF16) for sl in sls]

    lhs = [cat([a0[p], r0[p], a1[p], r1[p]], axis=0) for p in prs]
    o01 = [lax.dot_general(lhs[p], cat([bt[p], kt[p]], axis=0), _NT,
                           preferred_element_type=F32) for p in prs]
    o0 = [o01[p][:2 * L] for p in prs]
    o1 = [pltpu.roll(o01[p][2 * L:], hn, 1) for p in prs]
    st = [state[p] for p in prs]
    ars = [lax.dot_general(lhs[p], st[p].astype(BF16), _NT,
                           preferred_element_type=F32) for p in prs]
    u0 = [jnp.where(strict, o0[p][:L], 0.0) for p in prs]
    u1 = [jnp.where(strict, o1[p][:L], 0.0) for p in prs]
    m_ab = [cat([jnp.where(left, u0[p], 0.0), jnp.where(left, 0.0, u1[p])], axis=0) for p in prs]
    m_ak = [cat([jnp.where(left, 0.0, u0[p]), jnp.where(left, u1[p], 0.0)], axis=0) for p in prs]
    xt = [(cat([ars[p][:L], ars[p][2 * L:3 * L]], axis=0)
           + bdot(m_ak[p].astype(BF16), vsw[p])).T for p in prs]
    mt = [m_ab[p].T for p in prs]
    steps = (L - 1).bit_length()
    for it in range(steps):
        mb = [mt[p].astype(BF16) for p in prs]
        if it + 1 < steps:
            prod = [bdot(cat([xt[p].astype(BF16), mb[p]], axis=0), mb[p]) for p in prs]
            xt = [xt[p] + prod[p][:LANES] for p in prs]
            mt = [prod[p][LANES:] for p in prs]
        else:
            xt = [xt[p] + bdot(xt[p].astype(BF16), mb[p]) for p in prs]
    x = [xt[p].T for p in prs]
    y0 = [jnp.where(incl, o0[p][L:], 0.0) for p in prs]
    y1 = [jnp.where(incl, o1[p][L:], 0.0) for p in prs]
    m_rb = [cat([jnp.where(left, y0[p], 0.0), jnp.where(left, 0.0, y1[p])], axis=0) for p in prs]
    m_rk = [cat([jnp.where(left, 0.0, y0[p]), jnp.where(left, y1[p], 0.0)], axis=0) for p in prs]
    ys = [cat([ars[p][L:2 * L], ars[p][3 * L:]], axis=0)
          + bdot(cat([m_rb[p], m_rk[p]], axis=1).astype(BF16),
                 cat([x[p].astype(BF16), vsw[p]], axis=0)) for p in prs]
    y_all = cat([ys[p][:L] + ys[p][L:] for p in prs], axis=1)
    sv_t = [cat([x[p][:L] + x[p][L:], v_all[:, sl]], axis=0).T.astype(BF16)
            for p, sl in enumerate(sls)]
    bh_all = (bvec_all * e_rem).astype(BF16)
    kh_all = (k2_all * e_rem).astype(BF16)
    bkh = [cat([bh_all[:, sl], kh_all[:, sl]], axis=0) for sl in sls]
    upd = [bdot(sv_t[p], bkh[p]) for p in prs]
    for p, sl in enumerate(sls):
        state[p] = st[p] * gam_all[:, sl] + jnp.where(same_head, upd[p], 0.0)

    yc = y_all - seg_mean(y_all)
    var = seg_mean(yc * yc)
    rk_sum = hn * seg_mean(r_all * k2_all * rk_ref[...])
    yn = yc * lax.rsqrt(var + GN_EPS) * lnw_ref[...] + lnb_ref[...]
    o_ref[...] = ((yn + rk_sum * v_all) * g_ref[...]).astype(o_ref.dtype)


def rwkv7_mixer(xs, gain, bsz, seq, mu, w_r, w_k, w_v, w_o, w0, w1, w2, a0, a1, a2, g1, g2,
                k_k, k_a, r_k, lnx_w, lnx_b):
    x_res = xs[0]
    n, d = x_res.shape
    xr, xw, xk, xv, xa, xg = token_shift_mix(x_res, gain, mu, bsz, seq)
    r = matmul(xr, w_r.astype(BF16))
    k = matmul(xk, w_k.astype(BF16))
    v = matmul(xv, w_v.astype(BF16))
    tw = matmul(xw, w1.astype(BF16), out_dtype=BF16, act="tanh", bn=w1.shape[1])
    wl = matmul(tw, w2.astype(BF16))
    ta = matmul(xa, a1.astype(BF16), out_dtype=BF16, bn=a1.shape[1])
    al = matmul(ta, a2.astype(BF16))
    gpad = (-GATE_LORA) % LANES
    g1p = jnp.pad(g1, ((0, 0), (0, gpad))).astype(BF16)
    g2p = jnp.pad(g2, ((0, gpad), (0, 0))).astype(BF16)
    tg = matmul(xg, g1p, out_dtype=BF16, act="sigmoid", bn=g1p.shape[1])
    gate = matmul(tg, g2p)

    L, pairs = CHUNK, 8
    nc = seq // L
    wide = LANES * pairs
    row = lambda a: a.reshape(1, d).astype(F32)
    dat = pl.BlockSpec((L, wide), lambda b, j, c: (b * nc + c, j))
    par = pl.BlockSpec((1, wide), lambda b, j, c: (0, j))
    yg = pl.pallas_call(
        functools.partial(_wkv_kernel, L=L, pairs=pairs),
        grid=(bsz, d // wide, nc),
        in_specs=[dat] * 6 + [par] * 7,
        out_specs=dat,
        out_shape=jax.ShapeDtypeStruct((n, d), BF16),
        scratch_shapes=[pltpu.VMEM((pairs, LANES, LANES), F32)],
        compiler_params=_cparams("parallel", "parallel", "arbitrary"),
    )(r, k, v, wl, al, gate, row(w0), row(a0), row(k_k), row(k_a), row(r_k), row(lnx_w), row(lnx_b))
    return matmul(yg, w_o.astype(BF16), res=x_res, alpha=1.0, emit_norm=True)


def kernel(x, positions, ffn_norm_0, ffn_w_gu_0, ffn_w_down_0, mix_norm_0, mla0_w_in, mla0_q_norm, mla0_w_uq, mla0_kv_norm, mla0_w_ukv, mla0_w_o, ffn_norm_1, ffn_w_gu_1, ffn_w_down_1, mix_norm_1, dsa1_w_in, dsa1_w_o, ffn_norm_2, ffn_w_gu_2, ffn_w_down_2, mix_norm_2, rwkv2_mu, rwkv2_w_r, rwkv2_w_k, rwkv2_w_v, rwkv2_w_o, rwkv2_w0, rwkv2_w1, rwkv2_w2, rwkv2_a0, rwkv2_a1, rwkv2_a2, rwkv2_g1, rwkv2_g2, rwkv2_k_k, rwkv2_k_a, rwkv2_r_k, rwkv2_lnx_w, rwkv2_lnx_b, ffn_norm_3, ffn_w_gu_3, ffn_w_down_3, mix_norm_3, mla3_w_in, mla3_q_norm, mla3_w_uq, mla3_kv_norm, mla3_w_ukv, mla3_w_o, final_norm):
    bsz, seq, d = x.shape
    n = bsz * seq
    pos_col = positions.reshape(n, 1)
    tabs_mla = rope_tables(pos_col, "spread64")
    tabs128 = rope_tables(pos_col, "half128")
    tabs64 = rope_tables(pos_col, "lead64")

    def mla(xs, gain, w_in, q_norm, w_uq, kv_norm, w_ukv, w_o):
        return mla_mixer(xs, gain, tabs_mla, bsz, seq, w_in, q_norm, w_uq, kv_norm, w_ukv, w_o)

    def rwkv(xs, gain):
        return rwkv7_mixer(xs, gain, bsz, seq, rwkv2_mu, rwkv2_w_r, rwkv2_w_k, rwkv2_w_v, rwkv2_w_o,
                           rwkv2_w0, rwkv2_w1, rwkv2_w2, rwkv2_a0, rwkv2_a1, rwkv2_a2,
                           rwkv2_g1, rwkv2_g2, rwkv2_k_k, rwkv2_k_a,
                           rwkv2_r_k.reshape(-1), rwkv2_lnx_w, rwkv2_lnx_b)

    mixers = [
        lambda xs, gn: mla(xs, gn, mla0_w_in, mla0_q_norm, mla0_w_uq, mla0_kv_norm, mla0_w_ukv, mla0_w_o),
        lambda xs, gn: dsa_mixer(xs, gn, tabs64, tabs128, bsz, seq, dsa1_w_in, dsa1_w_o),
        rwkv,
        lambda xs, gn: mla(xs, gn, mla3_w_in, mla3_q_norm, mla3_w_uq, mla3_kv_norm, mla3_w_ukv, mla3_w_o),
    ]
    ffns = [(ffn_norm_0, ffn_w_gu_0, ffn_w_down_0, mix_norm_0),
            (ffn_norm_1, ffn_w_gu_1, ffn_w_down_1, mix_norm_1),
            (ffn_norm_2, ffn_w_gu_2, ffn_w_down_2, mix_norm_2),
            (ffn_norm_3, ffn_w_gu_3, ffn_w_down_3, mix_norm_3)]
    xs = norm_prep(x.reshape(n, d))
    for i in range(4):
        f_norm, w_gu, w_down, m_norm = ffns[i]
        w_gu = (f_norm[:, :, None] * w_gu).astype(BF16)
        w_down = w_down.astype(BF16)
        xs = ffn_half_step(xs, w_gu, w_down, 0)
        xs = mixers[i](xs, m_norm)
        xs = ffn_half_step(xs, w_gu, w_down, 1)
    return rmsnorm(xs[0], final_norm, out_dtype=F32).reshape(bsz, seq, d)
```

```python
import functools

import numpy as np
import jax
import jax.numpy as jnp
from jax import lax
from jax.experimental import pallas as pl
from jax.experimental.pallas import tpu as pltpu

F32 = jnp.float32
BF16 = jnp.bfloat16
I32 = jnp.int32

CHUNK = 64
CHUNK_SHIFT = CHUNK.bit_length() - 1
ROPE_THETA = 10000.0
NORM_EPS = 1e-6
FFN_RES = 0.5
MLA_HEADS = 32
MLA_Q_LORA = 1024
MLA_KV_LORA = 512
MLA_NOPE = 128
MLA_ROPE = 64
MLA_V = 128
DSA_HEADS = 32
DSA_KV_HEADS = 8
DSA_HEAD_DIM = 128
IDX_HEADS = 32
IDX_DIM = 128
IDX_ROPE = 64
DSA_TOPK_MAX = 256
RWKV_HEAD = 64
GATE_LORA = 480
GN_EPS = 64e-5

LANES = 128
VMEM_LIMIT = 56 * 1024 * 1024
INT_MIN = -(2 ** 31)
NEG_BIG = -1e30
LOG2E = 1.4426950408889634

_NT = (((1,), (1,)), ((), ()))


def _cparams(*sem):
    return pltpu.CompilerParams(dimension_semantics=sem, vmem_limit_bytes=VMEM_LIMIT)


def _sigmoid(x):
    return 1.0 / (1.0 + jnp.exp(-x))


def _rmsnorm_kernel(x_ref, g_ref, o_ref):
    x = x_ref[...].astype(F32)
    ms = jnp.mean(x * x, axis=-1, keepdims=True)
    o_ref[...] = (x * lax.rsqrt(ms + NORM_EPS) * g_ref[...]).astype(o_ref.dtype)


def rmsnorm(x, g, *, width=None, col_block=0, out_dtype=BF16, rows=256):
    n = x.shape[0]
    w = x.shape[1] if width is None else width
    return pl.pallas_call(
        _rmsnorm_kernel,
        grid=(n // rows,),
        in_specs=[pl.BlockSpec((rows, w), lambda i: (i, col_block)),
                  pl.BlockSpec((1, w), lambda i: (0, 0))],
        out_specs=pl.BlockSpec((rows, w), lambda i: (i, 0)),
        out_shape=jax.ShapeDtypeStruct((n, w), out_dtype),
        compiler_params=_cparams("parallel"),
    )(x, g.reshape(1, w).astype(F32))


def _row_rstd(ss_ref, d_norm):
    return lax.rsqrt(jnp.sum(ss_ref[...], axis=-1, keepdims=True) * (1.0 / d_norm) + NORM_EPS)


def _lane_partial_sumsq(x):
    sq = x * x
    part = sq[:, :LANES]
    for c in range(1, x.shape[1] // LANES):
        part = part + sq[:, c * LANES:(c + 1) * LANES]
    return part


def _mm_kernel(*refs, act, alpha, has_res, d_norm, emit_norm):
    refs = list(refs)
    a_ref, w_ref = refs[:2]
    del refs[:2]
    ss_ref = refs.pop(0) if d_norm else None
    r_ref = refs.pop(0) if has_res else None
    gn_ref = refs.pop(0) if emit_norm else None
    o_ref = refs.pop(0)
    acc = jnp.dot(a_ref[...], w_ref[...].astype(BF16), preferred_element_type=F32)
    if d_norm:
        acc = acc * _row_rstd(ss_ref, d_norm)
    if act == "tanh":
        acc = jnp.tanh(acc)
    elif act == "sigmoid":
        acc = _sigmoid(acc)
    if has_res:
        acc = r_ref[...] + alpha * acc
    o_ref[...] = acc.astype(o_ref.dtype)
    if emit_norm:
        xb_ref, sso_ref = refs
        xb_ref[...] = (acc * gn_ref[...]).astype(BF16)
        part = _lane_partial_sumsq(acc)

        @pl.when(pl.program_id(1) == 0)
        def _():
            sso_ref[...] = part

        @pl.when(pl.program_id(1) != 0)
        def _():
            sso_ref[...] += part


def _mm_tiles(m, k, n, w_itemsize, n_out_tiles):
    bm = min(1024, m)
    a_bytes = 2 * bm * k * 2
    for bn in (1024, 512, 256, 128):
        w_bytes = bn * k * (2 * w_itemsize + (2 if w_itemsize > 2 else 0))
        o_bytes = bm * bn * 4 * (2 * n_out_tiles + 1)
        if n % bn == 0 and a_bytes + w_bytes + o_bytes <= VMEM_LIMIT * 0.85:
            return bm, bn
    return bm, n


def matmul(a, w, *, out_dtype=F32, act=None, res=None, alpha=1.0, bm=None, bn=None, w_index=None,
           col0=0, n=None, row_ss=None, next_gain=None):
    m, k = a.shape
    n = w.shape[-1] if n is None else n
    emit_norm = next_gain is not None
    tbm, tbn = _mm_tiles(m, k, n, w.dtype.itemsize, 1 + (res is not None) + emit_norm)
    bm = tbm if bm is None else bm
    bn = tbn if bn is None else bn
    cb = col0 // bn
    if w.ndim == 3:
        w_spec = pl.BlockSpec((None, k, bn), lambda i, j: (w_index, 0, cb + j))
    else:
        w_spec = pl.BlockSpec((k, bn), lambda i, j: (0, cb + j))
    row_spec = pl.BlockSpec((bm, LANES), lambda i, j: (i, 0))
    tile_spec = pl.BlockSpec((bm, bn), lambda i, j: (i, j))
    in_specs = [pl.BlockSpec((bm, k), lambda i, j: (i, 0)), w_spec]
    args = [a, w]
    if row_ss is not None:
        in_specs.append(row_spec)
        args.append(row_ss)
    if res is not None:
        in_specs.append(tile_spec)
        args.append(res)
    out_specs, out_shape = tile_spec, jax.ShapeDtypeStruct((m, n), out_dtype)
    if emit_norm:
        in_specs.append(pl.BlockSpec((1, bn), lambda i, j: (0, j)))
        args.append(next_gain.reshape(1, n).astype(F32))
        out_specs = [tile_spec, tile_spec, row_spec]
        out_shape = [out_shape, jax.ShapeDtypeStruct((m, n), BF16),
                     jax.ShapeDtypeStruct((m, LANES), F32)]
    return pl.pallas_call(
        functools.partial(_mm_kernel, act=act, alpha=alpha, has_res=res is not None,
                          d_norm=k if row_ss is not None else 0, emit_norm=emit_norm),
        grid=(m // bm, n // bn),
        in_specs=in_specs,
        out_specs=out_specs,
        out_shape=out_shape,
        compiler_params=_cparams("parallel", "arbitrary"),
    )(*args)


def _swiglu_kernel(a_ref, ss_ref, wg_ref, wu_ref, o_ref, *, d_norm):
    a = a_ref[...]
    rstd = _row_rstd(ss_ref, d_norm)
    g = jnp.dot(a, wg_ref[...].astype(BF16), preferred_element_type=F32) * rstd
    u = jnp.dot(a, wu_ref[...].astype(BF16), preferred_element_type=F32) * rstd
    o_ref[...] = (g * _sigmoid(g) * u).astype(o_ref.dtype)


def swiglu_up(a, row_ss, w_gu, half, *, bm=2048, bf=256):
    m, k = a.shape
    bm = min(bm, m)
    f = w_gu.shape[-1] // 2
    nf = f // bf
    return pl.pallas_call(
        functools.partial(_swiglu_kernel, d_norm=k),
        grid=(m // bm, nf),
        in_specs=[pl.BlockSpec((bm, k), lambda i, j: (i, 0), pipeline_mode=pl.Buffered(1)),
                  pl.BlockSpec((bm, LANES), lambda i, j: (i, 0)),
                  pl.BlockSpec((None, k, bf), lambda i, j: (half, 0, j)),
                  pl.BlockSpec((None, k, bf), lambda i, j: (half, 0, j + nf))],
        out_specs=pl.BlockSpec((bm, bf), lambda i, j: (i, j)),
        out_shape=jax.ShapeDtypeStruct((m, f), BF16),
        compiler_params=_cparams("parallel", "arbitrary"),
    )(a, row_ss, w_gu, w_gu)


def ffn_half_step(xs, w_gu, w_down, half, next_gain):
    x, xb, ss = xs
    act = swiglu_up(xb, ss, w_gu, half)
    return matmul(act, w_down, res=x, alpha=FFN_RES, bm=min(1024, x.shape[0]), bn=256,
                  w_index=half, next_gain=next_gain)


def _norm_prep_kernel(x_ref, g_ref, xb_ref, ss_ref):
    x = x_ref[...]
    xb_ref[...] = (x * g_ref[...]).astype(BF16)
    ss_ref[...] = _lane_partial_sumsq(x)


def norm_prep(x, gain, rows=256):
    n, d = x.shape
    xb, ss = pl.pallas_call(
        _norm_prep_kernel,
        grid=(n // rows,),
        in_specs=[pl.BlockSpec((rows, d), lambda i: (i, 0)),
                  pl.BlockSpec((1, d), lambda i: (0, 0))],
        out_specs=[pl.BlockSpec((rows, d), lambda i: (i, 0)),
                   pl.BlockSpec((rows, LANES), lambda i: (i, 0))],
        out_shape=[jax.ShapeDtypeStruct((n, d), BF16), jax.ShapeDtypeStruct((n, LANES), F32)],
        compiler_params=_cparams("parallel"),
    )(x, gain.reshape(1, d).astype(F32))
    return x, xb, ss


def _rope_lane_consts(layout):
    lane = np.arange(LANES)
    if layout == "half128":
        rot_dim, freq_idx, active = LANES, lane % 64, np.ones(LANES, bool)
        rolls = {64: np.where(lane < 64, -1.0, 1.0)}
    elif layout == "spread64":
        rot_dim, freq_idx, active = 64, lane % 32, (lane % 64) < 32
        rolls = {64: np.where(active, np.where(lane < 64, -1.0, 1.0), 0.0)}
    else:
        rot_dim, freq_idx, active = 64, lane % 32, lane < 64
        rolls = {32: np.where((lane >= 32) & (lane < 64), 1.0, 0.0),
                 96: np.where(lane < 32, -1.0, 0.0)}
    inv = np.where(active, ROPE_THETA ** (-(2.0 * freq_idx) / rot_dim), 0.0)
    signs = np.stack([rolls[k] for k in sorted(rolls)])
    return jnp.asarray(inv, F32).reshape(1, LANES), jnp.asarray(signs, F32), tuple(sorted(rolls))


def _rope_tab_kernel(pos_ref, inv_ref, sign_ref, c_ref, *s_refs):
    ang = pos_ref[...].astype(F32) * inv_ref[...]
    c_ref[...] = jnp.cos(ang)
    sin = jnp.sin(ang)
    for k, s_ref in enumerate(s_refs):
        s_ref[...] = sin * sign_ref[k:k + 1, :]


def rope_tables(pos_col, layout, rows=512):
    n = pos_col.shape[0]
    inv, signs, rolls = _rope_lane_consts(layout)
    tab = jax.ShapeDtypeStruct((n, LANES), F32)
    tabs = pl.pallas_call(
        _rope_tab_kernel,
        grid=(n // rows,),
        in_specs=[pl.BlockSpec((rows, 1), lambda i: (i, 0)),
                  pl.BlockSpec((1, LANES), lambda i: (0, 0)),
                  pl.BlockSpec((len(rolls), LANES), lambda i: (0, 0))],
        out_specs=[pl.BlockSpec((rows, LANES), lambda i: (i, 0))] * (1 + len(rolls)),
        out_shape=[tab] * (1 + len(rolls)),
        compiler_params=_cparams("parallel"),
    )(pos_col, inv, signs)
    return rolls, tabs


def _rope(x, c, s, rolls=(64,)):
    s = s if isinstance(s, (list, tuple)) else [s]
    y = x * c
    for shift, sk in zip(rolls, s):
        y = y + pltpu.roll(x, shift, 1) * sk
    return y


def _mla_attn_kernel(q_ref, kn_ref, v_ref, kr_ref, cq_ref, sq_ref, ck_ref, sk_ref,
                     o_ref, kfull, vfull, krs, *, tq, nq, hb, scale):
    hg = pl.program_id(1)
    qi = pl.program_id(2)

    @pl.when(jnp.logical_and(hg == 0, qi == 0))
    def _():
        krs[...] = _rope(kr_ref[...], ck_ref[...], sk_ref[...]).astype(BF16)

    @pl.when(qi == 0)
    def _():
        for hd in range(hb):
            kfull[hd, :, :LANES] = kn_ref[:, hd * LANES:(hd + 1) * LANES]
            kfull[hd, :, LANES:] = krs[...]
            vfull[hd, :, :LANES] = v_ref[:, hd * LANES:(hd + 1) * LANES]
            vfull[hd, :, LANES:] = jnp.ones((vfull.shape[1], LANES), BF16)

    cq = cq_ref[...]
    sq = sq_ref[...]
    diag_ok = ((lax.broadcasted_iota(I32, (tq, tq), 1) >> CHUNK_SHIFT)
               <= (lax.broadcasted_iota(I32, (tq, tq), 0) >> CHUNK_SHIFT))

    for i in range(nq):
        @pl.when(qi == i)
        def _(i=i):
            kv_len = (i + 1) * tq
            for hd in range(hb):
                base = hd * 2 * LANES
                qr = _rope(q_ref[:, base + LANES:base + 2 * LANES], cq, sq)
                q = jnp.concatenate([q_ref[:, base:base + LANES], qr], axis=1)
                q = (q * (scale * LOG2E)).astype(BF16)
                s = lax.dot_general(q, kfull[hd, :kv_len, :], _NT,
                                    preferred_element_type=F32)
                s_diag = jnp.where(diag_ok, s[:, kv_len - tq:], NEG_BIG)
                s = s_diag if i == 0 else jnp.concatenate([s[:, :kv_len - tq], s_diag], axis=1)
                p = jnp.exp2(s - jnp.max(s, axis=-1, keepdims=True))
                ol = jnp.dot(p.astype(BF16), vfull[hd, :kv_len, :],
                             preferred_element_type=F32)
                o_ref[:, hd * LANES:(hd + 1) * LANES] = (
                    ol[:, :LANES] / ol[:, LANES:LANES + 1]).astype(o_ref.dtype)


def mla_mixer(xs, next_gain, tabs64, bsz, seq, w_in, q_norm, w_uq, kv_norm, w_ukv, w_o):
    x_res, xb, ss = xs
    n = xb.shape[0]
    hq = MLA_HEADS
    lane = np.arange(LANES)
    rope_on = (lane % 64) < (MLA_ROPE // 2)
    rope_src = np.where(rope_on, (lane // 64) * (MLA_ROPE // 2) + lane % (MLA_ROPE // 2), 0)
    base = MLA_Q_LORA + MLA_KV_LORA
    kr_cols = w_in[:, base + rope_src] * jnp.asarray(rope_on, F32)
    w_in_p = jnp.concatenate([w_in[:, :base], kr_cols], axis=1).astype(BF16)
    hd = MLA_NOPE + MLA_ROPE
    idx = np.concatenate([np.concatenate([hh * hd + np.arange(MLA_NOPE),
                                          hh * hd + MLA_NOPE + rope_src]) for hh in range(hq)])
    msk = np.tile(np.concatenate([np.ones(MLA_NOPE, bool), rope_on]), hq)
    w_uq_p = (w_uq[:, idx] * jnp.asarray(msk, F32)).astype(BF16)
    w_ukv_p = w_ukv.reshape(MLA_KV_LORA, hq, 2, MLA_NOPE).transpose(0, 2, 1, 3)
    w_ukv_p = w_ukv_p.reshape(MLA_KV_LORA, 2 * hq * MLA_NOPE).astype(BF16)

    lat = matmul(xb, w_in_p, bm=512, bn=w_in_p.shape[1], row_ss=ss)
    cq = rmsnorm(lat, q_norm, width=MLA_Q_LORA, col_block=0)
    ckv = rmsnorm(lat, kv_norm, width=MLA_KV_LORA, col_block=MLA_Q_LORA // MLA_KV_LORA)
    q = matmul(cq, w_uq_p)
    kv = matmul(ckv, w_ukv_p, out_dtype=BF16)

    tq, hb = 256, 4
    nq = seq // tq
    ng = hq // hb
    c64, s64 = tabs64[1]
    kr_blk = base // LANES
    o = pl.pallas_call(
        functools.partial(_mla_attn_kernel, tq=tq, nq=nq, hb=hb, scale=float(hd) ** -0.5),
        grid=(bsz, ng, nq),
        in_specs=[pl.BlockSpec((tq, 2 * LANES * hb), lambda b, hh, i: (b * nq + i, hh)),
                  pl.BlockSpec((seq, LANES * hb), lambda b, hh, i: (b, hh)),
                  pl.BlockSpec((seq, LANES * hb), lambda b, hh, i: (b, ng + hh)),
                  pl.BlockSpec((seq, LANES), lambda b, hh, i: (b, kr_blk)),
                  pl.BlockSpec((tq, LANES), lambda b, hh, i: (b * nq + i, 0)),
                  pl.BlockSpec((tq, LANES), lambda b, hh, i: (b * nq + i, 0)),
                  pl.BlockSpec((seq, LANES), lambda b, hh, i: (b, 0)),
                  pl.BlockSpec((seq, LANES), lambda b, hh, i: (b, 0))],
        out_specs=pl.BlockSpec((tq, LANES * hb), lambda b, hh, i: (b * nq + i, hh)),
        out_shape=jax.ShapeDtypeStruct((n, hq * MLA_V), BF16),
        scratch_shapes=[pltpu.VMEM((hb, seq, 2 * LANES), BF16),
                        pltpu.VMEM((hb, seq, 2 * LANES), BF16),
                        pltpu.VMEM((seq, LANES), BF16)],
        compiler_params=_cparams("arbitrary", "arbitrary", "arbitrary"),
    )(q, kv, kv, lat, c64, s64, c64, s64)
    return matmul(o, w_o.astype(BF16), res=x_res, alpha=1.0, next_gain=next_gain)


def _rope_heads_kernel(x_ref, c_ref, *rest, groups, out_scale, rolls):
    s_refs, o_ref = rest[:-1], rest[-1]
    c = c_ref[...]
    s = [s_ref[...] for s_ref in s_refs]
    for g in range(groups):
        x = x_ref[:, g * LANES:(g + 1) * LANES]
        o_ref[0, g] = (_rope(x, c, s, rolls) * out_scale).astype(o_ref.dtype)


def rope_heads(x, tabs, bsz, seq, *, col0, heads, groups, rows=512, out_scale=1.0):
    nt = seq // rows
    cb = col0 // (LANES * groups)
    rolls, tables = tabs
    tab_spec = pl.BlockSpec((rows, LANES), lambda b, t, j: (b * nt + t, 0))
    return pl.pallas_call(
        functools.partial(_rope_heads_kernel, groups=groups, out_scale=out_scale, rolls=rolls),
        grid=(bsz, nt, heads // groups),
        in_specs=[pl.BlockSpec((rows, LANES * groups), lambda b, t, j: (b * nt + t, cb + j))]
        + [tab_spec] * len(tables),
        out_specs=pl.BlockSpec((1, groups, rows, LANES), lambda b, t, j: (b, j, t, 0)),
        out_shape=jax.ShapeDtypeStruct((bsz, heads, seq, LANES), BF16),
        compiler_params=_cparams("parallel", "parallel", "arbitrary"),
    )(x, *tables)


ONES_ROWS = 16


def _transpose_heads_kernel(x_ref, o_ref, *, groups):
    rows = x_ref.shape[0]
    for g in range(groups):
        o_ref[0, g, :LANES, :] = x_ref[:, g * LANES:(g + 1) * LANES].T.astype(o_ref.dtype)
        o_ref[0, g, LANES:, :] = jnp.ones((ONES_ROWS, rows), o_ref.dtype)


def transpose_heads(x, bsz, seq, *, col0, heads, rows=256):
    nt = seq // rows
    cb = col0 // (LANES * heads)
    return pl.pallas_call(
        functools.partial(_transpose_heads_kernel, groups=heads),
        grid=(bsz, nt),
        in_specs=[pl.BlockSpec((rows, LANES * heads), lambda b, t: (b * nt + t, cb))],
        out_specs=pl.BlockSpec((1, heads, LANES + ONES_ROWS, rows), lambda b, t: (b, 0, 0, t)),
        out_shape=jax.ShapeDtypeStruct((bsz, heads, LANES + ONES_ROWS, seq), BF16),
        compiler_params=_cparams("parallel", "arbitrary"),
    )(x)


def _dsa_kernel(qi_ref, ki_ref, wi_ref, q_ref, k_ref, vt_ref, o_ref, key_scr, ot_scr,
                *, tq, tk, nq, topk, rep, idx_scale):
    qt = pl.program_id(1)
    q_chunk = (qt * tq + lax.broadcasted_iota(I32, (tk, tq), 1)) >> CHUNK_SHIFT
    k_iota = lax.broadcasted_iota(I32, (tk, tq), 0)
    groups = DSA_KV_HEADS

    for nb in range(nq * tq // tk):
        @pl.when((qt * tq) // tk == nb)
        def _(nb=nb):
            nblk = nb + 1
            kv_len = nblk * tk

            def idx_body(j, _):
                off = pl.multiple_of(j * tk, tk)
                kib = ki_ref[0, 0, pl.ds(off, tk), :]
                acc = jnp.zeros((tk, tq), F32)
                for hh in range(IDX_HEADS):
                    d = lax.dot_general(kib, qi_ref[0, hh], _NT, preferred_element_type=F32)
                    acc = acc + (wi_ref[0, hh:hh + 1, :] * idx_scale) * jnp.maximum(d, 0.0)
                bits = pltpu.bitcast(acc, I32)
                key = bits ^ ((bits >> 31) & 0x7FFFFFFF)
                valid = ((off + k_iota) >> CHUNK_SHIFT) <= q_chunk
                key_scr[pl.ds(off, tk), :] = jnp.where(valid, key, INT_MIN)
                return 0

            lax.fori_loop(0, nblk, idx_body, 0)

            def count_ge(cand):
                hit = jnp.where(key_scr[:kv_len, :] >= cand, 1, 0).astype(I32)
                c8 = jnp.sum(hit.reshape(kv_len // 8, 8, tq), axis=0)
                return jnp.sum(c8, axis=0, keepdims=True)

            thr = jnp.full((1, tq), INT_MIN, I32)
            thr = jnp.where(count_ge(jnp.zeros((1, tq), I32)) >= topk, 0, thr)

            def bit_body(i, t):
                cand = t + jnp.left_shift(jnp.int32(1), 30 - i)
                return jnp.where(count_ge(cand) >= topk, cand, t)

            thr = lax.fori_loop(0, 31, bit_body, thr)
            thr = jnp.maximum(thr, INT_MIN + 1)

            def group_body(g, _):
                qg = q_ref[0, pl.ds(g * rep, rep)].reshape(rep * tq, DSA_HEAD_DIM)
                s = lax.dot_general(k_ref[0, g, :kv_len, :], qg, _NT,
                                    preferred_element_type=F32)
                sel = key_scr[:kv_len, :] >= thr
                s = jnp.where(jnp.concatenate([sel] * rep, axis=1), s, NEG_BIG)
                p = jnp.exp2(s - jnp.max(s, axis=0, keepdims=True))
                acc = jnp.dot(vt_ref[0, g, :, :kv_len], p.astype(BF16),
                              preferred_element_type=F32)
                ot_scr[g] = acc[:DSA_HEAD_DIM] / acc[DSA_HEAD_DIM:DSA_HEAD_DIM + 1]
                return 0

            lax.fori_loop(0, groups, group_body, 0)

    for g in range(groups):
        for r in range(rep):
            hh = g * rep + r
            o_ref[:, hh * DSA_HEAD_DIM:(hh + 1) * DSA_HEAD_DIM] = (
                ot_scr[g, :, r * tq:(r + 1) * tq].T.astype(o_ref.dtype))


def dsa_mixer(xs, next_gain, tabs64, tabs128, bsz, seq, w_in, w_o):
    x_res, xb, ss = xs
    n = xb.shape[0]
    hq, g, hd = DSA_HEADS, DSA_KV_HEADS, DSA_HEAD_DIM
    rep = hq // g
    topk = min(DSA_TOPK_MAX, seq // 4)
    n_q, n_kv, n_qi = hq * hd, g * hd, IDX_HEADS * IDX_DIM
    n_qkv = n_q + 2 * n_kv
    main = n_qkv + n_qi
    n_small = IDX_DIM + IDX_HEADS
    w_small = jnp.pad(w_in[:, main:main + n_small], ((0, 0), (0, 2 * LANES - n_small))).astype(BF16)

    qkv = matmul(xb, w_in, col0=0, n=n_qkv, row_ss=ss)
    qi = matmul(xb, w_in, col0=n_qkv, n=n_qi, row_ss=ss)
    small = matmul(xb, w_small, bn=2 * LANES, row_ss=ss)

    tq, tk = 128, 256
    q_hm = rope_heads(qkv, tabs128, bsz, seq, col0=0, heads=hq, groups=8,
                      out_scale=float(hd) ** -0.5 * LOG2E)
    k_hm = rope_heads(qkv, tabs128, bsz, seq, col0=n_q, heads=g, groups=8)
    vt = transpose_heads(qkv, bsz, seq, col0=n_q + n_kv, heads=g)
    qi_hm = rope_heads(qi, tabs64, bsz, seq, col0=0, heads=IDX_HEADS, groups=8)
    ki_hm = rope_heads(small, tabs64, bsz, seq, col0=0, heads=1, groups=1)
    wi_t = small[:, LANES:LANES + IDX_HEADS].reshape(bsz, seq, IDX_HEADS).transpose(0, 2, 1)

    nq = seq // tq
    o = pl.pallas_call(
        functools.partial(_dsa_kernel, tq=tq, tk=tk, nq=nq, topk=topk, rep=rep,
                          idx_scale=float(IDX_HEADS) ** -0.5 * float(IDX_DIM) ** -0.5),
        grid=(bsz, nq),
        in_specs=[pl.BlockSpec((1, IDX_HEADS, tq, LANES), lambda b, i: (b, 0, i, 0)),
                  pl.BlockSpec((1, 1, seq, LANES), lambda b, i: (b, 0, 0, 0)),
                  pl.BlockSpec((1, IDX_HEADS, tq), lambda b, i: (b, 0, i)),
                  pl.BlockSpec((1, hq, tq, LANES), lambda b, i: (b, 0, i, 0)),
                  pl.BlockSpec((1, g, seq, LANES), lambda b, i: (b, 0, 0, 0)),
                  pl.BlockSpec((1, g, LANES + ONES_ROWS, seq), lambda b, i: (b, 0, 0, 0))],
        out_specs=pl.BlockSpec((tq, hq * hd), lambda b, i: (b * nq + i, 0)),
        out_shape=jax.ShapeDtypeStruct((n, hq * hd), BF16),
        scratch_shapes=[pltpu.VMEM((seq, tq), I32),
                        pltpu.VMEM((g, hd, rep * tq), F32)],
        compiler_params=_cparams("parallel", "arbitrary"),
    )(qi_hm, ki_hm, wi_t, q_hm, k_hm, vt)
    return matmul(o, w_o.astype(BF16), res=x_res, alpha=1.0, next_gain=next_gain)


SUBLANES = 8


def _rms(x, g):
    ms = jnp.mean(x * x, axis=-1, keepdims=True)
    return x * lax.rsqrt(ms + NORM_EPS) * g


def _shift_mix_kernel(x_ref, xp_ref, g_ref, mu_ref, *o_refs):
    g = g_ref[...]
    h = _rms(x_ref[...], g)
    first_tile = pl.program_id(1) == 0
    last_prev = _rms(xp_ref[...], g)[SUBLANES - 1:SUBLANES, :]
    last_prev = jnp.where(first_tile, 0.0, last_prev)
    row = lax.broadcasted_iota(I32, h.shape, 0)
    h_prev = jnp.where(row == 0, last_prev, pltpu.roll(h, 1, 0))
    xx = h_prev - h
    for i, o_ref in enumerate(o_refs):
        o_ref[...] = (h + xx * mu_ref[i:i + 1, :]).astype(o_ref.dtype)


def token_shift_mix(x, gain, mu, bsz, seq, rows=256):
    n, d = x.shape
    k = mu.shape[0]
    nt = seq // rows
    per = rows // SUBLANES
    blk = pl.BlockSpec((rows, d), lambda b, t: (b * nt + t, 0))
    prev = pl.BlockSpec((SUBLANES, d), lambda b, t: (jnp.maximum((b * nt + t) * per - 1, 0), 0))
    return pl.pallas_call(
        _shift_mix_kernel,
        grid=(bsz, nt),
        in_specs=[blk, prev, pl.BlockSpec((1, d), lambda b, t: (0, 0)),
                  pl.BlockSpec((k, d), lambda b, t: (0, 0))],
        out_specs=[blk] * k,
        out_shape=[jax.ShapeDtypeStruct((n, d), BF16)] * k,
        compiler_params=_cparams("parallel", "arbitrary"),
    )(x, x, gain.reshape(1, d).astype(F32), mu)


def _split3(x):
    hi = x.astype(BF16)
    r1 = x - hi.astype(F32)
    mid = r1.astype(BF16)
    lo = (r1 - mid.astype(F32)).astype(BF16)
    return hi, mid, lo


def _dot3_right(x, w_bf16):
    hi, mid, lo = _split3(x)
    d = functools.partial(jnp.dot, preferred_element_type=F32)
    return d(hi, w_bf16) + d(mid, w_bf16) + d(lo, w_bf16)


def _dot3_left(w_bf16, x):
    hi, mid, lo = _split3(x)
    d = functools.partial(jnp.dot, preferred_element_type=F32)
    return d(w_bf16, hi) + d(w_bf16, mid) + d(w_bf16, lo)


def _wkv_kernel(r_ref, k_ref, v_ref, wl_ref, al_ref, g_ref,
                w0_ref, a0_ref, kk_ref, ka_ref, rk_ref, lnw_ref, lnb_ref,
                o_ref, state, *, L, pairs):
    c_idx = pl.program_id(2)

    @pl.when(c_idx == 0)
    def _():
        state[...] = jnp.zeros_like(state)

    hn = RWKV_HEAD
    hshift = hn.bit_length() - 1
    li = lax.broadcasted_iota(I32, (LANES, LANES), 0)
    lj = lax.broadcasted_iota(I32, (LANES, LANES), 1)
    same_head = (li >> hshift) == (lj >> hshift)
    seg_avg = jnp.where(same_head, 1.0 / hn, 0.0).astype(BF16)

    def seg_mean(x):
        lhs = jnp.concatenate([part[:, p * LANES:(p + 1) * LANES]
                               for part in _split3(x) for p in range(pairs)], axis=0)
        res = jnp.dot(lhs, seg_avg, preferred_element_type=F32)
        rows = pairs * L
        tot = res[:rows] + res[rows:2 * rows] + res[2 * rows:]
        return jnp.concatenate([tot[p * L:(p + 1) * L] for p in range(pairs)], axis=1)
    ti = lax.broadcasted_iota(I32, (L, L), 0)
    tj = lax.broadcasted_iota(I32, (L, L), 1)
    tril_incl = jnp.where(ti >= tj, 1.0, 0.0).astype(BF16)
    hr = lax.broadcasted_iota(I32, (L, LANES), 0)
    hc = lax.broadcasted_iota(I32, (L, LANES), 1)
    left = hc < hn
    strict = hr > (hc & (hn - 1))
    incl = hr >= (hc & (hn - 1))

    r_all = r_ref[...]
    k_all = k_ref[...]
    v_all = v_ref[...]
    z = w0_ref[...] + wl_ref[...]
    u = -z
    softplus = jnp.maximum(u, 0.0) + jnp.log(1.0 + jnp.exp(-jnp.abs(u)))
    w_log = -softplus - 0.5
    lw = -jnp.exp(w_log)
    a_sig = _sigmoid(a0_ref[...] + al_ref[...])
    k2_all = k_all * (1.0 + (a_sig - 1.0) * ka_ref[...])
    kk_all = k_all * kk_ref[...]
    csum = _dot3_left(tril_incl, lw)
    c_last = csum[L - 1:L, :]
    e_c = jnp.exp(csum)
    e_cprev = jnp.exp(csum - lw)
    e_neg = jnp.exp(-csum)
    e_rem = jnp.exp(c_last - csum)
    gam_all = jnp.exp(c_last)

    bdot = functools.partial(jnp.dot, preferred_element_type=F32)
    cat = jnp.concatenate
    prs = range(pairs)
    sls = [slice(p * LANES, (p + 1) * LANES) for p in prs]
    ksq = seg_mean(kk_all * kk_all)
    kk_n = kk_all / jnp.maximum(jnp.sqrt(hn * ksq), 1e-12)
    bvec_all = kk_n * a_sig
    at_all = -kk_n * e_cprev
    rt_all = r_all * e_c
    bt = [(bvec_all[:, sl] * e_neg[:, sl]).astype(BF16) for sl in sls]
    kt = [(k2_all[:, sl] * e_neg[:, sl]).astype(BF16) for sl in sls]
    a0 = [jnp.where(left, at_all[:, sl], 0.0).astype(BF16) for sl in sls]
    a1 = [jnp.where(left, 0.0, at_all[:, sl]).astype(BF16) for sl in sls]
    r0 = [jnp.where(left, rt_all[:, sl], 0.0).astype(BF16) for sl in sls]
    r1 = [jnp.where(left, 0.0, rt_all[:, sl]).astype(BF16) for sl in sls]
    vsw = [cat([jnp.where(left, 0.0, v_all[:, sl]), jnp.where(left, v_all[:, sl], 0.0)],
               axis=0).astype(BF16) for sl in sls]

    lhs = [cat([a0[p], r0[p], a1[p], r1[p]], axis=0) for p in prs]
    o01 = [lax.dot_general(lhs[p], cat([bt[p], kt[p]], axis=0), _NT,
                           preferred_element_type=F32) for p in prs]
    o0 = [o01[p][:2 * L] for p in prs]
    o1 = [pltpu.roll(o01[p][2 * L:], hn, 1) for p in prs]
    st = [state[p] for p in prs]
    ars = [lax.dot_general(lhs[p], st[p].astype(BF16), _NT,
                           preferred_element_type=F32) for p in prs]
    u0 = [jnp.where(strict, o0[p][:L], 0.0) for p in prs]
    u1 = [jnp.where(strict, o1[p][:L], 0.0) for p in prs]
    m_ab = [cat([jnp.where(left, u0[p], 0.0), jnp.where(left, 0.0, u1[p])], axis=0) for p in prs]
    m_ak = [cat([jnp.where(left, 0.0, u0[p]), jnp.where(left, u1[p], 0.0)], axis=0) for p in prs]
    xt = [(cat([ars[p][:L], ars[p][2 * L:3 * L]], axis=0)
           + bdot(m_ak[p].astype(BF16), vsw[p])).T for p in prs]
    mt = [m_ab[p].T for p in prs]
    steps = (L - 1).bit_length()
    for it in range(steps):
        mb = [mt[p].astype(BF16) for p in prs]
        if it + 1 < steps:
            prod = [bdot(cat([xt[p].astype(BF16), mb[p]], axis=0), mb[p]) for p in prs]
            xt = [xt[p] + prod[p][:LANES] for p in prs]
            mt = [prod[p][LANES:] for p in prs]
        else:
            xt = [xt[p] + bdot(xt[p].astype(BF16), mb[p]) for p in prs]
    x = [xt[p].T for p in prs]
    y0 = [jnp.where(incl, o0[p][L:], 0.0) for p in prs]
    y1 = [jnp.where(incl, o1[p][L:], 0.0) for p in prs]
    m_rb = [cat([jnp.where(left, y0[p], 0.0), jnp.where(left, 0.0, y1[p])], axis=0) for p in prs]
    m_rk = [cat([jnp.where(left, 0.0, y0[p]), jnp.where(left, y1[p], 0.0)], axis=0) for p in prs]
    ys = [cat([ars[p][L:2 * L], ars[p][3 * L:]], axis=0)
          + bdot(cat([m_rb[p], m_rk[p]], axis=1).astype(BF16),
                 cat([x[p].astype(BF16), vsw[p]], axis=0)) for p in prs]
    y_all = cat([ys[p][:L] + ys[p][L:] for p in prs], axis=1)
    sv_t = [cat([x[p][:L] + x[p][L:], v_all[:, sl]], axis=0).T.astype(BF16)
            for p, sl in enumerate(sls)]
    bh_all = (bvec_all * e_rem).astype(BF16)
    kh_all = (k2_all * e_rem).astype(BF16)
    bkh = [cat([bh_all[:, sl], kh_all[:, sl]], axis=0) for sl in sls]
    upd = [bdot(sv_t[p], bkh[p]) for p in prs]
    for p, sl in enumerate(sls):
        state[p] = st[p] * gam_all[:, sl] + jnp.where(same_head, upd[p], 0.0)

    yc = y_all - seg_mean(y_all)
    var = seg_mean(yc * yc)
    rk_sum = hn * seg_mean(r_all * k2_all * rk_ref[...])
    yn = yc * lax.rsqrt(var + GN_EPS) * lnw_ref[...] + lnb_ref[...]
    o_ref[...] = ((yn + rk_sum * v_all) * g_ref[...]).astype(o_ref.dtype)


def rwkv7_mixer(xs, gain, next_gain, bsz, seq, mu, w_r, w_k, w_v, w_o, w0, w1, w2, a0, a1, a2,
                g1, g2, k_k, k_a, r_k, lnx_w, lnx_b):
    x_res = xs[0]
    n, d = x_res.shape
    xr, xw, xk, xv, xa, xg = token_shift_mix(x_res, gain, mu, bsz, seq)
    r = matmul(xr, w_r)
    k = matmul(xk, w_k)
    v = matmul(xv, w_v)
    tw = matmul(xw, w1.astype(BF16), out_dtype=BF16, act="tanh", bn=w1.shape[1])
    wl = matmul(tw, w2.astype(BF16))
    ta = matmul(xa, a1.astype(BF16), out_dtype=BF16, bn=a1.shape[1])
    al = matmul(ta, a2.astype(BF16))
    gpad = (-GATE_LORA) % LANES
    g1p = jnp.pad(g1, ((0, 0), (0, gpad))).astype(BF16)
    g2p = jnp.pad(g2, ((0, gpad), (0, 0))).astype(BF16)
    tg = matmul(xg, g1p, out_dtype=BF16, act="sigmoid", bn=g1p.shape[1])
    gate = matmul(tg, g2p)

    L, pairs = CHUNK, 8
    nc = seq // L
    wide = LANES * pairs
    row = lambda a: a.reshape(1, d).astype(F32)
    dat = pl.BlockSpec((L, wide), lambda b, j, c: (b * nc + c, j))
    par = pl.BlockSpec((1, wide), lambda b, j, c: (0, j))
    yg = pl.pallas_call(
        functools.partial(_wkv_kernel, L=L, pairs=pairs),
        grid=(bsz, d // wide, nc),
        in_specs=[dat] * 6 + [par] * 7,
        out_specs=dat,
        out_shape=jax.ShapeDtypeStruct((n, d), BF16),
        scratch_shapes=[pltpu.VMEM((pairs, LANES, LANES), F32)],
        compiler_params=_cparams("parallel", "parallel", "arbitrary"),
    )(r, k, v, wl, al, gate, row(w0), row(a0), row(k_k), row(k_a), row(r_k), row(lnx_w), row(lnx_b))
    return matmul(yg, w_o.astype(BF16), res=x_res, alpha=1.0, next_gain=next_gain)


def kernel(x, positions, ffn_norm_0, ffn_w_gu_0, ffn_w_down_0, mix_norm_0, mla0_w_in, mla0_q_norm, mla0_w_uq, mla0_kv_norm, mla0_w_ukv, mla0_w_o, ffn_norm_1, ffn_w_gu_1, ffn_w_down_1, mix_norm_1, dsa1_w_in, dsa1_w_o, ffn_norm_2, ffn_w_gu_2, ffn_w_down_2, mix_norm_2, rwkv2_mu, rwkv2_w_r, rwkv2_w_k, rwkv2_w_v, rwkv2_w_o, rwkv2_w0, rwkv2_w1, rwkv2_w2, rwkv2_a0, rwkv2_a1, rwkv2_a2, rwkv2_g1, rwkv2_g2, rwkv2_k_k, rwkv2_k_a, rwkv2_r_k, rwkv2_lnx_w, rwkv2_lnx_b, ffn_norm_3, ffn_w_gu_3, ffn_w_down_3, mix_norm_3, mla3_w_in, mla3_q_norm, mla3_w_uq, mla3_kv_norm, mla3_w_ukv, mla3_w_o, final_norm):
    bsz, seq, d = x.shape
    n = bsz * seq
    pos_col = positions.reshape(n, 1)
    tabs_mla = rope_tables(pos_col, "spread64")
    tabs128 = rope_tables(pos_col, "half128")
    tabs64 = rope_tables(pos_col, "lead64")

    def mla(xs, nxt, w_in, q_norm, w_uq, kv_norm, w_ukv, w_o):
        return mla_mixer(xs, nxt, tabs_mla, bsz, seq, w_in, q_norm, w_uq, kv_norm, w_ukv, w_o)

    def rwkv(xs, gain, nxt):
        return rwkv7_mixer(xs, gain, nxt, bsz, seq, rwkv2_mu, rwkv2_w_r, rwkv2_w_k, rwkv2_w_v,
                           rwkv2_w_o, rwkv2_w0, rwkv2_w1, rwkv2_w2, rwkv2_a0, rwkv2_a1, rwkv2_a2,
                           rwkv2_g1, rwkv2_g2, rwkv2_k_k, rwkv2_k_a,
                           rwkv2_r_k.reshape(-1), rwkv2_lnx_w, rwkv2_lnx_b)

    mixers = [
        lambda xs, gn, nxt: mla(xs, nxt, mla0_w_in, mla0_q_norm, mla0_w_uq, mla0_kv_norm,
                                mla0_w_ukv, mla0_w_o),
        lambda xs, gn, nxt: dsa_mixer(xs, nxt, tabs64, tabs128, bsz, seq, dsa1_w_in, dsa1_w_o),
        rwkv,
        lambda xs, gn, nxt: mla(xs, nxt, mla3_w_in, mla3_q_norm, mla3_w_uq, mla3_kv_norm,
                                mla3_w_ukv, mla3_w_o),
    ]
    ffns = [(ffn_norm_0, ffn_w_gu_0, ffn_w_down_0, mix_norm_0),
            (ffn_norm_1, ffn_w_gu_1, ffn_w_down_1, mix_norm_1),
            (ffn_norm_2, ffn_w_gu_2, ffn_w_down_2, mix_norm_2),
            (ffn_norm_3, ffn_w_gu_3, ffn_w_down_3, mix_norm_3)]
    xs = norm_prep(x.reshape(n, d), ffn_norm_0[0])
    for i in range(4):
        f_norm, w_gu, w_down, m_norm = ffns[i]
        after_layer = ffns[i + 1][0][0] if i + 1 < len(ffns) else final_norm
        xs = ffn_half_step(xs, w_gu, w_down, 0, m_norm)
        xs = mixers[i](xs, m_norm, f_norm[1])
        xs = ffn_half_step(xs, w_gu, w_down, 1, after_layer)
    return rmsnorm(xs[0], final_norm, out_dtype=F32).reshape(bsz, seq, d)
```

```python
import functools

import numpy as np
import jax
import jax.numpy as jnp
from jax import lax
from jax.experimental import pallas as pl
from jax.experimental.pallas import tpu as pltpu

F32 = jnp.float32
BF16 = jnp.bfloat16
I32 = jnp.int32

CHUNK = 64
CHUNK_SHIFT = CHUNK.bit_length() - 1
ROPE_THETA = 10000.0
NORM_EPS = 1e-6
FFN_RES = 0.5
MLA_HEADS = 32
MLA_Q_LORA = 1024
MLA_KV_LORA = 512
MLA_NOPE = 128
MLA_ROPE = 64
MLA_V = 128
DSA_HEADS = 32
DSA_KV_HEADS = 8
DSA_HEAD_DIM = 128
IDX_HEADS = 32
IDX_DIM = 128
IDX_ROPE = 64
DSA_TOPK_MAX = 256
RWKV_HEAD = 64
GATE_LORA = 480
GN_EPS = 64e-5

LANES = 128
VMEM_LIMIT = 56 * 1024 * 1024
INT_MIN = -(2 ** 31)
NEG_BIG = -1e30
LOG2E = 1.4426950408889634

_NT = (((1,), (1,)), ((), ()))


def _cparams(*sem):
    return pltpu.CompilerParams(dimension_semantics=sem, vmem_limit_bytes=VMEM_LIMIT)


def _sigmoid(x):
    return 1.0 / (1.0 + jnp.exp(-x))


def _rmsnorm_kernel(x_ref, g_ref, o_ref):
    x = x_ref[...].astype(F32)
    ms = jnp.mean(x * x, axis=-1, keepdims=True)
    o_ref[...] = (x * lax.rsqrt(ms + NORM_EPS) * g_ref[...]).astype(o_ref.dtype)


def rmsnorm(x, g, *, width=None, col_block=0, out_dtype=BF16, rows=256):
    n = x.shape[0]
    w = x.shape[1] if width is None else width
    return pl.pallas_call(
        _rmsnorm_kernel,
        grid=(n // rows,),
        in_specs=[pl.BlockSpec((rows, w), lambda i: (i, col_block)),
                  pl.BlockSpec((1, w), lambda i: (0, 0))],
        out_specs=pl.BlockSpec((rows, w), lambda i: (i, 0)),
        out_shape=jax.ShapeDtypeStruct((n, w), out_dtype),
        compiler_params=_cparams("parallel"),
    )(x, g.reshape(1, w).astype(F32))


def _row_rstd(ss_ref, d_norm):
    return lax.rsqrt(jnp.sum(ss_ref[...], axis=-1, keepdims=True) * (1.0 / d_norm) + NORM_EPS)


def _lane_partial_sumsq(x):
    sq = x * x
    part = sq[:, :LANES]
    for c in range(1, x.shape[1] // LANES):
        part = part + sq[:, c * LANES:(c + 1) * LANES]
    return part


def _mm_kernel(*refs, act, alpha, has_res, d_norm, emit_norm):
    refs = list(refs)
    a_ref, w_ref = refs[:2]
    del refs[:2]
    ss_ref = refs.pop(0) if d_norm else None
    r_ref = refs.pop(0) if has_res else None
    gn_ref = refs.pop(0) if emit_norm else None
    o_ref = refs.pop(0)
    acc = jnp.dot(a_ref[...], w_ref[...].astype(BF16), preferred_element_type=F32)
    if d_norm:
        acc = acc * _row_rstd(ss_ref, d_norm)
    if act == "tanh":
        acc = jnp.tanh(acc)
    elif act == "sigmoid":
        acc = _sigmoid(acc)
    if has_res:
        acc = r_ref[...] + alpha * acc
    o_ref[...] = acc.astype(o_ref.dtype)
    if emit_norm:
        xb_ref, sso_ref = refs
        xb_ref[...] = (acc * gn_ref[...]).astype(BF16)
        part = _lane_partial_sumsq(acc)

        @pl.when(pl.program_id(1) == 0)
        def _():
            sso_ref[...] = part

        @pl.when(pl.program_id(1) != 0)
        def _():
            sso_ref[...] += part


def _mm_tiles(m, k, n, w_itemsize, n_out_tiles):
    bm = min(1024, m)
    a_bytes = 2 * bm * k * 2
    for bn in (1024, 512, 256, 128):
        w_bytes = bn * k * (2 * w_itemsize + (2 if w_itemsize > 2 else 0))
        o_bytes = bm * bn * 4 * (2 * n_out_tiles + 1)
        if n % bn == 0 and a_bytes + w_bytes + o_bytes <= VMEM_LIMIT * 0.85:
            return bm, bn
    return bm, n


def matmul(a, w, *, out_dtype=F32, act=None, res=None, alpha=1.0, bm=None, bn=None, w_index=None,
           row_ss=None, next_gain=None):
    m, k = a.shape
    n = w.shape[-1]
    emit_norm = next_gain is not None
    tbm, tbn = _mm_tiles(m, k, n, w.dtype.itemsize, 1 + (res is not None) + emit_norm)
    bm = tbm if bm is None else bm
    bn = tbn if bn is None else bn
    if w.ndim == 3:
        w_spec = pl.BlockSpec((None, k, bn), lambda i, j: (w_index, 0, j))
    else:
        w_spec = pl.BlockSpec((k, bn), lambda i, j: (0, j))
    row_spec = pl.BlockSpec((bm, LANES), lambda i, j: (i, 0))
    tile_spec = pl.BlockSpec((bm, bn), lambda i, j: (i, j))
    in_specs = [pl.BlockSpec((bm, k), lambda i, j: (i, 0)), w_spec]
    args = [a, w]
    if row_ss is not None:
        in_specs.append(row_spec)
        args.append(row_ss)
    if res is not None:
        in_specs.append(tile_spec)
        args.append(res)
    out_specs, out_shape = tile_spec, jax.ShapeDtypeStruct((m, n), out_dtype)
    if emit_norm:
        in_specs.append(pl.BlockSpec((1, bn), lambda i, j: (0, j)))
        args.append(next_gain.reshape(1, n).astype(F32))
        out_specs = [tile_spec, tile_spec, row_spec]
        out_shape = [out_shape, jax.ShapeDtypeStruct((m, n), BF16),
                     jax.ShapeDtypeStruct((m, LANES), F32)]
    return pl.pallas_call(
        functools.partial(_mm_kernel, act=act, alpha=alpha, has_res=res is not None,
                          d_norm=k if row_ss is not None else 0, emit_norm=emit_norm),
        grid=(m // bm, n // bn),
        in_specs=in_specs,
        out_specs=out_specs,
        out_shape=out_shape,
        compiler_params=_cparams("parallel", "arbitrary"),
    )(*args)


def _swiglu_kernel(a_ref, ss_ref, wg_ref, wu_ref, o_ref, *, d_norm):
    a = a_ref[...]
    rstd = _row_rstd(ss_ref, d_norm)
    g = jnp.dot(a, wg_ref[...].astype(BF16), preferred_element_type=F32) * rstd
    u = jnp.dot(a, wu_ref[...].astype(BF16), preferred_element_type=F32) * rstd
    o_ref[...] = (g * _sigmoid(g) * u).astype(o_ref.dtype)


def swiglu_up(a, row_ss, w_gu, half, *, bm=2048, bf=256):
    m, k = a.shape
    bm = min(bm, m)
    f = w_gu.shape[-1] // 2
    nf = f // bf
    return pl.pallas_call(
        functools.partial(_swiglu_kernel, d_norm=k),
        grid=(m // bm, nf),
        in_specs=[pl.BlockSpec((bm, k), lambda i, j: (i, 0), pipeline_mode=pl.Buffered(1)),
                  pl.BlockSpec((bm, LANES), lambda i, j: (i, 0)),
                  pl.BlockSpec((None, k, bf), lambda i, j: (half, 0, j)),
                  pl.BlockSpec((None, k, bf), lambda i, j: (half, 0, j + nf))],
        out_specs=pl.BlockSpec((bm, bf), lambda i, j: (i, j)),
        out_shape=jax.ShapeDtypeStruct((m, f), BF16),
        compiler_params=_cparams("parallel", "arbitrary"),
    )(a, row_ss, w_gu, w_gu)


def ffn_half_step(xs, w_gu, w_down, half, next_gain):
    x, xb, ss = xs
    act = swiglu_up(xb, ss, w_gu, half)
    return matmul(act, w_down, res=x, alpha=FFN_RES, bm=min(1024, x.shape[0]), bn=256,
                  w_index=half, next_gain=next_gain)


def _norm_prep_kernel(x_ref, g_ref, xb_ref, ss_ref):
    x = x_ref[...]
    xb_ref[...] = (x * g_ref[...]).astype(BF16)
    ss_ref[...] = _lane_partial_sumsq(x)


def norm_prep(x, gain, rows=256):
    n, d = x.shape
    xb, ss = pl.pallas_call(
        _norm_prep_kernel,
        grid=(n // rows,),
        in_specs=[pl.BlockSpec((rows, d), lambda i: (i, 0)),
                  pl.BlockSpec((1, d), lambda i: (0, 0))],
        out_specs=[pl.BlockSpec((rows, d), lambda i: (i, 0)),
                   pl.BlockSpec((rows, LANES), lambda i: (i, 0))],
        out_shape=[jax.ShapeDtypeStruct((n, d), BF16), jax.ShapeDtypeStruct((n, LANES), F32)],
        compiler_params=_cparams("parallel"),
    )(x, gain.reshape(1, d).astype(F32))
    return x, xb, ss


def _rope_lane_consts(layout):
    lane = np.arange(LANES)
    if layout == "half128":
        rot_dim, freq_idx, active = LANES, lane % 64, np.ones(LANES, bool)
        rolls = {64: np.where(lane < 64, -1.0, 1.0)}
    elif layout == "spread64":
        rot_dim, freq_idx, active = 64, lane % 32, (lane % 64) < 32
        rolls = {64: np.where(active, np.where(lane < 64, -1.0, 1.0), 0.0)}
    else:
        rot_dim, freq_idx, active = 64, lane % 32, lane < 64
        rolls = {32: np.where((lane >= 32) & (lane < 64), 1.0, 0.0),
                 96: np.where(lane < 32, -1.0, 0.0)}
    inv = np.where(active, ROPE_THETA ** (-(2.0 * freq_idx) / rot_dim), 0.0)
    signs = np.stack([rolls[k] for k in sorted(rolls)])
    return jnp.asarray(inv, F32).reshape(1, LANES), jnp.asarray(signs, F32), tuple(sorted(rolls))


def _rope_tab_kernel(pos_ref, inv_ref, sign_ref, c_ref, *s_refs):
    ang = pos_ref[...].astype(F32) * inv_ref[...]
    c_ref[...] = jnp.cos(ang)
    sin = jnp.sin(ang)
    for k, s_ref in enumerate(s_refs):
        s_ref[...] = sin * sign_ref[k:k + 1, :]


def rope_tables(pos_col, layout, rows=512):
    n = pos_col.shape[0]
    inv, signs, rolls = _rope_lane_consts(layout)
    tab = jax.ShapeDtypeStruct((n, LANES), F32)
    tabs = pl.pallas_call(
        _rope_tab_kernel,
        grid=(n // rows,),
        in_specs=[pl.BlockSpec((rows, 1), lambda i: (i, 0)),
                  pl.BlockSpec((1, LANES), lambda i: (0, 0)),
                  pl.BlockSpec((len(rolls), LANES), lambda i: (0, 0))],
        out_specs=[pl.BlockSpec((rows, LANES), lambda i: (i, 0))] * (1 + len(rolls)),
        out_shape=[tab] * (1 + len(rolls)),
        compiler_params=_cparams("parallel"),
    )(pos_col, inv, signs)
    return rolls, tabs


def _rope(x, c, s, rolls=(64,)):
    s = s if isinstance(s, (list, tuple)) else [s]
    y = x * c
    for shift, sk in zip(rolls, s):
        y = y + pltpu.roll(x, shift, 1) * sk
    return y


def _mla_attn_kernel(q_ref, kn_ref, v_ref, kr_ref, cq_ref, sq_ref, ck_ref, sk_ref,
                     o_ref, kfull, vfull, krs, *, tq, nq, hb, scale):
    hg = pl.program_id(1)
    qi = pl.program_id(2)

    @pl.when(jnp.logical_and(hg == 0, qi == 0))
    def _():
        krs[...] = _rope(kr_ref[...], ck_ref[...], sk_ref[...]).astype(BF16)

    @pl.when(qi == 0)
    def _():
        for hd in range(hb):
            kfull[hd, :, :LANES] = kn_ref[:, hd * LANES:(hd + 1) * LANES]
            kfull[hd, :, LANES:] = krs[...]
            vfull[hd, :, :LANES] = v_ref[:, hd * LANES:(hd + 1) * LANES]
            vfull[hd, :, LANES:] = jnp.ones((vfull.shape[1], LANES), BF16)

    cq = cq_ref[...]
    sq = sq_ref[...]
    diag_ok = ((lax.broadcasted_iota(I32, (tq, tq), 1) >> CHUNK_SHIFT)
               <= (lax.broadcasted_iota(I32, (tq, tq), 0) >> CHUNK_SHIFT))

    for i in range(nq):
        @pl.when(qi == i)
        def _(i=i):
            kv_len = (i + 1) * tq
            for hd in range(hb):
                base = hd * 2 * LANES
                qr = _rope(q_ref[:, base + LANES:base + 2 * LANES], cq, sq)
                q = jnp.concatenate([q_ref[:, base:base + LANES], qr], axis=1)
                q = (q * (scale * LOG2E)).astype(BF16)
                s = lax.dot_general(q, kfull[hd, :kv_len, :], _NT,
                                    preferred_element_type=F32)
                s_diag = jnp.where(diag_ok, s[:, kv_len - tq:], NEG_BIG)
                s = s_diag if i == 0 else jnp.concatenate([s[:, :kv_len - tq], s_diag], axis=1)
                p = jnp.exp2(s - jnp.max(s, axis=-1, keepdims=True))
                ol = jnp.dot(p.astype(BF16), vfull[hd, :kv_len, :],
                             preferred_element_type=F32)
                o_ref[:, hd * LANES:(hd + 1) * LANES] = (
                    ol[:, :LANES] / ol[:, LANES:LANES + 1]).astype(o_ref.dtype)


def mla_mixer(xs, next_gain, tabs64, bsz, seq, w_in, q_norm, w_uq, kv_norm, w_ukv, w_o):
    x_res, xb, ss = xs
    n = xb.shape[0]
    hq = MLA_HEADS
    lane = np.arange(LANES)
    rope_on = (lane % 64) < (MLA_ROPE // 2)
    rope_src = np.where(rope_on, (lane // 64) * (MLA_ROPE // 2) + lane % (MLA_ROPE // 2), 0)
    base = MLA_Q_LORA + MLA_KV_LORA
    kr_cols = w_in[:, base + rope_src] * jnp.asarray(rope_on, F32)
    w_in_p = jnp.concatenate([w_in[:, :base], kr_cols], axis=1).astype(BF16)
    hd = MLA_NOPE + MLA_ROPE
    idx = np.concatenate([np.concatenate([hh * hd + np.arange(MLA_NOPE),
                                          hh * hd + MLA_NOPE + rope_src]) for hh in range(hq)])
    msk = np.tile(np.concatenate([np.ones(MLA_NOPE, bool), rope_on]), hq)
    w_uq_p = (w_uq[:, idx] * jnp.asarray(msk, F32)).astype(BF16)
    w_ukv_p = w_ukv.reshape(MLA_KV_LORA, hq, 2, MLA_NOPE).transpose(0, 2, 1, 3)
    w_ukv_p = w_ukv_p.reshape(MLA_KV_LORA, 2 * hq * MLA_NOPE).astype(BF16)

    lat = matmul(xb, w_in_p, bm=512, bn=w_in_p.shape[1], row_ss=ss)
    cq = rmsnorm(lat, q_norm, width=MLA_Q_LORA, col_block=0)
    ckv = rmsnorm(lat, kv_norm, width=MLA_KV_LORA, col_block=MLA_Q_LORA // MLA_KV_LORA)
    q = matmul(cq, w_uq_p)
    kv = matmul(ckv, w_ukv_p, out_dtype=BF16)

    tq, hb = 256, 4
    nq = seq // tq
    ng = hq // hb
    c64, s64 = tabs64[1]
    kr_blk = base // LANES
    o = pl.pallas_call(
        functools.partial(_mla_attn_kernel, tq=tq, nq=nq, hb=hb, scale=float(hd) ** -0.5),
        grid=(bsz, ng, nq),
        in_specs=[pl.BlockSpec((tq, 2 * LANES * hb), lambda b, hh, i: (b * nq + i, hh)),
                  pl.BlockSpec((seq, LANES * hb), lambda b, hh, i: (b, hh)),
                  pl.BlockSpec((seq, LANES * hb), lambda b, hh, i: (b, ng + hh)),
                  pl.BlockSpec((seq, LANES), lambda b, hh, i: (b, kr_blk)),
                  pl.BlockSpec((tq, LANES), lambda b, hh, i: (b * nq + i, 0)),
                  pl.BlockSpec((tq, LANES), lambda b, hh, i: (b * nq + i, 0)),
                  pl.BlockSpec((seq, LANES), lambda b, hh, i: (b, 0)),
                  pl.BlockSpec((seq, LANES), lambda b, hh, i: (b, 0))],
        out_specs=pl.BlockSpec((tq, LANES * hb), lambda b, hh, i: (b * nq + i, hh)),
        out_shape=jax.ShapeDtypeStruct((n, hq * MLA_V), BF16),
        scratch_shapes=[pltpu.VMEM((hb, seq, 2 * LANES), BF16),
                        pltpu.VMEM((hb, seq, 2 * LANES), BF16),
                        pltpu.VMEM((seq, LANES), BF16)],
        compiler_params=_cparams("arbitrary", "arbitrary", "arbitrary"),
    )(q, kv, kv, lat, c64, s64, c64, s64)
    return matmul(o, w_o.astype(BF16), res=x_res, alpha=1.0, next_gain=next_gain)


ONES_ROWS = 16


def _proj_heads_kernel(a_ref, w_ref, ss_ref, *rest, d_norm, rolls, out_scale, transpose):
    tabs, o_ref = [t[...] for t in rest[:-1]], rest[-1]
    acc = lax.dot_general(a_ref[...], w_ref[...].astype(BF16), _NT, preferred_element_type=F32)
    acc = acc * _row_rstd(ss_ref, d_norm)
    for g in range(acc.shape[1] // LANES):
        blk = acc[:, g * LANES:(g + 1) * LANES]
        if transpose:
            o_ref[0, g, :LANES, :] = blk.T.astype(o_ref.dtype)
            o_ref[0, g, LANES:, :] = jnp.ones((ONES_ROWS, blk.shape[0]), o_ref.dtype)
        else:
            if tabs:
                blk = _rope(blk, tabs[0], tabs[1:], rolls)
            o_ref[0, g] = (blk * out_scale).astype(o_ref.dtype)


def project_heads(xb, row_ss, w_t, bsz, seq, *, col0, heads, tabs=None, out_scale=1.0,
                  transpose=False, out_dtype=BF16):
    m, k = xb.shape
    n = heads * LANES
    bm = min(1024, seq)
    bn = min(512, n)
    nt = seq // bm
    cb = col0 // bn
    rolls, tables = tabs if tabs is not None else ((), [])
    row_blk = lambda width: pl.BlockSpec((bm, width), lambda i, j: (i, 0))
    if transpose:
        out_spec = pl.BlockSpec((1, bn // LANES, LANES + ONES_ROWS, bm),
                                lambda i, j: (i // nt, j, 0, i % nt))
        out_shape = jax.ShapeDtypeStruct((bsz, heads, LANES + ONES_ROWS, seq), out_dtype)
    else:
        out_spec = pl.BlockSpec((1, bn // LANES, bm, LANES), lambda i, j: (i // nt, j, i % nt, 0))
        out_shape = jax.ShapeDtypeStruct((bsz, heads, seq, LANES), out_dtype)
    return pl.pallas_call(
        functools.partial(_proj_heads_kernel, d_norm=k, rolls=rolls, out_scale=out_scale,
                          transpose=transpose),
        grid=(m // bm, n // bn),
        in_specs=[row_blk(k), pl.BlockSpec((bn, k), lambda i, j: (cb + j, 0)), row_blk(LANES)]
        + [row_blk(LANES)] * len(tables),
        out_specs=out_spec,
        out_shape=out_shape,
        compiler_params=_cparams("parallel", "arbitrary"),
    )(xb, w_t, row_ss, *tables)


def _dsa_kernel(qi_ref, ki_ref, wi_ref, q_ref, k_ref, vt_ref, o_ref, key_scr, ot_scr,
                *, tq, tk, nq, topk, rep, idx_scale):
    qt = pl.program_id(1)
    q_chunk = (qt * tq + lax.broadcasted_iota(I32, (tk, tq), 1)) >> CHUNK_SHIFT
    k_iota = lax.broadcasted_iota(I32, (tk, tq), 0)
    groups = DSA_KV_HEADS

    for nb in range(nq * tq // tk):
        @pl.when((qt * tq) // tk == nb)
        def _(nb=nb):
            nblk = nb + 1
            kv_len = nblk * tk

            def idx_body(j, _):
                off = pl.multiple_of(j * tk, tk)
                kib = ki_ref[0, 0, pl.ds(off, tk), :]
                acc = jnp.zeros((tk, tq), F32)
                for hh in range(IDX_HEADS):
                    d = lax.dot_general(kib, qi_ref[0, hh], _NT, preferred_element_type=F32)
                    acc = acc + (wi_ref[0, hh:hh + 1, :] * idx_scale) * jnp.maximum(d, 0.0)
                bits = pltpu.bitcast(acc, I32)
                key = bits ^ ((bits >> 31) & 0x7FFFFFFF)
                valid = ((off + k_iota) >> CHUNK_SHIFT) <= q_chunk
                key_scr[pl.ds(off, tk), :] = jnp.where(valid, key, INT_MIN)
                return 0

            lax.fori_loop(0, nblk, idx_body, 0)

            def count_ge(cand):
                hit = jnp.where(key_scr[:kv_len, :] >= cand, 1, 0).astype(I32)
                c8 = jnp.sum(hit.reshape(kv_len // 8, 8, tq), axis=0)
                return jnp.sum(c8, axis=0, keepdims=True)

            thr = jnp.full((1, tq), INT_MIN, I32)
            thr = jnp.where(count_ge(jnp.zeros((1, tq), I32)) >= topk, 0, thr)

            def bit_body(i, t):
                cand = t + jnp.left_shift(jnp.int32(1), 30 - i)
                return jnp.where(count_ge(cand) >= topk, cand, t)

            thr = lax.fori_loop(0, 31, bit_body, thr)
            thr = jnp.maximum(thr, INT_MIN + 1)

            def group_body(g, _):
                qg = q_ref[0, pl.ds(g * rep, rep)].reshape(rep * tq, DSA_HEAD_DIM)
                s = lax.dot_general(k_ref[0, g, :kv_len, :], qg, _NT,
                                    preferred_element_type=F32)
                sel = key_scr[:kv_len, :] >= thr
                s = jnp.where(jnp.concatenate([sel] * rep, axis=1), s, NEG_BIG)
                p = jnp.exp2(s - jnp.max(s, axis=0, keepdims=True))
                acc = jnp.dot(vt_ref[0, g, :, :kv_len], p.astype(BF16),
                              preferred_element_type=F32)
                ot_scr[g] = acc[:DSA_HEAD_DIM] / acc[DSA_HEAD_DIM:DSA_HEAD_DIM + 1]
                return 0

            lax.fori_loop(0, groups, group_body, 0)

    for g in range(groups):
        for r in range(rep):
            hh = g * rep + r
            o_ref[:, hh * DSA_HEAD_DIM:(hh + 1) * DSA_HEAD_DIM] = (
                ot_scr[g, :, r * tq:(r + 1) * tq].T.astype(o_ref.dtype))


def dsa_mixer(xs, next_gain, tabs64, tabs128, bsz, seq, w_in, w_o):
    x_res, xb, ss = xs
    n = xb.shape[0]
    hq, g, hd = DSA_HEADS, DSA_KV_HEADS, DSA_HEAD_DIM
    rep = hq // g
    topk = min(DSA_TOPK_MAX, seq // 4)
    n_q, n_kv, n_qi = hq * hd, g * hd, IDX_HEADS * IDX_DIM
    n_qkv = n_q + 2 * n_kv
    main = n_qkv + n_qi

    proj = functools.partial(project_heads, xb, ss, w_in.T, bsz, seq)
    q_hm = proj(col0=0, heads=hq, tabs=tabs128, out_scale=float(hd) ** -0.5 * LOG2E)
    k_hm = proj(col0=n_q, heads=g, tabs=tabs128)
    vt = proj(col0=n_q + n_kv, heads=g, transpose=True)
    qi_hm = proj(col0=n_qkv, heads=IDX_HEADS, tabs=tabs64)
    ki_hm = proj(col0=main, heads=1, tabs=tabs64)
    wi = proj(col0=main + IDX_DIM, heads=1, out_dtype=F32)
    wi_t = wi[:, 0, :, :IDX_HEADS].transpose(0, 2, 1)
    tq, tk = 128, 256

    nq = seq // tq
    o = pl.pallas_call(
        functools.partial(_dsa_kernel, tq=tq, tk=tk, nq=nq, topk=topk, rep=rep,
                          idx_scale=float(IDX_HEADS) ** -0.5 * float(IDX_DIM) ** -0.5),
        grid=(bsz, nq),
        in_specs=[pl.BlockSpec((1, IDX_HEADS, tq, LANES), lambda b, i: (b, 0, i, 0)),
                  pl.BlockSpec((1, 1, seq, LANES), lambda b, i: (b, 0, 0, 0)),
                  pl.BlockSpec((1, IDX_HEADS, tq), lambda b, i: (b, 0, i)),
                  pl.BlockSpec((1, hq, tq, LANES), lambda b, i: (b, 0, i, 0)),
                  pl.BlockSpec((1, g, seq, LANES), lambda b, i: (b, 0, 0, 0)),
                  pl.BlockSpec((1, g, LANES + ONES_ROWS, seq), lambda b, i: (b, 0, 0, 0))],
        out_specs=pl.BlockSpec((tq, hq * hd), lambda b, i: (b * nq + i, 0)),
        out_shape=jax.ShapeDtypeStruct((n, hq * hd), BF16),
        scratch_shapes=[pltpu.VMEM((seq, tq), I32),
                        pltpu.VMEM((g, hd, rep * tq), F32)],
        compiler_params=_cparams("parallel", "arbitrary"),
    )(qi_hm, ki_hm, wi_t, q_hm, k_hm, vt)
    return matmul(o, w_o.astype(BF16), res=x_res, alpha=1.0, next_gain=next_gain)


SUBLANES = 8


def _rms(x, g):
    ms = jnp.mean(x * x, axis=-1, keepdims=True)
    return x * lax.rsqrt(ms + NORM_EPS) * g


def _shift_mix_kernel(x_ref, xp_ref, g_ref, mu_ref, *o_refs):
    g = g_ref[...]
    h = _rms(x_ref[...], g)
    first_tile = pl.program_id(1) == 0
    last_prev = _rms(xp_ref[...], g)[SUBLANES - 1:SUBLANES, :]
    last_prev = jnp.where(first_tile, 0.0, last_prev)
    row = lax.broadcasted_iota(I32, h.shape, 0)
    h_prev = jnp.where(row == 0, last_prev, pltpu.roll(h, 1, 0))
    xx = h_prev - h
    for i, o_ref in enumerate(o_refs):
        o_ref[...] = (h + xx * mu_ref[i:i + 1, :]).astype(o_ref.dtype)


def token_shift_mix(x, gain, mu, bsz, seq, rows=256):
    n, d = x.shape
    k = mu.shape[0]
    nt = seq // rows
    per = rows // SUBLANES
    blk = pl.BlockSpec((rows, d), lambda b, t: (b * nt + t, 0))
    prev = pl.BlockSpec((SUBLANES, d), lambda b, t: (jnp.maximum((b * nt + t) * per - 1, 0), 0))
    return pl.pallas_call(
        _shift_mix_kernel,
        grid=(bsz, nt),
        in_specs=[blk, prev, pl.BlockSpec((1, d), lambda b, t: (0, 0)),
                  pl.BlockSpec((k, d), lambda b, t: (0, 0))],
        out_specs=[blk] * k,
        out_shape=[jax.ShapeDtypeStruct((n, d), BF16)] * k,
        compiler_params=_cparams("parallel", "arbitrary"),
    )(x, x, gain.reshape(1, d).astype(F32), mu)


def _split3(x):
    hi = x.astype(BF16)
    r1 = x - hi.astype(F32)
    mid = r1.astype(BF16)
    lo = (r1 - mid.astype(F32)).astype(BF16)
    return hi, mid, lo


def _dot3_right(x, w_bf16):
    hi, mid, lo = _split3(x)
    d = functools.partial(jnp.dot, preferred_element_type=F32)
    return d(hi, w_bf16) + d(mid, w_bf16) + d(lo, w_bf16)


def _dot3_left(w_bf16, x):
    hi, mid, lo = _split3(x)
    d = functools.partial(jnp.dot, preferred_element_type=F32)
    return d(w_bf16, hi) + d(w_bf16, mid) + d(w_bf16, lo)


def _wkv_kernel(r_ref, k_ref, v_ref, wl_ref, al_ref, g_ref,
                w0_ref, a0_ref, kk_ref, ka_ref, rk_ref, lnw_ref, lnb_ref,
                o_ref, state, *, L, pairs):
    c_idx = pl.program_id(2)

    @pl.when(c_idx == 0)
    def _():
        state[...] = jnp.zeros_like(state)

    hn = RWKV_HEAD
    hshift = hn.bit_length() - 1
    li = lax.broadcasted_iota(I32, (LANES, LANES), 0)
    lj = lax.broadcasted_iota(I32, (LANES, LANES), 1)
    same_head = (li >> hshift) == (lj >> hshift)
    seg_avg = jnp.where(same_head, 1.0 / hn, 0.0).astype(BF16)

    def seg_mean(x):
        lhs = jnp.concatenate([part[:, p * LANES:(p + 1) * LANES]
                               for part in _split3(x) for p in range(pairs)], axis=0)
        res = jnp.dot(lhs, seg_avg, preferred_element_type=F32)
        rows = pairs * L
        tot = res[:rows] + res[rows:2 * rows] + res[2 * rows:]
        return jnp.concatenate([tot[p * L:(p + 1) * L] for p in range(pairs)], axis=1)
    ti = lax.broadcasted_iota(I32, (L, L), 0)
    tj = lax.broadcasted_iota(I32, (L, L), 1)
    tril_incl = jnp.where(ti >= tj, 1.0, 0.0).astype(BF16)
    hr = lax.broadcasted_iota(I32, (L, LANES), 0)
    hc = lax.broadcasted_iota(I32, (L, LANES), 1)
    left = hc < hn
    strict = hr > (hc & (hn - 1))
    incl = hr >= (hc & (hn - 1))

    r_all = r_ref[...]
    k_all = k_ref[...]
    v_all = v_ref[...]
    z = w0_ref[...] + wl_ref[...]
    u = -z
    softplus = jnp.maximum(u, 0.0) + jnp.log(1.0 + jnp.exp(-jnp.abs(u)))
    w_log = -softplus - 0.5
    lw = -jnp.exp(w_log)
    a_sig = _sigmoid(a0_ref[...] + al_ref[...])
    k2_all = k_all * (1.0 + (a_sig - 1.0) * ka_ref[...])
    kk_all = k_all * kk_ref[...]
    csum = _dot3_left(tril_incl, lw)
    c_last = csum[L - 1:L, :]
    e_c = jnp.exp(csum)
    e_cprev = jnp.exp(csum - lw)
    e_neg = jnp.exp(-csum)
    e_rem = jnp.exp(c_last - csum)
    gam_all = jnp.exp(c_last)

    bdot = functools.partial(jnp.dot, preferred_element_type=F32)
    cat = jnp.concatenate
    prs = range(pairs)
    sls = [slice(p * LANES, (p + 1) * LANES) for p in prs]
    ksq = seg_mean(kk_all * kk_all)
    kk_n = kk_all / jnp.maximum(jnp.sqrt(hn * ksq), 1e-12)
    bvec_all = kk_n * a_sig
    at_all = -kk_n * e_cprev
    rt_all = r_all * e_c
    bt = [(bvec_all[:, sl] * e_neg[:, sl]).astype(BF16) for sl in sls]
    kt = [(k2_all[:, sl] * e_neg[:, sl]).astype(BF16) for sl in sls]
    a0 = [jnp.where(left, at_all[:, sl], 0.0).astype(BF16) for sl in sls]
    a1 = [jnp.where(left, 0.0, at_all[:, sl]).astype(BF16) for sl in sls]
    r0 = [jnp.where(left, rt_all[:, sl], 0.0).astype(BF16) for sl in sls]
    r1 = [jnp.where(left, 0.0, rt_all[:, sl]).astype(BF16) for sl in sls]
    vsw = [cat([jnp.where(left, 0.0, v_all[:, sl]), jnp.where(left, v_all[:, sl], 0.0)],
               axis=0).astype(BF16) for sl in sls]

    lhs = [cat([a0[p], r0[p], a1[p], r1[p]], axis=0) for p in prs]
    o01 = [lax.dot_general(lhs[p], cat([bt[p], kt[p]], axis=0), _NT,
                           preferred_element_type=F32) for p in prs]
    o0 = [o01[p][:2 * L] for p in prs]
    o1 = [pltpu.roll(o01[p][2 * L:], hn, 1) for p in prs]
    st = [state[p] for p in prs]
    ars = [lax.dot_general(lhs[p], st[p].astype(BF16), _NT,
                           preferred_element_type=F32) for p in prs]
    u0 = [jnp.where(strict, o0[p][:L], 0.0) for p in prs]
    u1 = [jnp.where(strict, o1[p][:L], 0.0) for p in prs]
    m_ab = [cat([jnp.where(left, u0[p], 0.0), jnp.where(left, 0.0, u1[p])], axis=0) for p in prs]
    m_ak = [cat([jnp.where(left, 0.0, u0[p]), jnp.where(left, u1[p], 0.0)], axis=0) for p in prs]
    xt = [(cat([ars[p][:L], ars[p][2 * L:3 * L]], axis=0)
           + bdot(m_ak[p].astype(BF16), vsw[p])).T for p in prs]
    mt = [m_ab[p].T for p in prs]
    steps = (L - 1).bit_length()
    for it in range(steps):
        mb = [mt[p].astype(BF16) for p in prs]
        if it + 1 < steps:
            prod = [bdot(cat([xt[p].astype(BF16), mb[p]], axis=0), mb[p]) for p in prs]
            xt = [xt[p] + prod[p][:LANES] for p in prs]
            mt = [prod[p][LANES:] for p in prs]
        else:
            xt = [xt[p] + bdot(xt[p].astype(BF16), mb[p]) for p in prs]
    x = [xt[p].T for p in prs]
    y0 = [jnp.where(incl, o0[p][L:], 0.0) for p in prs]
    y1 = [jnp.where(incl, o1[p][L:], 0.0) for p in prs]
    m_rb = [cat([jnp.where(left, y0[p], 0.0), jnp.where(left, 0.0, y1[p])], axis=0) for p in prs]
    m_rk = [cat([jnp.where(left, 0.0, y0[p]), jnp.where(left, y1[p], 0.0)], axis=0) for p in prs]
    ys = [cat([ars[p][L:2 * L], ars[p][3 * L:]], axis=0)
          + bdot(cat([m_rb[p], m_rk[p]], axis=1).astype(BF16),
                 cat([x[p].astype(BF16), vsw[p]], axis=0)) for p in prs]
    y_all = cat([ys[p][:L] + ys[p][L:] for p in prs], axis=1)
    sv_t = [cat([x[p][:L] + x[p][L:], v_all[:, sl]], axis=0).T.astype(BF16)
            for p, sl in enumerate(sls)]
    bh_all = (bvec_all * e_rem).astype(BF16)
    kh_all = (k2_all * e_rem).astype(BF16)
    bkh = [cat([bh_all[:, sl], kh_all[:, sl]], axis=0) for sl in sls]
    upd = [bdot(sv_t[p], bkh[p]) for p in prs]
    for p, sl in enumerate(sls):
        state[p] = st[p] * gam_all[:, sl] + jnp.where(same_head, upd[p], 0.0)

    yc = y_all - seg_mean(y_all)
    var = seg_mean(yc * yc)
    rk_sum = hn * seg_mean(r_all * k2_all * rk_ref[...])
    yn = yc * lax.rsqrt(var + GN_EPS) * lnw_ref[...] + lnb_ref[...]
    o_ref[...] = ((yn + rk_sum * v_all) * g_ref[...]).astype(o_ref.dtype)


def rwkv7_mixer(xs, gain, next_gain, bsz, seq, mu, w_r, w_k, w_v, w_o, w0, w1, w2, a0, a1, a2,
                g1, g2, k_k, k_a, r_k, lnx_w, lnx_b):
    x_res = xs[0]
    n, d = x_res.shape
    xr, xw, xk, xv, xa, xg = token_shift_mix(x_res, gain, mu, bsz, seq)
    r = matmul(xr, w_r)
    k = matmul(xk, w_k)
    v = matmul(xv, w_v)
    tw = matmul(xw, w1.astype(BF16), out_dtype=BF16, act="tanh", bn=w1.shape[1])
    wl = matmul(tw, w2.astype(BF16))
    ta = matmul(xa, a1.astype(BF16), out_dtype=BF16, bn=a1.shape[1])
    al = matmul(ta, a2.astype(BF16))
    gpad = (-GATE_LORA) % LANES
    g1p = jnp.pad(g1, ((0, 0), (0, gpad))).astype(BF16)
    g2p = jnp.pad(g2, ((0, gpad), (0, 0))).astype(BF16)
    tg = matmul(xg, g1p, out_dtype=BF16, act="sigmoid", bn=g1p.shape[1])
    gate = matmul(tg, g2p)

    L, pairs = CHUNK, 8
    nc = seq // L
    wide = LANES * pairs
    row = lambda a: a.reshape(1, d).astype(F32)
    dat = pl.BlockSpec((L, wide), lambda b, j, c: (b * nc + c, j))
    par = pl.BlockSpec((1, wide), lambda b, j, c: (0, j))
    yg = pl.pallas_call(
        functools.partial(_wkv_kernel, L=L, pairs=pairs),
        grid=(bsz, d // wide, nc),
        in_specs=[dat] * 6 + [par] * 7,
        out_specs=dat,
        out_shape=jax.ShapeDtypeStruct((n, d), BF16),
        scratch_shapes=[pltpu.VMEM((pairs, LANES, LANES), F32)],
        compiler_params=_cparams("parallel", "parallel", "arbitrary"),
    )(r, k, v, wl, al, gate, row(w0), row(a0), row(k_k), row(k_a), row(r_k), row(lnx_w), row(lnx_b))
    return matmul(yg, w_o.astype(BF16), res=x_res, alpha=1.0, next_gain=next_gain)


def kernel(x, positions, ffn_norm_0, ffn_w_gu_0, ffn_w_down_0, mix_norm_0, mla0_w_in, mla0_q_norm, mla0_w_uq, mla0_kv_norm, mla0_w_ukv, mla0_w_o, ffn_norm_1, ffn_w_gu_1, ffn_w_down_1, mix_norm_1, dsa1_w_in, dsa1_w_o, ffn_norm_2, ffn_w_gu_2, ffn_w_down_2, mix_norm_2, rwkv2_mu, rwkv2_w_r, rwkv2_w_k, rwkv2_w_v, rwkv2_w_o, rwkv2_w0, rwkv2_w1, rwkv2_w2, rwkv2_a0, rwkv2_a1, rwkv2_a2, rwkv2_g1, rwkv2_g2, rwkv2_k_k, rwkv2_k_a, rwkv2_r_k, rwkv2_lnx_w, rwkv2_lnx_b, ffn_norm_3, ffn_w_gu_3, ffn_w_down_3, mix_norm_3, mla3_w_in, mla3_q_norm, mla3_w_uq, mla3_kv_norm, mla3_w_ukv, mla3_w_o, final_norm):
    bsz, seq, d = x.shape
    n = bsz * seq
    pos_col = positions.reshape(n, 1)
    tabs_mla = rope_tables(pos_col, "spread64")
    tabs128 = rope_tables(pos_col, "half128")
    tabs64 = rope_tables(pos_col, "lead64")

    def mla(xs, nxt, w_in, q_norm, w_uq, kv_norm, w_ukv, w_o):
        return mla_mixer(xs, nxt, tabs_mla, bsz, seq, w_in, q_norm, w_uq, kv_norm, w_ukv, w_o)

    def rwkv(xs, gain, nxt):
        return rwkv7_mixer(xs, gain, nxt, bsz, seq, rwkv2_mu, rwkv2_w_r, rwkv2_w_k, rwkv2_w_v,
                           rwkv2_w_o, rwkv2_w0, rwkv2_w1, rwkv2_w2, rwkv2_a0, rwkv2_a1, rwkv2_a2,
                           rwkv2_g1, rwkv2_g2, rwkv2_k_k, rwkv2_k_a,
                           rwkv2_r_k.reshape(-1), rwkv2_lnx_w, rwkv2_lnx_b)

    mixers = [
        lambda xs, gn, nxt: mla(xs, nxt, mla0_w_in, mla0_q_norm, mla0_w_uq, mla0_kv_norm,
                                mla0_w_ukv, mla0_w_o),
        lambda xs, gn, nxt: dsa_mixer(xs, nxt, tabs64, tabs128, bsz, seq, dsa1_w_in, dsa1_w_o),
        rwkv,
        lambda xs, gn, nxt: mla(xs, nxt, mla3_w_in, mla3_q_norm, mla3_w_uq, mla3_kv_norm,
                                mla3_w_ukv, mla3_w_o),
    ]
    ffns = [(ffn_norm_0, ffn_w_gu_0, ffn_w_down_0, mix_norm_0),
            (ffn_norm_1, ffn_w_gu_1, ffn_w_down_1, mix_norm_1),
            (ffn_norm_2, ffn_w_gu_2, ffn_w_down_2, mix_norm_2),
            (ffn_norm_3, ffn_w_gu_3, ffn_w_down_3, mix_norm_3)]
    xs = norm_prep(x.reshape(n, d), ffn_norm_0[0])
    for i in range(4):
        f_norm, w_gu, w_down, m_norm = ffns[i]
        after_layer = ffns[i + 1][0][0] if i + 1 < len(ffns) else final_norm
        xs = ffn_half_step(xs, w_gu, w_down, 0, m_norm)
        xs = mixers[i](xs, m_norm, f_norm[1])
        xs = ffn_half_step(xs, w_gu, w_down, 1, after_layer)
    return rmsnorm(xs[0], final_norm, out_dtype=F32).reshape(bsz, seq, d)
```

```python
import functools

import numpy as np
import jax
import jax.numpy as jnp
from jax import lax
from jax.experimental import pallas as pl
from jax.experimental.pallas import tpu as pltpu

F32 = jnp.float32
BF16 = jnp.bfloat16
I32 = jnp.int32

CHUNK = 64
CHUNK_SHIFT = CHUNK.bit_length() - 1
ROPE_THETA = 10000.0
NORM_EPS = 1e-6
FFN_RES = 0.5
MLA_HEADS = 32
MLA_Q_LORA = 1024
MLA_KV_LORA = 512
MLA_NOPE = 128
MLA_ROPE = 64
MLA_V = 128
DSA_HEADS = 32
DSA_KV_HEADS = 8
DSA_HEAD_DIM = 128
IDX_HEADS = 32
IDX_DIM = 128
IDX_ROPE = 64
DSA_TOPK_MAX = 256
RWKV_HEAD = 64
GATE_LORA = 480
GN_EPS = 64e-5

LANES = 128
VMEM_LIMIT = 56 * 1024 * 1024
INT_MIN = -(2 ** 31)
NEG_BIG = -1e30
LOG2E = 1.4426950408889634

_NT = (((1,), (1,)), ((), ()))


def _cparams(*sem):
    return pltpu.CompilerParams(dimension_semantics=sem, vmem_limit_bytes=VMEM_LIMIT)


def _sigmoid(x):
    return 1.0 / (1.0 + jnp.exp(-x))


def _rmsnorm_kernel(x_ref, g_ref, o_ref):
    x = x_ref[...].astype(F32)
    ms = jnp.mean(x * x, axis=-1, keepdims=True)
    o_ref[...] = (x * lax.rsqrt(ms + NORM_EPS) * g_ref[...]).astype(o_ref.dtype)


def rmsnorm(x, g, *, width=None, col_block=0, out_dtype=BF16, rows=256):
    n = x.shape[0]
    w = x.shape[1] if width is None else width
    return pl.pallas_call(
        _rmsnorm_kernel,
        grid=(n // rows,),
        in_specs=[pl.BlockSpec((rows, w), lambda i: (i, col_block)),
                  pl.BlockSpec((1, w), lambda i: (0, 0))],
        out_specs=pl.BlockSpec((rows, w), lambda i: (i, 0)),
        out_shape=jax.ShapeDtypeStruct((n, w), out_dtype),
        compiler_params=_cparams("parallel"),
    )(x, g.reshape(1, w).astype(F32))


def _row_rstd(ss_ref, d_norm):
    return lax.rsqrt(jnp.sum(ss_ref[...], axis=-1, keepdims=True) * (1.0 / d_norm) + NORM_EPS)


def _lane_partial_sumsq(x):
    sq = x * x
    part = sq[:, :LANES]
    for c in range(1, x.shape[1] // LANES):
        part = part + sq[:, c * LANES:(c + 1) * LANES]
    return part


def _mm_kernel(*refs, act, alpha, has_res, d_norm, emit_norm):
    refs = list(refs)
    a_ref, w_ref = refs[:2]
    del refs[:2]
    ss_ref = refs.pop(0) if d_norm else None
    r_ref = refs.pop(0) if has_res else None
    gn_ref = refs.pop(0) if emit_norm else None
    o_ref = refs.pop(0)
    acc = jnp.dot(a_ref[...], w_ref[...].astype(BF16), preferred_element_type=F32)
    if d_norm:
        acc = acc * _row_rstd(ss_ref, d_norm)
    if act == "tanh":
        acc = jnp.tanh(acc)
    elif act == "sigmoid":
        acc = _sigmoid(acc)
    if has_res:
        acc = r_ref[...] + alpha * acc
    o_ref[...] = acc.astype(o_ref.dtype)
    if emit_norm:
        xb_ref, sso_ref = refs
        xb_ref[...] = (acc * gn_ref[...]).astype(BF16)
        part = _lane_partial_sumsq(acc)

        @pl.when(pl.program_id(1) == 0)
        def _():
            sso_ref[...] = part

        @pl.when(pl.program_id(1) != 0)
        def _():
            sso_ref[...] += part


def _mm_tiles(m, k, n, w_itemsize, n_out_tiles):
    bm = min(1024, m)
    a_bytes = 2 * bm * k * 2
    for bn in (1024, 512, 256, 128):
        w_bytes = bn * k * (2 * w_itemsize + (2 if w_itemsize > 2 else 0))
        o_bytes = bm * bn * 4 * (2 * n_out_tiles + 1)
        if n % bn == 0 and a_bytes + w_bytes + o_bytes <= VMEM_LIMIT * 0.85:
            return bm, bn
    return bm, n


def matmul(a, w, *, out_dtype=F32, act=None, res=None, alpha=1.0, bm=None, bn=None, w_index=None,
           row_ss=None, next_gain=None):
    m, k = a.shape
    n = w.shape[-1]
    emit_norm = next_gain is not None
    tbm, tbn = _mm_tiles(m, k, n, w.dtype.itemsize, 1 + (res is not None) + emit_norm)
    bm = tbm if bm is None else bm
    bn = tbn if bn is None else bn
    if w.ndim == 3:
        w_spec = pl.BlockSpec((None, k, bn), lambda i, j: (w_index, 0, j))
    else:
        w_spec = pl.BlockSpec((k, bn), lambda i, j: (0, j))
    row_spec = pl.BlockSpec((bm, LANES), lambda i, j: (i, 0))
    tile_spec = pl.BlockSpec((bm, bn), lambda i, j: (i, j))
    in_specs = [pl.BlockSpec((bm, k), lambda i, j: (i, 0)), w_spec]
    args = [a, w]
    if row_ss is not None:
        in_specs.append(row_spec)
        args.append(row_ss)
    if res is not None:
        in_specs.append(tile_spec)
        args.append(res)
    out_specs, out_shape = tile_spec, jax.ShapeDtypeStruct((m, n), out_dtype)
    if emit_norm:
        in_specs.append(pl.BlockSpec((1, bn), lambda i, j: (0, j)))
        args.append(next_gain.reshape(1, n).astype(F32))
        out_specs = [tile_spec, tile_spec, row_spec]
        out_shape = [out_shape, jax.ShapeDtypeStruct((m, n), BF16),
                     jax.ShapeDtypeStruct((m, LANES), F32)]
    return pl.pallas_call(
        functools.partial(_mm_kernel, act=act, alpha=alpha, has_res=res is not None,
                          d_norm=k if row_ss is not None else 0, emit_norm=emit_norm),
        grid=(m // bm, n // bn),
        in_specs=in_specs,
        out_specs=out_specs,
        out_shape=out_shape,
        compiler_params=_cparams("parallel", "arbitrary"),
    )(*args)


def _swiglu_kernel(a_ref, ss_ref, wg_ref, wu_ref, o_ref, *, d_norm):
    a = a_ref[...]
    rstd = _row_rstd(ss_ref, d_norm)
    g = jnp.dot(a, wg_ref[...].astype(BF16), preferred_element_type=F32) * rstd
    u = jnp.dot(a, wu_ref[...].astype(BF16), preferred_element_type=F32) * rstd
    o_ref[...] = (g * _sigmoid(g) * u).astype(o_ref.dtype)


def swiglu_up(a, row_ss, w_gu, half, *, bm=2048, bf=256):
    m, k = a.shape
    bm = min(bm, m)
    f = w_gu.shape[-1] // 2
    nf = f // bf
    return pl.pallas_call(
        functools.partial(_swiglu_kernel, d_norm=k),
        grid=(m // bm, nf),
        in_specs=[pl.BlockSpec((bm, k), lambda i, j: (i, 0), pipeline_mode=pl.Buffered(1)),
                  pl.BlockSpec((bm, LANES), lambda i, j: (i, 0)),
                  pl.BlockSpec((None, k, bf), lambda i, j: (half, 0, j)),
                  pl.BlockSpec((None, k, bf), lambda i, j: (half, 0, j + nf))],
        out_specs=pl.BlockSpec((bm, bf), lambda i, j: (i, j)),
        out_shape=jax.ShapeDtypeStruct((m, f), BF16),
        compiler_params=_cparams("parallel", "arbitrary"),
    )(a, row_ss, w_gu, w_gu)


def ffn_half_step(xs, w_gu, w_down, half, next_gain):
    x, xb, ss = xs
    act = swiglu_up(xb, ss, w_gu, half)
    return matmul(act, w_down, res=x, alpha=FFN_RES, bm=min(1024, x.shape[0]), bn=256,
                  w_index=half, next_gain=next_gain)


def _norm_prep_kernel(x_ref, g_ref, xb_ref, ss_ref):
    x = x_ref[...]
    xb_ref[...] = (x * g_ref[...]).astype(BF16)
    ss_ref[...] = _lane_partial_sumsq(x)


def norm_prep(x, gain, rows=256):
    n, d = x.shape
    xb, ss = pl.pallas_call(
        _norm_prep_kernel,
        grid=(n // rows,),
        in_specs=[pl.BlockSpec((rows, d), lambda i: (i, 0)),
                  pl.BlockSpec((1, d), lambda i: (0, 0))],
        out_specs=[pl.BlockSpec((rows, d), lambda i: (i, 0)),
                   pl.BlockSpec((rows, LANES), lambda i: (i, 0))],
        out_shape=[jax.ShapeDtypeStruct((n, d), BF16), jax.ShapeDtypeStruct((n, LANES), F32)],
        compiler_params=_cparams("parallel"),
    )(x, gain.reshape(1, d).astype(F32))
    return x, xb, ss


def _rope_lane_consts(layout):
    lane = np.arange(LANES)
    if layout == "half128":
        rot_dim, freq_idx, active = LANES, lane % 64, np.ones(LANES, bool)
        rolls = {64: np.where(lane < 64, -1.0, 1.0)}
    elif layout == "spread64":
        rot_dim, freq_idx, active = 64, lane % 32, (lane % 64) < 32
        rolls = {64: np.where(active, np.where(lane < 64, -1.0, 1.0), 0.0)}
    else:
        rot_dim, freq_idx, active = 64, lane % 32, lane < 64
        rolls = {32: np.where((lane >= 32) & (lane < 64), 1.0, 0.0),
                 96: np.where(lane < 32, -1.0, 0.0)}
    inv = np.where(active, ROPE_THETA ** (-(2.0 * freq_idx) / rot_dim), 0.0)
    signs = np.stack([rolls[k] for k in sorted(rolls)])
    return jnp.asarray(inv, F32).reshape(1, LANES), jnp.asarray(signs, F32), tuple(sorted(rolls))


def _rope_tab_kernel(pos_ref, inv_ref, sign_ref, c_ref, *s_refs):
    ang = pos_ref[...].astype(F32) * inv_ref[...]
    c_ref[...] = jnp.cos(ang)
    sin = jnp.sin(ang)
    for k, s_ref in enumerate(s_refs):
        s_ref[...] = sin * sign_ref[k:k + 1, :]


def rope_tables(pos_col, layout, rows=512):
    n = pos_col.shape[0]
    inv, signs, rolls = _rope_lane_consts(layout)
    tab = jax.ShapeDtypeStruct((n, LANES), F32)
    tabs = pl.pallas_call(
        _rope_tab_kernel,
        grid=(n // rows,),
        in_specs=[pl.BlockSpec((rows, 1), lambda i: (i, 0)),
                  pl.BlockSpec((1, LANES), lambda i: (0, 0)),
                  pl.BlockSpec((len(rolls), LANES), lambda i: (0, 0))],
        out_specs=[pl.BlockSpec((rows, LANES), lambda i: (i, 0))] * (1 + len(rolls)),
        out_shape=[tab] * (1 + len(rolls)),
        compiler_params=_cparams("parallel"),
    )(pos_col, inv, signs)
    return rolls, tabs


def _rope(x, c, s, rolls=(64,)):
    s = s if isinstance(s, (list, tuple)) else [s]
    y = x * c
    for shift, sk in zip(rolls, s):
        y = y + pltpu.roll(x, shift, 1) * sk
    return y


def _mla_attn_kernel(q_ref, kn_ref, v_ref, kr_ref, cq_ref, sq_ref, ck_ref, sk_ref,
                     o_ref, kfull, vfull, krs, *, tq, nq, hb, scale):
    hg = pl.program_id(1)
    qi = pl.program_id(2)

    @pl.when(jnp.logical_and(hg == 0, qi == 0))
    def _():
        krs[...] = _rope(kr_ref[...], ck_ref[...], sk_ref[...]).astype(BF16)

    @pl.when(qi == 0)
    def _():
        for hd in range(hb):
            kfull[hd, :, :LANES] = kn_ref[:, hd * LANES:(hd + 1) * LANES]
            kfull[hd, :, LANES:] = krs[...]
            vfull[hd, :, :LANES] = v_ref[:, hd * LANES:(hd + 1) * LANES]
            vfull[hd, :, LANES:] = jnp.ones((vfull.shape[1], LANES), BF16)

    cq = cq_ref[...]
    sq = sq_ref[...]
    diag_ok = ((lax.broadcasted_iota(I32, (tq, tq), 1) >> CHUNK_SHIFT)
               <= (lax.broadcasted_iota(I32, (tq, tq), 0) >> CHUNK_SHIFT))

    for i in range(nq):
        @pl.when(qi == i)
        def _(i=i):
            kv_len = (i + 1) * tq
            for hd in range(hb):
                base = hd * 2 * LANES
                qr = _rope(q_ref[:, base + LANES:base + 2 * LANES], cq, sq)
                q = jnp.concatenate([q_ref[:, base:base + LANES], qr], axis=1)
                q = (q * (scale * LOG2E)).astype(BF16)
                s = lax.dot_general(q, kfull[hd, :kv_len, :], _NT,
                                    preferred_element_type=F32)
                s_diag = jnp.where(diag_ok, s[:, kv_len - tq:], NEG_BIG)
                s = s_diag if i == 0 else jnp.concatenate([s[:, :kv_len - tq], s_diag], axis=1)
                p = jnp.exp2(s - jnp.max(s, axis=-1, keepdims=True))
                ol = jnp.dot(p.astype(BF16), vfull[hd, :kv_len, :],
                             preferred_element_type=F32)
                o_ref[:, hd * LANES:(hd + 1) * LANES] = (
                    ol[:, :LANES] / ol[:, LANES:LANES + 1]).astype(o_ref.dtype)


def mla_mixer(xs, next_gain, tabs64, bsz, seq, w_in, q_norm, w_uq, kv_norm, w_ukv, w_o):
    x_res, xb, ss = xs
    n = xb.shape[0]
    hq = MLA_HEADS
    lane = np.arange(LANES)
    rope_on = (lane % 64) < (MLA_ROPE // 2)
    rope_src = np.where(rope_on, (lane // 64) * (MLA_ROPE // 2) + lane % (MLA_ROPE // 2), 0)
    base = MLA_Q_LORA + MLA_KV_LORA
    kr_cols = w_in[:, base + rope_src] * jnp.asarray(rope_on, F32)
    w_in_p = jnp.concatenate([w_in[:, :base], kr_cols], axis=1).astype(BF16)
    hd = MLA_NOPE + MLA_ROPE
    idx = np.concatenate([np.concatenate([hh * hd + np.arange(MLA_NOPE),
                                          hh * hd + MLA_NOPE + rope_src]) for hh in range(hq)])
    msk = np.tile(np.concatenate([np.ones(MLA_NOPE, bool), rope_on]), hq)
    w_uq_p = (w_uq[:, idx] * jnp.asarray(msk, F32)).astype(BF16)
    w_ukv_p = w_ukv.reshape(MLA_KV_LORA, hq, 2, MLA_NOPE).transpose(0, 2, 1, 3)
    w_ukv_p = w_ukv_p.reshape(MLA_KV_LORA, 2 * hq * MLA_NOPE).astype(BF16)

    lat = matmul(xb, w_in_p, bm=512, bn=w_in_p.shape[1], row_ss=ss)
    cq = rmsnorm(lat, q_norm, width=MLA_Q_LORA, col_block=0)
    ckv = rmsnorm(lat, kv_norm, width=MLA_KV_LORA, col_block=MLA_Q_LORA // MLA_KV_LORA)
    q = matmul(cq, w_uq_p)
    kv = matmul(ckv, w_ukv_p, out_dtype=BF16)

    tq, hb = 256, 8
    nq = seq // tq
    ng = hq // hb
    c64, s64 = tabs64[1]
    kr_blk = base // LANES
    o = pl.pallas_call(
        functools.partial(_mla_attn_kernel, tq=tq, nq=nq, hb=hb, scale=float(hd) ** -0.5),
        grid=(bsz, ng, nq),
        in_specs=[pl.BlockSpec((tq, 2 * LANES * hb), lambda b, hh, i: (b * nq + i, hh)),
                  pl.BlockSpec((seq, LANES * hb), lambda b, hh, i: (b, hh)),
                  pl.BlockSpec((seq, LANES * hb), lambda b, hh, i: (b, ng + hh)),
                  pl.BlockSpec((seq, LANES), lambda b, hh, i: (b, kr_blk)),
                  pl.BlockSpec((tq, LANES), lambda b, hh, i: (b * nq + i, 0)),
                  pl.BlockSpec((tq, LANES), lambda b, hh, i: (b * nq + i, 0)),
                  pl.BlockSpec((seq, LANES), lambda b, hh, i: (b, 0)),
                  pl.BlockSpec((seq, LANES), lambda b, hh, i: (b, 0))],
        out_specs=pl.BlockSpec((tq, LANES * hb), lambda b, hh, i: (b * nq + i, hh)),
        out_shape=jax.ShapeDtypeStruct((n, hq * MLA_V), BF16),
        scratch_shapes=[pltpu.VMEM((hb, seq, 2 * LANES), BF16),
                        pltpu.VMEM((hb, seq, 2 * LANES), BF16),
                        pltpu.VMEM((seq, LANES), BF16)],
        compiler_params=_cparams("arbitrary", "arbitrary", "arbitrary"),
    )(q, kv, kv, lat, c64, s64, c64, s64)
    return matmul(o, w_o.astype(BF16), res=x_res, alpha=1.0, next_gain=next_gain)


ONES_ROWS = 16


def _proj_heads_kernel(a_ref, w_ref, ss_ref, *rest, d_norm, rolls, out_scale, transpose):
    tabs, o_ref = [t[...] for t in rest[:-1]], rest[-1]
    acc = lax.dot_general(a_ref[...], w_ref[...].astype(BF16), _NT, preferred_element_type=F32)
    acc = acc * _row_rstd(ss_ref, d_norm)
    for g in range(acc.shape[1] // LANES):
        blk = acc[:, g * LANES:(g + 1) * LANES]
        if transpose:
            o_ref[0, g, :LANES, :] = blk.T.astype(o_ref.dtype)
            o_ref[0, g, LANES:, :] = jnp.ones((ONES_ROWS, blk.shape[0]), o_ref.dtype)
        else:
            if tabs:
                blk = _rope(blk, tabs[0], tabs[1:], rolls)
            o_ref[0, g] = (blk * out_scale).astype(o_ref.dtype)


def project_heads(xb, row_ss, w_t, bsz, seq, *, col0, heads, tabs=None, out_scale=1.0,
                  transpose=False, out_dtype=BF16):
    m, k = xb.shape
    n = heads * LANES
    bm = min(1024, seq)
    bn = min(512, n)
    nt = seq // bm
    cb = col0 // bn
    rolls, tables = tabs if tabs is not None else ((), [])
    row_blk = lambda width: pl.BlockSpec((bm, width), lambda i, j: (i, 0))
    if transpose:
        out_spec = pl.BlockSpec((1, bn // LANES, LANES + ONES_ROWS, bm),
                                lambda i, j: (i // nt, j, 0, i % nt))
        out_shape = jax.ShapeDtypeStruct((bsz, heads, LANES + ONES_ROWS, seq), out_dtype)
    else:
        out_spec = pl.BlockSpec((1, bn // LANES, bm, LANES), lambda i, j: (i // nt, j, i % nt, 0))
        out_shape = jax.ShapeDtypeStruct((bsz, heads, seq, LANES), out_dtype)
    return pl.pallas_call(
        functools.partial(_proj_heads_kernel, d_norm=k, rolls=rolls, out_scale=out_scale,
                          transpose=transpose),
        grid=(m // bm, n // bn),
        in_specs=[row_blk(k), pl.BlockSpec((bn, k), lambda i, j: (cb + j, 0)), row_blk(LANES)]
        + [row_blk(LANES)] * len(tables),
        out_specs=out_spec,
        out_shape=out_shape,
        compiler_params=_cparams("parallel", "arbitrary"),
    )(xb, w_t, row_ss, *tables)


def _dsa_kernel(qi_ref, ki_ref, wi_ref, q_ref, k_ref, vt_ref, o_ref, key_scr, ot_scr,
                *, tq, tk, nq, topk, rep, idx_scale):
    qt = pl.program_id(1)
    q_chunk = (qt * tq + lax.broadcasted_iota(I32, (tk, tq), 1)) >> CHUNK_SHIFT
    k_iota = lax.broadcasted_iota(I32, (tk, tq), 0)
    groups = DSA_KV_HEADS

    for nb in range(nq * tq // tk):
        @pl.when((qt * tq) // tk == nb)
        def _(nb=nb):
            nblk = nb + 1
            kv_len = nblk * tk

            def idx_body(j, _):
                off = pl.multiple_of(j * tk, tk)
                kib = ki_ref[0, 0, pl.ds(off, tk), :]
                acc = jnp.zeros((tk, tq), F32)
                for hh in range(IDX_HEADS):
                    d = lax.dot_general(kib, qi_ref[0, hh], _NT, preferred_element_type=F32)
                    acc = acc + (wi_ref[0, hh:hh + 1, :] * idx_scale) * jnp.maximum(d, 0.0)
                bits = pltpu.bitcast(acc, I32)
                key = bits ^ ((bits >> 31) & 0x7FFFFFFF)
                valid = ((off + k_iota) >> CHUNK_SHIFT) <= q_chunk
                key_scr[pl.ds(off, tk), :] = jnp.where(valid, key, INT_MIN)
                return 0

            lax.fori_loop(0, nblk, idx_body, 0)

            def count_ge(cand):
                hit = jnp.where(key_scr[:kv_len, :] >= cand, 1, 0).astype(I32)
                c8 = jnp.sum(hit.reshape(kv_len // 8, 8, tq), axis=0)
                return jnp.sum(c8, axis=0, keepdims=True)

            thr = jnp.full((1, tq), INT_MIN, I32)
            thr = jnp.where(count_ge(jnp.zeros((1, tq), I32)) >= topk, 0, thr)

            def bit_body(i, t):
                cand = t + jnp.left_shift(jnp.int32(1), 30 - i)
                return jnp.where(count_ge(cand) >= topk, cand, t)

            thr = lax.fori_loop(0, 31, bit_body, thr)
            thr = jnp.maximum(thr, INT_MIN + 1)

            def group_body(g, _):
                qg = q_ref[0, pl.ds(g * rep, rep)].reshape(rep * tq, DSA_HEAD_DIM)
                s = lax.dot_general(k_ref[0, g, :kv_len, :], qg, _NT,
                                    preferred_element_type=F32)
                sel = key_scr[:kv_len, :] >= thr
                s = jnp.where(jnp.concatenate([sel] * rep, axis=1), s, NEG_BIG)
                p = jnp.exp2(s - jnp.max(s, axis=0, keepdims=True))
                acc = jnp.dot(vt_ref[0, g, :, :kv_len], p.astype(BF16),
                              preferred_element_type=F32)
                ot_scr[g] = acc[:DSA_HEAD_DIM] / acc[DSA_HEAD_DIM:DSA_HEAD_DIM + 1]
                return 0

            lax.fori_loop(0, groups, group_body, 0)

    for g in range(groups):
        for r in range(rep):
            hh = g * rep + r
            o_ref[:, hh * DSA_HEAD_DIM:(hh + 1) * DSA_HEAD_DIM] = (
                ot_scr[g, :, r * tq:(r + 1) * tq].T.astype(o_ref.dtype))


def dsa_mixer(xs, next_gain, tabs64, tabs128, bsz, seq, w_in, w_o):
    x_res, xb, ss = xs
    n = xb.shape[0]
    hq, g, hd = DSA_HEADS, DSA_KV_HEADS, DSA_HEAD_DIM
    rep = hq // g
    topk = min(DSA_TOPK_MAX, seq // 4)
    n_q, n_kv, n_qi = hq * hd, g * hd, IDX_HEADS * IDX_DIM
    n_qkv = n_q + 2 * n_kv
    main = n_qkv + n_qi

    proj = functools.partial(project_heads, xb, ss, w_in.T, bsz, seq)
    q_hm = proj(col0=0, heads=hq, tabs=tabs128, out_scale=float(hd) ** -0.5 * LOG2E)
    k_hm = proj(col0=n_q, heads=g, tabs=tabs128)
    vt = proj(col0=n_q + n_kv, heads=g, transpose=True)
    qi_hm = proj(col0=n_qkv, heads=IDX_HEADS, tabs=tabs64)
    ki_hm = proj(col0=main, heads=1, tabs=tabs64)
    wi = proj(col0=main + IDX_DIM, heads=1, out_dtype=F32)
    wi_t = wi[:, 0, :, :IDX_HEADS].transpose(0, 2, 1)
    tq, tk = 128, 256

    nq = seq // tq
    o = pl.pallas_call(
        functools.partial(_dsa_kernel, tq=tq, tk=tk, nq=nq, topk=topk, rep=rep,
                          idx_scale=float(IDX_HEADS) ** -0.5 * float(IDX_DIM) ** -0.5),
        grid=(bsz, nq),
        in_specs=[pl.BlockSpec((1, IDX_HEADS, tq, LANES), lambda b, i: (b, 0, i, 0)),
                  pl.BlockSpec((1, 1, seq, LANES), lambda b, i: (b, 0, 0, 0)),
                  pl.BlockSpec((1, IDX_HEADS, tq), lambda b, i: (b, 0, i)),
                  pl.BlockSpec((1, hq, tq, LANES), lambda b, i: (b, 0, i, 0)),
                  pl.BlockSpec((1, g, seq, LANES), lambda b, i: (b, 0, 0, 0)),
                  pl.BlockSpec((1, g, LANES + ONES_ROWS, seq), lambda b, i: (b, 0, 0, 0))],
        out_specs=pl.BlockSpec((tq, hq * hd), lambda b, i: (b * nq + i, 0)),
        out_shape=jax.ShapeDtypeStruct((n, hq * hd), BF16),
        scratch_shapes=[pltpu.VMEM((seq, tq), I32),
                        pltpu.VMEM((g, hd, rep * tq), F32)],
        compiler_params=_cparams("parallel", "arbitrary"),
    )(qi_hm, ki_hm, wi_t, q_hm, k_hm, vt)
    return matmul(o, w_o.astype(BF16), res=x_res, alpha=1.0, next_gain=next_gain)


SUBLANES = 8


def _rms(x, g):
    ms = jnp.mean(x * x, axis=-1, keepdims=True)
    return x * lax.rsqrt(ms + NORM_EPS) * g


def _shift_mix_kernel(x_ref, xp_ref, g_ref, mu_ref, *o_refs):
    g = g_ref[...]
    h = _rms(x_ref[...], g)
    first_tile = pl.program_id(1) == 0
    last_prev = _rms(xp_ref[...], g)[SUBLANES - 1:SUBLANES, :]
    last_prev = jnp.where(first_tile, 0.0, last_prev)
    row = lax.broadcasted_iota(I32, h.shape, 0)
    h_prev = jnp.where(row == 0, last_prev, pltpu.roll(h, 1, 0))
    xx = h_prev - h
    for i, o_ref in enumerate(o_refs):
        o_ref[...] = (h + xx * mu_ref[i:i + 1, :]).astype(o_ref.dtype)


def token_shift_mix(x, gain, mu, bsz, seq, rows=256):
    n, d = x.shape
    k = mu.shape[0]
    nt = seq // rows
    per = rows // SUBLANES
    blk = pl.BlockSpec((rows, d), lambda b, t: (b * nt + t, 0))
    prev = pl.BlockSpec((SUBLANES, d), lambda b, t: (jnp.maximum((b * nt + t) * per - 1, 0), 0))
    return pl.pallas_call(
        _shift_mix_kernel,
        grid=(bsz, nt),
        in_specs=[blk, prev, pl.BlockSpec((1, d), lambda b, t: (0, 0)),
                  pl.BlockSpec((k, d), lambda b, t: (0, 0))],
        out_specs=[blk] * k,
        out_shape=[jax.ShapeDtypeStruct((n, d), BF16)] * k,
        compiler_params=_cparams("parallel", "arbitrary"),
    )(x, x, gain.reshape(1, d).astype(F32), mu)


def _split3(x):
    hi = x.astype(BF16)
    r1 = x - hi.astype(F32)
    mid = r1.astype(BF16)
    lo = (r1 - mid.astype(F32)).astype(BF16)
    return hi, mid, lo


def _dot3_right(x, w_bf16):
    hi, mid, lo = _split3(x)
    d = functools.partial(jnp.dot, preferred_element_type=F32)
    return d(hi, w_bf16) + d(mid, w_bf16) + d(lo, w_bf16)


def _dot3_left(w_bf16, x):
    hi, mid, lo = _split3(x)
    d = functools.partial(jnp.dot, preferred_element_type=F32)
    return d(w_bf16, hi) + d(w_bf16, mid) + d(w_bf16, lo)


def _wkv_kernel(r_ref, k_ref, v_ref, wl_ref, al_ref, g_ref,
                w0_ref, a0_ref, kk_ref, ka_ref, rk_ref, lnw_ref, lnb_ref,
                o_ref, state, *, L, pairs):
    c_idx = pl.program_id(2)

    @pl.when(c_idx == 0)
    def _():
        state[...] = jnp.zeros_like(state)

    hn = RWKV_HEAD
    hshift = hn.bit_length() - 1
    li = lax.broadcasted_iota(I32, (LANES, LANES), 0)
    lj = lax.broadcasted_iota(I32, (LANES, LANES), 1)
    same_head = (li >> hshift) == (lj >> hshift)
    seg_avg = jnp.where(same_head, 1.0 / hn, 0.0).astype(BF16)

    def seg_mean(x):
        lhs = jnp.concatenate([part[:, p * LANES:(p + 1) * LANES]
                               for part in _split3(x) for p in range(pairs)], axis=0)
        res = jnp.dot(lhs, seg_avg, preferred_element_type=F32)
        rows = pairs * L
        tot = res[:rows] + res[rows:2 * rows] + res[2 * rows:]
        return jnp.concatenate([tot[p * L:(p + 1) * L] for p in range(pairs)], axis=1)
    ti = lax.broadcasted_iota(I32, (L, L), 0)
    tj = lax.broadcasted_iota(I32, (L, L), 1)
    tril_incl = jnp.where(ti >= tj, 1.0, 0.0).astype(BF16)
    hr = lax.broadcasted_iota(I32, (L, LANES), 0)
    hc = lax.broadcasted_iota(I32, (L, LANES), 1)
    left = hc < hn
    strict = hr > (hc & (hn - 1))
    incl = hr >= (hc & (hn - 1))

    r_all = r_ref[...]
    k_all = k_ref[...]
    v_all = v_ref[...]
    z = w0_ref[...] + wl_ref[...]
    u = -z
    softplus = jnp.maximum(u, 0.0) + jnp.log(1.0 + jnp.exp(-jnp.abs(u)))
    w_log = -softplus - 0.5
    lw = -jnp.exp(w_log)
    a_sig = _sigmoid(a0_ref[...] + al_ref[...])
    k2_all = k_all * (1.0 + (a_sig - 1.0) * ka_ref[...])
    kk_all = k_all * kk_ref[...]
    csum = _dot3_left(tril_incl, lw)
    c_last = csum[L - 1:L, :]
    e_c = jnp.exp(csum)
    e_cprev = jnp.exp(csum - lw)
    e_neg = jnp.exp(-csum)
    e_rem = jnp.exp(c_last - csum)
    gam_all = jnp.exp(c_last)

    bdot = functools.partial(jnp.dot, preferred_element_type=F32)
    cat = jnp.concatenate
    prs = range(pairs)
    sls = [slice(p * LANES, (p + 1) * LANES) for p in prs]
    ksq = seg_mean(kk_all * kk_all)
    kk_n = kk_all / jnp.maximum(jnp.sqrt(hn * ksq), 1e-12)
    bvec_all = kk_n * a_sig
    at_all = -kk_n * e_cprev
    rt_all = r_all * e_c
    bt = [(bvec_all[:, sl] * e_neg[:, sl]).astype(BF16) for sl in sls]
    kt = [(k2_all[:, sl] * e_neg[:, sl]).astype(BF16) for sl in sls]
    a0 = [jnp.where(left, at_all[:, sl], 0.0).astype(BF16) for sl in sls]
    a1 = [jnp.where(left, 0.0, at_all[:, sl]).astype(BF16) for sl in sls]
    r0 = [jnp.where(left, rt_all[:, sl], 0.0).astype(BF16) for sl in sls]
    r1 = [jnp.where(left, 0.0, rt_all[:, sl]).astype(BF16) for sl in sls]
    vsw = [cat([jnp.where(left, 0.0, v_all[:, sl]), jnp.where(left, v_all[:, sl], 0.0)],
               axis=0).astype(BF16) for sl in sls]

    lhs = [cat([a0[p], r0[p], a1[p], r1[p]], axis=0) for p in prs]
    o01 = [lax.dot_general(lhs[p], cat([bt[p], kt[p]], axis=0), _NT,
                           preferred_element_type=F32) for p in prs]
    o0 = [o01[p][:2 * L] for p in prs]
    o1 = [pltpu.roll(o01[p][2 * L:], hn, 1) for p in prs]
    st = [state[p] for p in prs]
    ars = [lax.dot_general(lhs[p], st[p].astype(BF16), _NT,
                           preferred_element_type=F32) for p in prs]
    u0 = [jnp.where(strict, o0[p][:L], 0.0) for p in prs]
    u1 = [jnp.where(strict, o1[p][:L], 0.0) for p in prs]
    m_ab = [cat([jnp.where(left, u0[p], 0.0), jnp.where(left, 0.0, u1[p])], axis=0) for p in prs]
    m_ak = [cat([jnp.where(left, 0.0, u0[p]), jnp.where(left, u1[p], 0.0)], axis=0) for p in prs]
    xt = [(cat([ars[p][:L], ars[p][2 * L:3 * L]], axis=0)
           + bdot(m_ak[p].astype(BF16), vsw[p])).T for p in prs]
    mt = [m_ab[p].T for p in prs]
    steps = (L - 1).bit_length()
    for it in range(steps):
        mb = [mt[p].astype(BF16) for p in prs]
        if it + 1 < steps:
            prod = [bdot(cat([xt[p].astype(BF16), mb[p]], axis=0), mb[p]) for p in prs]
            xt = [xt[p] + prod[p][:LANES] for p in prs]
            mt = [prod[p][LANES:] for p in prs]
        else:
            xt = [xt[p] + bdot(xt[p].astype(BF16), mb[p]) for p in prs]
    x = [xt[p].T for p in prs]
    y0 = [jnp.where(incl, o0[p][L:], 0.0) for p in prs]
    y1 = [jnp.where(incl, o1[p][L:], 0.0) for p in prs]
    m_rb = [cat([jnp.where(left, y0[p], 0.0), jnp.where(left, 0.0, y1[p])], axis=0) for p in prs]
    m_rk = [cat([jnp.where(left, 0.0, y0[p]), jnp.where(left, y1[p], 0.0)], axis=0) for p in prs]
    ys = [cat([ars[p][L:2 * L], ars[p][3 * L:]], axis=0)
          + bdot(cat([m_rb[p], m_rk[p]], axis=1).astype(BF16),
                 cat([x[p].astype(BF16), vsw[p]], axis=0)) for p in prs]
    y_all = cat([ys[p][:L] + ys[p][L:] for p in prs], axis=1)
    sv_t = [cat([x[p][:L] + x[p][L:], v_all[:, sl]], axis=0).T.astype(BF16)
            for p, sl in enumerate(sls)]
    bh_all = (bvec_all * e_rem).astype(BF16)
    kh_all = (k2_all * e_rem).astype(BF16)
    bkh = [cat([bh_all[:, sl], kh_all[:, sl]], axis=0) for sl in sls]
    upd = [bdot(sv_t[p], bkh[p]) for p in prs]
    for p, sl in enumerate(sls):
        state[p] = st[p] * gam_all[:, sl] + jnp.where(same_head, upd[p], 0.0)

    yc = y_all - seg_mean(y_all)
    var = seg_mean(yc * yc)
    rk_sum = hn * seg_mean(r_all * k2_all * rk_ref[...])
    yn = yc * lax.rsqrt(var + GN_EPS) * lnw_ref[...] + lnb_ref[...]
    o_ref[...] = ((yn + rk_sum * v_all) * g_ref[...]).astype(o_ref.dtype)


def rwkv7_mixer(xs, gain, next_gain, bsz, seq, mu, w_r, w_k, w_v, w_o, w0, w1, w2, a0, a1, a2,
                g1, g2, k_k, k_a, r_k, lnx_w, lnx_b):
    x_res = xs[0]
    n, d = x_res.shape
    xr, xw, xk, xv, xa, xg = token_shift_mix(x_res, gain, mu, bsz, seq)
    r = matmul(xr, w_r)
    k = matmul(xk, w_k)
    v = matmul(xv, w_v)
    tw = matmul(xw, w1.astype(BF16), out_dtype=BF16, act="tanh", bn=w1.shape[1])
    wl = matmul(tw, w2.astype(BF16))
    ta = matmul(xa, a1.astype(BF16), out_dtype=BF16, bn=a1.shape[1])
    al = matmul(ta, a2.astype(BF16))
    gpad = (-GATE_LORA) % LANES
    g1p = jnp.pad(g1, ((0, 0), (0, gpad))).astype(BF16)
    g2p = jnp.pad(g2, ((0, gpad), (0, 0))).astype(BF16)
    tg = matmul(xg, g1p, out_dtype=BF16, act="sigmoid", bn=g1p.shape[1])
    gate = matmul(tg, g2p)

    L, pairs = CHUNK, 32
    nc = seq // L
    wide = LANES * pairs
    row = lambda a: a.reshape(1, d).astype(F32)
    dat = pl.BlockSpec((L, wide), lambda b, j, c: (b * nc + c, j))
    par = pl.BlockSpec((1, wide), lambda b, j, c: (0, j))
    yg = pl.pallas_call(
        functools.partial(_wkv_kernel, L=L, pairs=pairs),
        grid=(bsz, d // wide, nc),
        in_specs=[dat] * 6 + [par] * 7,
        out_specs=dat,
        out_shape=jax.ShapeDtypeStruct((n, d), BF16),
        scratch_shapes=[pltpu.VMEM((pairs, LANES, LANES), F32)],
        compiler_params=_cparams("parallel", "parallel", "arbitrary"),
    )(r, k, v, wl, al, gate, row(w0), row(a0), row(k_k), row(k_a), row(r_k), row(lnx_w), row(lnx_b))
    return matmul(yg, w_o.astype(BF16), res=x_res, alpha=1.0, next_gain=next_gain)


def kernel(x, positions, ffn_norm_0, ffn_w_gu_0, ffn_w_down_0, mix_norm_0, mla0_w_in, mla0_q_norm, mla0_w_uq, mla0_kv_norm, mla0_w_ukv, mla0_w_o, ffn_norm_1, ffn_w_gu_1, ffn_w_down_1, mix_norm_1, dsa1_w_in, dsa1_w_o, ffn_norm_2, ffn_w_gu_2, ffn_w_down_2, mix_norm_2, rwkv2_mu, rwkv2_w_r, rwkv2_w_k, rwkv2_w_v, rwkv2_w_o, rwkv2_w0, rwkv2_w1, rwkv2_w2, rwkv2_a0, rwkv2_a1, rwkv2_a2, rwkv2_g1, rwkv2_g2, rwkv2_k_k, rwkv2_k_a, rwkv2_r_k, rwkv2_lnx_w, rwkv2_lnx_b, ffn_norm_3, ffn_w_gu_3, ffn_w_down_3, mix_norm_3, mla3_w_in, mla3_q_norm, mla3_w_uq, mla3_kv_norm, mla3_w_ukv, mla3_w_o, final_norm):
    bsz, seq, d = x.shape
    n = bsz * seq
    pos_col = positions.reshape(n, 1)
    tabs_mla = rope_tables(pos_col, "spread64")
    tabs128 = rope_tables(pos_col, "half128")
    tabs64 = rope_tables(pos_col, "lead64")

    def mla(xs, nxt, w_in, q_norm, w_uq, kv_norm, w_ukv, w_o):
        return mla_mixer(xs, nxt, tabs_mla, bsz, seq, w_in, q_norm, w_uq, kv_norm, w_ukv, w_o)

    def rwkv(xs, gain, nxt):
        return rwkv7_mixer(xs, gain, nxt, bsz, seq, rwkv2_mu, rwkv2_w_r, rwkv2_w_k, rwkv2_w_v,
                           rwkv2_w_o, rwkv2_w0, rwkv2_w1, rwkv2_w2, rwkv2_a0, rwkv2_a1, rwkv2_a2,
                           rwkv2_g1, rwkv2_g2, rwkv2_k_k, rwkv2_k_a,
                           rwkv2_r_k.reshape(-1), rwkv2_lnx_w, rwkv2_lnx_b)

    mixers = [
        lambda xs, gn, nxt: mla(xs, nxt, mla0_w_in, mla0_q_norm, mla0_w_uq, mla0_kv_norm,
                                mla0_w_ukv, mla0_w_o),
        lambda xs, gn, nxt: dsa_mixer(xs, nxt, tabs64, tabs128, bsz, seq, dsa1_w_in, dsa1_w_o),
        rwkv,
        lambda xs, gn, nxt: mla(xs, nxt, mla3_w_in, mla3_q_norm, mla3_w_uq, mla3_kv_norm,
                                mla3_w_ukv, mla3_w_o),
    ]
    ffns = [(ffn_norm_0, ffn_w_gu_0, ffn_w_down_0, mix_norm_0),
            (ffn_norm_1, ffn_w_gu_1, ffn_w_down_1, mix_norm_1),
            (ffn_norm_2, ffn_w_gu_2, ffn_w_down_2, mix_norm_2),
            (ffn_norm_3, ffn_w_gu_3, ffn_w_down_3, mix_norm_3)]
    xs = norm_prep(x.reshape(n, d), ffn_norm_0[0])
    for i in range(4):
        f_norm, w_gu, w_down, m_norm = ffns[i]
        after_layer = ffns[i + 1][0][0] if i + 1 < len(ffns) else final_norm
        xs = ffn_half_step(xs, w_gu, w_down, 0, m_norm)
        xs = mixers[i](xs, m_norm, f_norm[1])
        xs = ffn_half_step(xs, w_gu, w_down, 1, after_layer)
    return rmsnorm(xs[0], final_norm, out_dtype=F32).reshape(bsz, seq, d)
```

```python
import functools

import numpy as np
import jax
import jax.numpy as jnp
from jax import lax
from jax.experimental import pallas as pl
from jax.experimental.pallas import tpu as pltpu

F32 = jnp.float32
BF16 = jnp.bfloat16
I32 = jnp.int32

CHUNK = 64
CHUNK_SHIFT = CHUNK.bit_length() - 1
ROPE_THETA = 10000.0
NORM_EPS = 1e-6
FFN_RES = 0.5
MLA_HEADS = 32
MLA_Q_LORA = 1024
MLA_KV_LORA = 512
MLA_NOPE = 128
MLA_ROPE = 64
MLA_V = 128
DSA_HEADS = 32
DSA_KV_HEADS = 8
DSA_HEAD_DIM = 128
IDX_HEADS = 32
IDX_DIM = 128
IDX_ROPE = 64
DSA_TOPK_MAX = 256
RWKV_HEAD = 64
GATE_LORA = 480
GN_EPS = 64e-5

LANES = 128
VMEM_LIMIT = 56 * 1024 * 1024
INT_MIN = -(2 ** 31)
NEG_BIG = -1e30
LOG2E = 1.4426950408889634

_NT = (((1,), (1,)), ((), ()))


def _cparams(*sem):
    return pltpu.CompilerParams(dimension_semantics=sem, vmem_limit_bytes=VMEM_LIMIT)


def _sigmoid(x):
    return 1.0 / (1.0 + jnp.exp(-x))


def _rmsnorm_kernel(x_ref, g_ref, o_ref):
    x = x_ref[...].astype(F32)
    ms = jnp.mean(x * x, axis=-1, keepdims=True)
    o_ref[...] = (x * lax.rsqrt(ms + NORM_EPS) * g_ref[...]).astype(o_ref.dtype)


def rmsnorm(x, g, *, width=None, col_block=0, out_dtype=BF16, rows=256):
    n = x.shape[0]
    w = x.shape[1] if width is None else width
    return pl.pallas_call(
        _rmsnorm_kernel,
        grid=(n // rows,),
        in_specs=[pl.BlockSpec((rows, w), lambda i: (i, col_block)),
                  pl.BlockSpec((1, w), lambda i: (0, 0))],
        out_specs=pl.BlockSpec((rows, w), lambda i: (i, 0)),
        out_shape=jax.ShapeDtypeStruct((n, w), out_dtype),
        compiler_params=_cparams("parallel"),
    )(x, g.reshape(1, w).astype(F32))


def _row_rstd(ss_ref, d_norm):
    return lax.rsqrt(jnp.sum(ss_ref[...], axis=-1, keepdims=True) * (1.0 / d_norm) + NORM_EPS)


def _lane_partial_sumsq(x):
    sq = x * x
    part = sq[:, :LANES]
    for c in range(1, x.shape[1] // LANES):
        part = part + sq[:, c * LANES:(c + 1) * LANES]
    return part


def _mm_kernel(*refs, act, alpha, has_res, d_norm, emit_norm):
    refs = list(refs)
    a_ref, w_ref = refs[:2]
    del refs[:2]
    ss_ref = refs.pop(0) if d_norm else None
    r_ref = refs.pop(0) if has_res else None
    gn_ref = refs.pop(0) if emit_norm else None
    o_ref = refs.pop(0)
    acc = jnp.dot(a_ref[...], w_ref[...].astype(BF16), preferred_element_type=F32)
    if d_norm:
        acc = acc * _row_rstd(ss_ref, d_norm)
    if act == "tanh":
        acc = jnp.tanh(acc)
    elif act == "sigmoid":
        acc = _sigmoid(acc)
    if has_res:
        acc = r_ref[...] + alpha * acc
    o_ref[...] = acc.astype(o_ref.dtype)
    if emit_norm:
        xb_ref, sso_ref = refs
        xb_ref[...] = (acc * gn_ref[...]).astype(BF16)
        part = _lane_partial_sumsq(acc)

        @pl.when(pl.program_id(1) == 0)
        def _():
            sso_ref[...] = part

        @pl.when(pl.program_id(1) != 0)
        def _():
            sso_ref[...] += part


def _mm_tiles(m, k, n, w_itemsize, n_out_tiles):
    bm = min(1024, m)
    a_bytes = 2 * bm * k * 2
    for bn in (1024, 512, 256, 128):
        w_bytes = bn * k * (2 * w_itemsize + (2 if w_itemsize > 2 else 0))
        o_bytes = bm * bn * 4 * (2 * n_out_tiles + 1)
        if n % bn == 0 and a_bytes + w_bytes + o_bytes <= VMEM_LIMIT * 0.85:
            return bm, bn
    return bm, n


def matmul(a, w, *, out_dtype=F32, act=None, res=None, alpha=1.0, bm=None, bn=None, w_index=None,
           row_ss=None, next_gain=None):
    m, k = a.shape
    n = w.shape[-1]
    emit_norm = next_gain is not None
    tbm, tbn = _mm_tiles(m, k, n, w.dtype.itemsize, 1 + (res is not None) + emit_norm)
    bm = tbm if bm is None else bm
    bn = tbn if bn is None else bn
    if w.ndim == 3:
        w_spec = pl.BlockSpec((None, k, bn), lambda i, j: (w_index, 0, j))
    else:
        w_spec = pl.BlockSpec((k, bn), lambda i, j: (0, j))
    row_spec = pl.BlockSpec((bm, LANES), lambda i, j: (i, 0))
    tile_spec = pl.BlockSpec((bm, bn), lambda i, j: (i, j))
    in_specs = [pl.BlockSpec((bm, k), lambda i, j: (i, 0)), w_spec]
    args = [a, w]
    if row_ss is not None:
        in_specs.append(row_spec)
        args.append(row_ss)
    if res is not None:
        in_specs.append(tile_spec)
        args.append(res)
    out_specs, out_shape = tile_spec, jax.ShapeDtypeStruct((m, n), out_dtype)
    if emit_norm:
        in_specs.append(pl.BlockSpec((1, bn), lambda i, j: (0, j)))
        args.append(next_gain.reshape(1, n).astype(F32))
        out_specs = [tile_spec, tile_spec, row_spec]
        out_shape = [out_shape, jax.ShapeDtypeStruct((m, n), BF16),
                     jax.ShapeDtypeStruct((m, LANES), F32)]
    return pl.pallas_call(
        functools.partial(_mm_kernel, act=act, alpha=alpha, has_res=res is not None,
                          d_norm=k if row_ss is not None else 0, emit_norm=emit_norm),
        grid=(m // bm, n // bn),
        in_specs=in_specs,
        out_specs=out_specs,
        out_shape=out_shape,
        compiler_params=_cparams("parallel", "arbitrary"),
    )(*args)


def _swiglu_kernel(a_ref, ss_ref, wg_ref, wu_ref, o_ref, *, d_norm):
    a = a_ref[...]
    rstd = _row_rstd(ss_ref, d_norm)
    g = jnp.dot(a, wg_ref[...].astype(BF16), preferred_element_type=F32) * rstd
    u = jnp.dot(a, wu_ref[...].astype(BF16), preferred_element_type=F32) * rstd
    o_ref[...] = (g * _sigmoid(g) * u).astype(o_ref.dtype)


def swiglu_up(a, row_ss, w_gu, half, *, bm=2048, bf=256):
    m, k = a.shape
    bm = min(bm, m)
    f = w_gu.shape[-1] // 2
    nf = f // bf
    return pl.pallas_call(
        functools.partial(_swiglu_kernel, d_norm=k),
        grid=(m // bm, nf),
        in_specs=[pl.BlockSpec((bm, k), lambda i, j: (i, 0), pipeline_mode=pl.Buffered(1)),
                  pl.BlockSpec((bm, LANES), lambda i, j: (i, 0)),
                  pl.BlockSpec((None, k, bf), lambda i, j: (half, 0, j)),
                  pl.BlockSpec((None, k, bf), lambda i, j: (half, 0, j + nf))],
        out_specs=pl.BlockSpec((bm, bf), lambda i, j: (i, j)),
        out_shape=jax.ShapeDtypeStruct((m, f), BF16),
        compiler_params=_cparams("parallel", "arbitrary"),
    )(a, row_ss, w_gu, w_gu)


def ffn_half_step(xs, w_gu, w_down, half, next_gain):
    x, xb, ss = xs
    act = swiglu_up(xb, ss, w_gu, half)
    return matmul(act, w_down, res=x, alpha=FFN_RES, bm=min(1024, x.shape[0]), bn=256,
                  w_index=half, next_gain=next_gain)


def _norm_prep_kernel(x_ref, g_ref, xb_ref, ss_ref):
    x = x_ref[...]
    xb_ref[...] = (x * g_ref[...]).astype(BF16)
    ss_ref[...] = _lane_partial_sumsq(x)


def norm_prep(x, gain, rows=256):
    n, d = x.shape
    xb, ss = pl.pallas_call(
        _norm_prep_kernel,
        grid=(n // rows,),
        in_specs=[pl.BlockSpec((rows, d), lambda i: (i, 0)),
                  pl.BlockSpec((1, d), lambda i: (0, 0))],
        out_specs=[pl.BlockSpec((rows, d), lambda i: (i, 0)),
                   pl.BlockSpec((rows, LANES), lambda i: (i, 0))],
        out_shape=[jax.ShapeDtypeStruct((n, d), BF16), jax.ShapeDtypeStruct((n, LANES), F32)],
        compiler_params=_cparams("parallel"),
    )(x, gain.reshape(1, d).astype(F32))
    return x, xb, ss


def _rope_lane_consts(layout):
    lane = np.arange(LANES)
    if layout == "half128":
        rot_dim, freq_idx, active = LANES, lane % 64, np.ones(LANES, bool)
        rolls = {64: np.where(lane < 64, -1.0, 1.0)}
    elif layout == "spread64":
        rot_dim, freq_idx, active = 64, lane % 32, (lane % 64) < 32
        rolls = {64: np.where(active, np.where(lane < 64, -1.0, 1.0), 0.0)}
    else:
        rot_dim, freq_idx, active = 64, lane % 32, lane < 64
        rolls = {32: np.where((lane >= 32) & (lane < 64), 1.0, 0.0),
                 96: np.where(lane < 32, -1.0, 0.0)}
    inv = np.where(active, ROPE_THETA ** (-(2.0 * freq_idx) / rot_dim), 0.0)
    signs = np.stack([rolls[k] for k in sorted(rolls)])
    return jnp.asarray(inv, F32).reshape(1, LANES), jnp.asarray(signs, F32), tuple(sorted(rolls))


def _rope_tab_kernel(pos_ref, inv_ref, sign_ref, c_ref, *s_refs):
    ang = pos_ref[...].astype(F32) * inv_ref[...]
    c_ref[...] = jnp.cos(ang)
    sin = jnp.sin(ang)
    for k, s_ref in enumerate(s_refs):
        s_ref[...] = sin * sign_ref[k:k + 1, :]


def rope_tables(pos_col, layout, rows=512):
    n = pos_col.shape[0]
    inv, signs, rolls = _rope_lane_consts(layout)
    tab = jax.ShapeDtypeStruct((n, LANES), F32)
    tabs = pl.pallas_call(
        _rope_tab_kernel,
        grid=(n // rows,),
        in_specs=[pl.BlockSpec((rows, 1), lambda i: (i, 0)),
                  pl.BlockSpec((1, LANES), lambda i: (0, 0)),
                  pl.BlockSpec((len(rolls), LANES), lambda i: (0, 0))],
        out_specs=[pl.BlockSpec((rows, LANES), lambda i: (i, 0))] * (1 + len(rolls)),
        out_shape=[tab] * (1 + len(rolls)),
        compiler_params=_cparams("parallel"),
    )(pos_col, inv, signs)
    return rolls, tabs


def _rope(x, c, s, rolls=(64,)):
    s = s if isinstance(s, (list, tuple)) else [s]
    y = x * c
    for shift, sk in zip(rolls, s):
        y = y + pltpu.roll(x, shift, 1) * sk
    return y


def _mla_attn_kernel(q_ref, kn_ref, v_ref, kr_ref, cq_ref, sq_ref, ck_ref, sk_ref,
                     o_ref, kfull, vfull, krs, *, tq, nq, hb, scale):
    hg = pl.program_id(1)
    qi = pl.program_id(2)

    @pl.when(jnp.logical_and(hg == 0, qi == 0))
    def _():
        krs[...] = _rope(kr_ref[...], ck_ref[...], sk_ref[...]).astype(BF16)

    @pl.when(qi == 0)
    def _():
        for hd in range(hb):
            kfull[hd, :, :LANES] = kn_ref[:, hd * LANES:(hd + 1) * LANES]
            kfull[hd, :, LANES:] = krs[...]
            vfull[hd, :, :LANES] = v_ref[:, hd * LANES:(hd + 1) * LANES]
            vfull[hd, :, LANES:] = jnp.ones((vfull.shape[1], LANES), BF16)

    cq = cq_ref[...]
    sq = sq_ref[...]
    diag_ok = ((lax.broadcasted_iota(I32, (tq, tq), 1) >> CHUNK_SHIFT)
               <= (lax.broadcasted_iota(I32, (tq, tq), 0) >> CHUNK_SHIFT))

    for i in range(nq):
        @pl.when(qi == i)
        def _(i=i):
            kv_len = (i + 1) * tq
            for hd in range(hb):
                base = hd * 2 * LANES
                qr = _rope(q_ref[:, base + LANES:base + 2 * LANES], cq, sq)
                q = jnp.concatenate([q_ref[:, base:base + LANES], qr], axis=1)
                q = (q * (scale * LOG2E)).astype(BF16)
                s = lax.dot_general(q, kfull[hd, :kv_len, :], _NT,
                                    preferred_element_type=F32)
                s_diag = jnp.where(diag_ok, s[:, kv_len - tq:], NEG_BIG)
                s = s_diag if i == 0 else jnp.concatenate([s[:, :kv_len - tq], s_diag], axis=1)
                p = jnp.exp2(s - jnp.max(s, axis=-1, keepdims=True))
                ol = jnp.dot(p.astype(BF16), vfull[hd, :kv_len, :],
                             preferred_element_type=F32)
                o_ref[:, hd * LANES:(hd + 1) * LANES] = (
                    ol[:, :LANES] / ol[:, LANES:LANES + 1]).astype(o_ref.dtype)


def mla_mixer(xs, next_gain, tabs64, bsz, seq, w_in, q_norm, w_uq, kv_norm, w_ukv, w_o):
    x_res, xb, ss = xs
    n = xb.shape[0]
    hq = MLA_HEADS
    lane = np.arange(LANES)
    rope_on = (lane % 64) < (MLA_ROPE // 2)
    rope_src = np.where(rope_on, (lane // 64) * (MLA_ROPE // 2) + lane % (MLA_ROPE // 2), 0)
    base = MLA_Q_LORA + MLA_KV_LORA
    kr_cols = w_in[:, base + rope_src] * jnp.asarray(rope_on, F32)
    w_in_p = jnp.concatenate([w_in[:, :base], kr_cols], axis=1).astype(BF16)
    hd = MLA_NOPE + MLA_ROPE
    idx = np.concatenate([np.concatenate([hh * hd + np.arange(MLA_NOPE),
                                          hh * hd + MLA_NOPE + rope_src]) for hh in range(hq)])
    msk = np.tile(np.concatenate([np.ones(MLA_NOPE, bool), rope_on]), hq)
    w_uq_p = (w_uq[:, idx] * jnp.asarray(msk, F32)).astype(BF16)
    w_ukv_p = w_ukv.reshape(MLA_KV_LORA, hq, 2, MLA_NOPE).transpose(0, 2, 1, 3)
    w_ukv_p = w_ukv_p.reshape(MLA_KV_LORA, 2 * hq * MLA_NOPE).astype(BF16)

    lat = matmul(xb, w_in_p, bm=512, bn=w_in_p.shape[1], row_ss=ss)
    cq = rmsnorm(lat, q_norm, width=MLA_Q_LORA, col_block=0)
    ckv = rmsnorm(lat, kv_norm, width=MLA_KV_LORA, col_block=MLA_Q_LORA // MLA_KV_LORA)
    q = matmul(cq, w_uq_p)
    kv = matmul(ckv, w_ukv_p, out_dtype=BF16)

    tq, hb = 256, 8
    nq = seq // tq
    ng = hq // hb
    c64, s64 = tabs64[1]
    kr_blk = base // LANES
    o = pl.pallas_call(
        functools.partial(_mla_attn_kernel, tq=tq, nq=nq, hb=hb, scale=float(hd) ** -0.5),
        grid=(bsz, ng, nq),
        in_specs=[pl.BlockSpec((tq, 2 * LANES * hb), lambda b, hh, i: (b * nq + i, hh)),
                  pl.BlockSpec((seq, LANES * hb), lambda b, hh, i: (b, hh)),
                  pl.BlockSpec((seq, LANES * hb), lambda b, hh, i: (b, ng + hh)),
                  pl.BlockSpec((seq, LANES), lambda b, hh, i: (b, kr_blk)),
                  pl.BlockSpec((tq, LANES), lambda b, hh, i: (b * nq + i, 0)),
                  pl.BlockSpec((tq, LANES), lambda b, hh, i: (b * nq + i, 0)),
                  pl.BlockSpec((seq, LANES), lambda b, hh, i: (b, 0)),
                  pl.BlockSpec((seq, LANES), lambda b, hh, i: (b, 0))],
        out_specs=pl.BlockSpec((tq, LANES * hb), lambda b, hh, i: (b * nq + i, hh)),
        out_shape=jax.ShapeDtypeStruct((n, hq * MLA_V), BF16),
        scratch_shapes=[pltpu.VMEM((hb, seq, 2 * LANES), BF16),
                        pltpu.VMEM((hb, seq, 2 * LANES), BF16),
                        pltpu.VMEM((seq, LANES), BF16)],
        compiler_params=_cparams("arbitrary", "arbitrary", "arbitrary"),
    )(q, kv, kv, lat, c64, s64, c64, s64)
    return matmul(o, w_o.astype(BF16), res=x_res, alpha=1.0, next_gain=next_gain)


ONES_ROWS = 16


def _proj_heads_kernel(a_ref, w_ref, ss_ref, *rest, d_norm, rolls, out_scale, transpose):
    tabs, o_ref = [t[...] for t in rest[:-1]], rest[-1]
    acc = lax.dot_general(a_ref[...], w_ref[...].astype(BF16), _NT, preferred_element_type=F32)
    acc = acc * _row_rstd(ss_ref, d_norm)
    for g in range(acc.shape[1] // LANES):
        blk = acc[:, g * LANES:(g + 1) * LANES]
        if transpose:
            o_ref[0, g, :LANES, :] = blk.T.astype(o_ref.dtype)
            o_ref[0, g, LANES:, :] = jnp.ones((ONES_ROWS, blk.shape[0]), o_ref.dtype)
        else:
            if tabs:
                blk = _rope(blk, tabs[0], tabs[1:], rolls)
            o_ref[0, g] = (blk * out_scale).astype(o_ref.dtype)


def project_heads(xb, row_ss, w_t, bsz, seq, *, col0, heads, tabs=None, out_scale=1.0,
                  transpose=False, out_dtype=BF16):
    m, k = xb.shape
    n = heads * LANES
    bm = min(1024, seq)
    bn = min(512, n)
    nt = seq // bm
    cb = col0 // bn
    rolls, tables = tabs if tabs is not None else ((), [])
    row_blk = lambda width: pl.BlockSpec((bm, width), lambda i, j: (i, 0))
    if transpose:
        out_spec = pl.BlockSpec((1, bn // LANES, LANES + ONES_ROWS, bm),
                                lambda i, j: (i // nt, j, 0, i % nt))
        out_shape = jax.ShapeDtypeStruct((bsz, heads, LANES + ONES_ROWS, seq), out_dtype)
    else:
        out_spec = pl.BlockSpec((1, bn // LANES, bm, LANES), lambda i, j: (i // nt, j, i % nt, 0))
        out_shape = jax.ShapeDtypeStruct((bsz, heads, seq, LANES), out_dtype)
    return pl.pallas_call(
        functools.partial(_proj_heads_kernel, d_norm=k, rolls=rolls, out_scale=out_scale,
                          transpose=transpose),
        grid=(m // bm, n // bn),
        in_specs=[row_blk(k), pl.BlockSpec((bn, k), lambda i, j: (cb + j, 0)), row_blk(LANES)]
        + [row_blk(LANES)] * len(tables),
        out_specs=out_spec,
        out_shape=out_shape,
        compiler_params=_cparams("parallel", "arbitrary"),
    )(xb, w_t, row_ss, *tables)


def _dsa_kernel(qi_ref, ki_ref, wi_ref, q_ref, k_ref, vt_ref, o_ref, key_scr, ot_scr,
                *, tq, tk, nq, topk, rep, idx_scale):
    qt = pl.program_id(1)
    q_chunk = (qt * tq + lax.broadcasted_iota(I32, (tk, tq), 1)) >> CHUNK_SHIFT
    k_iota = lax.broadcasted_iota(I32, (tk, tq), 0)
    groups = DSA_KV_HEADS

    for nb in range(nq * tq // tk):
        @pl.when((qt * tq) // tk == nb)
        def _(nb=nb):
            nblk = nb + 1
            kv_len = nblk * tk

            def idx_body(j, _):
                off = pl.multiple_of(j * tk, tk)
                kib = ki_ref[0, 0, pl.ds(off, tk), :]
                acc = jnp.zeros((tk, tq), F32)
                for hh in range(IDX_HEADS):
                    d = lax.dot_general(kib, qi_ref[0, hh], _NT, preferred_element_type=F32)
                    acc = acc + (wi_ref[0, hh:hh + 1, :] * idx_scale) * jnp.maximum(d, 0.0)
                bits = pltpu.bitcast(acc, I32)
                key = bits ^ ((bits >> 31) & 0x7FFFFFFF)
                valid = ((off + k_iota) >> CHUNK_SHIFT) <= q_chunk
                key_scr[pl.ds(off, tk), :] = jnp.where(valid, key, INT_MIN)
                return 0

            lax.fori_loop(0, nblk, idx_body, 0)

            def count_ge(cand):
                hit = jnp.where(key_scr[:kv_len, :] >= cand, 1, 0).astype(I32)
                c8 = jnp.sum(hit.reshape(kv_len // 8, 8, tq), axis=0)
                return jnp.sum(c8, axis=0, keepdims=True)

            thr = jnp.full((1, tq), INT_MIN, I32)
            thr = jnp.where(count_ge(jnp.zeros((1, tq), I32)) >= topk, 0, thr)

            def bit_body(i, t):
                cand = t + jnp.left_shift(jnp.int32(1), 30 - i)
                return jnp.where(count_ge(cand) >= topk, cand, t)

            thr = lax.fori_loop(0, 31, bit_body, thr)
            thr = jnp.maximum(thr, INT_MIN + 1)

            def group_body(g, _):
                qg = q_ref[0, pl.ds(g * rep, rep)].reshape(rep * tq, DSA_HEAD_DIM)
                s = lax.dot_general(k_ref[0, g, :kv_len, :], qg, _NT,
                                    preferred_element_type=F32)
                sel = key_scr[:kv_len, :] >= thr
                s = jnp.where(jnp.concatenate([sel] * rep, axis=1), s, NEG_BIG)
                p = jnp.exp2(s - jnp.max(s, axis=0, keepdims=True))
                acc = jnp.dot(vt_ref[0, g, :, :kv_len], p.astype(BF16),
                              preferred_element_type=F32)
                ot_scr[g] = acc[:DSA_HEAD_DIM] / acc[DSA_HEAD_DIM:DSA_HEAD_DIM + 1]
                return 0

            lax.fori_loop(0, groups, group_body, 0, unroll=2)

    for g in range(groups):
        for r in range(rep):
            hh = g * rep + r
            o_ref[:, hh * DSA_HEAD_DIM:(hh + 1) * DSA_HEAD_DIM] = (
                ot_scr[g, :, r * tq:(r + 1) * tq].T.astype(o_ref.dtype))


def dsa_mixer(xs, next_gain, tabs64, tabs128, bsz, seq, w_in, w_o):
    x_res, xb, ss = xs
    n = xb.shape[0]
    hq, g, hd = DSA_HEADS, DSA_KV_HEADS, DSA_HEAD_DIM
    rep = hq // g
    topk = min(DSA_TOPK_MAX, seq // 4)
    n_q, n_kv, n_qi = hq * hd, g * hd, IDX_HEADS * IDX_DIM
    n_qkv = n_q + 2 * n_kv
    main = n_qkv + n_qi

    proj = functools.partial(project_heads, xb, ss, w_in.T, bsz, seq)
    q_hm = proj(col0=0, heads=hq, tabs=tabs128, out_scale=float(hd) ** -0.5 * LOG2E)
    k_hm = proj(col0=n_q, heads=g, tabs=tabs128)
    vt = proj(col0=n_q + n_kv, heads=g, transpose=True)
    qi_hm = proj(col0=n_qkv, heads=IDX_HEADS, tabs=tabs64)
    ki_hm = proj(col0=main, heads=1, tabs=tabs64)
    wi = proj(col0=main + IDX_DIM, heads=1, out_dtype=F32)
    wi_t = wi[:, 0, :, :IDX_HEADS].transpose(0, 2, 1)
    tq, tk = 128, 256

    nq = seq // tq
    o = pl.pallas_call(
        functools.partial(_dsa_kernel, tq=tq, tk=tk, nq=nq, topk=topk, rep=rep,
                          idx_scale=float(IDX_HEADS) ** -0.5 * float(IDX_DIM) ** -0.5),
        grid=(bsz, nq),
        in_specs=[pl.BlockSpec((1, IDX_HEADS, tq, LANES), lambda b, i: (b, 0, i, 0)),
                  pl.BlockSpec((1, 1, seq, LANES), lambda b, i: (b, 0, 0, 0)),
                  pl.BlockSpec((1, IDX_HEADS, tq), lambda b, i: (b, 0, i)),
                  pl.BlockSpec((1, hq, tq, LANES), lambda b, i: (b, 0, i, 0)),
                  pl.BlockSpec((1, g, seq, LANES), lambda b, i: (b, 0, 0, 0)),
                  pl.BlockSpec((1, g, LANES + ONES_ROWS, seq), lambda b, i: (b, 0, 0, 0))],
        out_specs=pl.BlockSpec((tq, hq * hd), lambda b, i: (b * nq + i, 0)),
        out_shape=jax.ShapeDtypeStruct((n, hq * hd), BF16),
        scratch_shapes=[pltpu.VMEM((seq, tq), I32),
                        pltpu.VMEM((g, hd, rep * tq), F32)],
        compiler_params=_cparams("parallel", "arbitrary"),
    )(qi_hm, ki_hm, wi_t, q_hm, k_hm, vt)
    return matmul(o, w_o.astype(BF16), res=x_res, alpha=1.0, next_gain=next_gain)


SUBLANES = 8


def _rms(x, g):
    ms = jnp.mean(x * x, axis=-1, keepdims=True)
    return x * lax.rsqrt(ms + NORM_EPS) * g


def _shift_mix_kernel(x_ref, xp_ref, g_ref, mu_ref, w1_ref, a1_ref, g1_ref,
                      xr_ref, xk_ref, xv_ref, tw_ref, ta_ref, tg_ref):
    g = g_ref[...]
    h = _rms(x_ref[...], g)
    first_tile = pl.program_id(1) == 0
    last_prev = _rms(xp_ref[...], g)[SUBLANES - 1:SUBLANES, :]
    last_prev = jnp.where(first_tile, 0.0, last_prev)
    row = lax.broadcasted_iota(I32, h.shape, 0)
    h_prev = jnp.where(row == 0, last_prev, pltpu.roll(h, 1, 0))
    xx = h_prev - h

    def mix(i):
        return (h + xx * mu_ref[i:i + 1, :]).astype(BF16)

    xr_ref[...] = mix(0)
    xk_ref[...] = mix(2)
    xv_ref[...] = mix(3)
    dot = functools.partial(jnp.dot, preferred_element_type=F32)
    tw_ref[...] = jnp.tanh(dot(mix(1), w1_ref[...])).astype(tw_ref.dtype)
    ta_ref[...] = dot(mix(4), a1_ref[...]).astype(ta_ref.dtype)
    tg_ref[...] = _sigmoid(dot(mix(5), g1_ref[...])).astype(tg_ref.dtype)


def token_shift_mix(x, gain, mu, w1, a1, g1, bsz, seq, rows=256):
    n, d = x.shape
    nt = seq // rows
    per = rows // SUBLANES
    blk = lambda width: pl.BlockSpec((rows, width), lambda b, t: (b * nt + t, 0))
    whole = lambda a: pl.BlockSpec(a.shape, lambda b, t: (0, 0))
    prev = pl.BlockSpec((SUBLANES, d), lambda b, t: (jnp.maximum((b * nt + t) * per - 1, 0), 0))
    gain_row = gain.reshape(1, d).astype(F32)
    widths = [d, d, d, w1.shape[1], a1.shape[1], g1.shape[1]]
    return pl.pallas_call(
        _shift_mix_kernel,
        grid=(bsz, nt),
        in_specs=[blk(d), prev, whole(gain_row), whole(mu), whole(w1), whole(a1), whole(g1)],
        out_specs=[blk(w) for w in widths],
        out_shape=[jax.ShapeDtypeStruct((n, w), BF16) for w in widths],
        compiler_params=_cparams("parallel", "arbitrary"),
    )(x, x, gain_row, mu, w1, a1, g1)


def _split3(x):
    hi = x.astype(BF16)
    r1 = x - hi.astype(F32)
    mid = r1.astype(BF16)
    lo = (r1 - mid.astype(F32)).astype(BF16)
    return hi, mid, lo


def _dot3_right(x, w_bf16):
    hi, mid, lo = _split3(x)
    d = functools.partial(jnp.dot, preferred_element_type=F32)
    return d(hi, w_bf16) + d(mid, w_bf16) + d(lo, w_bf16)


def _dot3_left(w_bf16, x):
    hi, mid, lo = _split3(x)
    d = functools.partial(jnp.dot, preferred_element_type=F32)
    return d(w_bf16, hi) + d(w_bf16, mid) + d(w_bf16, lo)


def _wkv_kernel(r_ref, k_ref, v_ref, wl_ref, al_ref, g_ref,
                w0_ref, a0_ref, kk_ref, ka_ref, rk_ref, lnw_ref, lnb_ref,
                o_ref, state, *, L, pairs):
    c_idx = pl.program_id(2)

    @pl.when(c_idx == 0)
    def _():
        state[...] = jnp.zeros_like(state)

    hn = RWKV_HEAD
    hshift = hn.bit_length() - 1
    li = lax.broadcasted_iota(I32, (LANES, LANES), 0)
    lj = lax.broadcasted_iota(I32, (LANES, LANES), 1)
    same_head = (li >> hshift) == (lj >> hshift)
    seg_avg = jnp.where(same_head, 1.0 / hn, 0.0).astype(BF16)

    def seg_mean(x):
        lhs = jnp.concatenate([part[:, p * LANES:(p + 1) * LANES]
                               for part in _split3(x) for p in range(pairs)], axis=0)
        res = jnp.dot(lhs, seg_avg, preferred_element_type=F32)
        rows = pairs * L
        tot = res[:rows] + res[rows:2 * rows] + res[2 * rows:]
        return jnp.concatenate([tot[p * L:(p + 1) * L] for p in range(pairs)], axis=1)
    ti = lax.broadcasted_iota(I32, (L, L), 0)
    tj = lax.broadcasted_iota(I32, (L, L), 1)
    tril_incl = jnp.where(ti >= tj, 1.0, 0.0).astype(BF16)
    hr = lax.broadcasted_iota(I32, (L, LANES), 0)
    hc = lax.broadcasted_iota(I32, (L, LANES), 1)
    left = hc < hn
    strict = hr > (hc & (hn - 1))
    incl = hr >= (hc & (hn - 1))

    r_all = r_ref[...]
    k_all = k_ref[...]
    v_all = v_ref[...]
    z = w0_ref[...] + wl_ref[...]
    u = -z
    softplus = jnp.maximum(u, 0.0) + jnp.log(1.0 + jnp.exp(-jnp.abs(u)))
    w_log = -softplus - 0.5
    lw = -jnp.exp(w_log)
    a_sig = _sigmoid(a0_ref[...] + al_ref[...])
    k2_all = k_all * (1.0 + (a_sig - 1.0) * ka_ref[...])
    kk_all = k_all * kk_ref[...]
    csum = _dot3_left(tril_incl, lw)
    c_last = csum[L - 1:L, :]
    e_c = jnp.exp(csum)
    e_cprev = jnp.exp(csum - lw)
    e_neg = jnp.exp(-csum)
    e_rem = jnp.exp(c_last - csum)
    gam_all = jnp.exp(c_last)

    bdot = functools.partial(jnp.dot, preferred_element_type=F32)
    cat = jnp.concatenate
    prs = range(pairs)
    sls = [slice(p * LANES, (p + 1) * LANES) for p in prs]
    ksq = seg_mean(kk_all * kk_all)
    kk_n = kk_all / jnp.maximum(jnp.sqrt(hn * ksq), 1e-12)
    bvec_all = kk_n * a_sig
    at_all = -kk_n * e_cprev
    rt_all = r_all * e_c
    bt = [(bvec_all[:, sl] * e_neg[:, sl]).astype(BF16) for sl in sls]
    kt = [(k2_all[:, sl] * e_neg[:, sl]).astype(BF16) for sl in sls]
    a0 = [jnp.where(left, at_all[:, sl], 0.0).astype(BF16) for sl in sls]
    a1 = [jnp.where(left, 0.0, at_all[:, sl]).astype(BF16) for sl in sls]
    r0 = [jnp.where(left, rt_all[:, sl], 0.0).astype(BF16) for sl in sls]
    r1 = [jnp.where(left, 0.0, rt_all[:, sl]).astype(BF16) for sl in sls]
    vsw = [cat([jnp.where(left, 0.0, v_all[:, sl]), jnp.where(left, v_all[:, sl], 0.0)],
               axis=0).astype(BF16) for sl in sls]

    lhs = [cat([a0[p], r0[p], a1[p], r1[p]], axis=0) for p in prs]
    o01 = [lax.dot_general(lhs[p], cat([bt[p], kt[p]], axis=0), _NT,
                           preferred_element_type=F32) for p in prs]
    o0 = [o01[p][:2 * L] for p in prs]
    o1 = [pltpu.roll(o01[p][2 * L:], hn, 1) for p in prs]
    st = [state[p] for p in prs]
    ars = [lax.dot_general(lhs[p], st[p].astype(BF16), _NT,
                           preferred_element_type=F32) for p in prs]
    u0 = [jnp.where(strict, o0[p][:L], 0.0) for p in prs]
    u1 = [jnp.where(strict, o1[p][:L], 0.0) for p in prs]
    m_ab = [cat([jnp.where(left, u0[p], 0.0), jnp.where(left, 0.0, u1[p])], axis=0) for p in prs]
    m_ak = [cat([jnp.where(left, 0.0, u0[p]), jnp.where(left, u1[p], 0.0)], axis=0) for p in prs]
    xt = [(cat([ars[p][:L], ars[p][2 * L:3 * L]], axis=0)
           + bdot(m_ak[p].astype(BF16), vsw[p])).T for p in prs]
    mt = [m_ab[p].T for p in prs]
    steps = (L - 1).bit_length()
    for it in range(steps):
        mb = [mt[p].astype(BF16) for p in prs]
        if it + 1 < steps:
            prod = [bdot(cat([xt[p].astype(BF16), mb[p]], axis=0), mb[p]) for p in prs]
            xt = [xt[p] + prod[p][:LANES] for p in prs]
            mt = [prod[p][LANES:] for p in prs]
        else:
            xt = [xt[p] + bdot(xt[p].astype(BF16), mb[p]) for p in prs]
    x = [xt[p].T for p in prs]
    y0 = [jnp.where(incl, o0[p][L:], 0.0) for p in prs]
    y1 = [jnp.where(incl, o1[p][L:], 0.0) for p in prs]
    m_rb = [cat([jnp.where(left, y0[p], 0.0), jnp.where(left, 0.0, y1[p])], axis=0) for p in prs]
    m_rk = [cat([jnp.where(left, 0.0, y0[p]), jnp.where(left, y1[p], 0.0)], axis=0) for p in prs]
    ys = [cat([ars[p][L:2 * L], ars[p][3 * L:]], axis=0)
          + bdot(cat([m_rb[p], m_rk[p]], axis=1).astype(BF16),
                 cat([x[p].astype(BF16), vsw[p]], axis=0)) for p in prs]
    y_all = cat([ys[p][:L] + ys[p][L:] for p in prs], axis=1)
    sv_t = [cat([x[p][:L] + x[p][L:], v_all[:, sl]], axis=0).T.astype(BF16)
            for p, sl in enumerate(sls)]
    bh_all = (bvec_all * e_rem).astype(BF16)
    kh_all = (k2_all * e_rem).astype(BF16)
    bkh = [cat([bh_all[:, sl], kh_all[:, sl]], axis=0) for sl in sls]
    upd = [bdot(sv_t[p], bkh[p]) for p in prs]
    for p, sl in enumerate(sls):
        state[p] = st[p] * gam_all[:, sl] + jnp.where(same_head, upd[p], 0.0)

    yc = y_all - seg_mean(y_all)
    var = seg_mean(yc * yc)
    rk_sum = hn * seg_mean(r_all * k2_all * rk_ref[...])
    yn = yc * lax.rsqrt(var + GN_EPS) * lnw_ref[...] + lnb_ref[...]
    o_ref[...] = ((yn + rk_sum * v_all) * g_ref[...]).astype(o_ref.dtype)


def rwkv7_mixer(xs, gain, next_gain, bsz, seq, mu, w_r, w_k, w_v, w_o, w0, w1, w2, a0, a1, a2,
                g1, g2, k_k, k_a, r_k, lnx_w, lnx_b):
    x_res = xs[0]
    n, d = x_res.shape
    gpad = (-GATE_LORA) % LANES
    g1p = jnp.pad(g1, ((0, 0), (0, gpad))).astype(BF16)
    g2p = jnp.pad(g2, ((0, gpad), (0, 0))).astype(BF16)
    xr, xk, xv, tw, ta, tg = token_shift_mix(x_res, gain, mu, w1.astype(BF16), a1.astype(BF16),
                                             g1p, bsz, seq)
    r = matmul(xr, w_r)
    k = matmul(xk, w_k)
    v = matmul(xv, w_v)
    wl = matmul(tw, w2.astype(BF16))
    al = matmul(ta, a2.astype(BF16))
    gate = matmul(tg, g2p)

    L, pairs = CHUNK, 32
    nc = seq // L
    wide = LANES * pairs
    row = lambda a: a.reshape(1, d).astype(F32)
    dat = pl.BlockSpec((L, wide), lambda b, j, c: (b * nc + c, j))
    par = pl.BlockSpec((1, wide), lambda b, j, c: (0, j))
    yg = pl.pallas_call(
        functools.partial(_wkv_kernel, L=L, pairs=pairs),
        grid=(bsz, d // wide, nc),
        in_specs=[dat] * 6 + [par] * 7,
        out_specs=dat,
        out_shape=jax.ShapeDtypeStruct((n, d), BF16),
        scratch_shapes=[pltpu.VMEM((pairs, LANES, LANES), F32)],
        compiler_params=_cparams("parallel", "parallel", "arbitrary"),
    )(r, k, v, wl, al, gate, row(w0), row(a0), row(k_k), row(k_a), row(r_k), row(lnx_w), row(lnx_b))
    return matmul(yg, w_o.astype(BF16), res=x_res, alpha=1.0, next_gain=next_gain)


def kernel(x, positions, ffn_norm_0, ffn_w_gu_0, ffn_w_down_0, mix_norm_0, mla0_w_in, mla0_q_norm, mla0_w_uq, mla0_kv_norm, mla0_w_ukv, mla0_w_o, ffn_norm_1, ffn_w_gu_1, ffn_w_down_1, mix_norm_1, dsa1_w_in, dsa1_w_o, ffn_norm_2, ffn_w_gu_2, ffn_w_down_2, mix_norm_2, rwkv2_mu, rwkv2_w_r, rwkv2_w_k, rwkv2_w_v, rwkv2_w_o, rwkv2_w0, rwkv2_w1, rwkv2_w2, rwkv2_a0, rwkv2_a1, rwkv2_a2, rwkv2_g1, rwkv2_g2, rwkv2_k_k, rwkv2_k_a, rwkv2_r_k, rwkv2_lnx_w, rwkv2_lnx_b, ffn_norm_3, ffn_w_gu_3, ffn_w_down_3, mix_norm_3, mla3_w_in, mla3_q_norm, mla3_w_uq, mla3_kv_norm, mla3_w_ukv, mla3_w_o, final_norm):
    bsz, seq, d = x.shape
    n = bsz * seq
    pos_col = positions.reshape(n, 1)
    tabs_mla = rope_tables(pos_col, "spread64")
    tabs128 = rope_tables(pos_col, "half128")
    tabs64 = rope_tables(pos_col, "lead64")

    def mla(xs, nxt, w_in, q_norm, w_uq, kv_norm, w_ukv, w_o):
        return mla_mixer(xs, nxt, tabs_mla, bsz, seq, w_in, q_norm, w_uq, kv_norm, w_ukv, w_o)

    def rwkv(xs, gain, nxt):
        return rwkv7_mixer(xs, gain, nxt, bsz, seq, rwkv2_mu, rwkv2_w_r, rwkv2_w_k, rwkv2_w_v,
                           rwkv2_w_o, rwkv2_w0, rwkv2_w1, rwkv2_w2, rwkv2_a0, rwkv2_a1, rwkv2_a2,
                           rwkv2_g1, rwkv2_g2, rwkv2_k_k, rwkv2_k_a,
                           rwkv2_r_k.reshape(-1), rwkv2_lnx_w, rwkv2_lnx_b)

    mixers = [
        lambda xs, gn, nxt: mla(xs, nxt, mla0_w_in, mla0_q_norm, mla0_w_uq, mla0_kv_norm,
                                mla0_w_ukv, mla0_w_o),
        lambda xs, gn, nxt: dsa_mixer(xs, nxt, tabs64, tabs128, bsz, seq, dsa1_w_in, dsa1_w_o),
        rwkv,
        lambda xs, gn, nxt: mla(xs, nxt, mla3_w_in, mla3_q_norm, mla3_w_uq, mla3_kv_norm,
                                mla3_w_ukv, mla3_w_o),
    ]
    ffns = [(ffn_norm_0, ffn_w_gu_0, ffn_w_down_0, mix_norm_0),
            (ffn_norm_1, ffn_w_gu_1, ffn_w_down_1, mix_norm_1),
            (ffn_norm_2, ffn_w_gu_2, ffn_w_down_2, mix_norm_2),
            (ffn_norm_3, ffn_w_gu_3, ffn_w_down_3, mix_norm_3)]
    xs = norm_prep(x.reshape(n, d), ffn_norm_0[0])
    for i in range(4):
        f_norm, w_gu, w_down, m_norm = ffns[i]
        after_layer = ffns[i + 1][0][0] if i + 1 < len(ffns) else final_norm
        xs = ffn_half_step(xs, w_gu, w_down, 0, m_norm)
        xs = mixers[i](xs, m_norm, f_norm[1])
        xs = ffn_half_step(xs, w_gu, w_down, 1, after_layer)
    return rmsnorm(xs[0], final_norm, out_dtype=F32).reshape(bsz, seq, d)
```

```python
import functools

import numpy as np
import jax
import jax.numpy as jnp
from jax import lax
from jax.experimental import pallas as pl
from jax.experimental.pallas import tpu as pltpu

F32 = jnp.float32
BF16 = jnp.bfloat16
I32 = jnp.int32

CHUNK = 64
CHUNK_SHIFT = CHUNK.bit_length() - 1
ROPE_THETA = 10000.0
NORM_EPS = 1e-6
FFN_RES = 0.5
MLA_HEADS = 32
MLA_Q_LORA = 1024
MLA_KV_LORA = 512
MLA_NOPE = 128
MLA_ROPE = 64
MLA_V = 128
DSA_HEADS = 32
DSA_KV_HEADS = 8
DSA_HEAD_DIM = 128
IDX_HEADS = 32
IDX_DIM = 128
DSA_TOPK_MAX = 256
RWKV_HEAD = 64
GATE_LORA = 480
GN_EPS = 64e-5

LANES = 128
VMEM_LIMIT = 56 * 1024 * 1024
INT_MIN = -(2 ** 31)
NEG_BIG = -1e30
LOG2E = 1.4426950408889634

_NT = (((1,), (1,)), ((), ()))


def _cparams(*sem):
    return pltpu.CompilerParams(dimension_semantics=sem, vmem_limit_bytes=VMEM_LIMIT)


def _sigmoid(x):
    return 1.0 / (1.0 + jnp.exp(-x))


def _rmsnorm_kernel(x_ref, g_ref, o_ref):
    o_ref[...] = _rms(x_ref[...].astype(F32), g_ref[...]).astype(o_ref.dtype)


def rmsnorm(x, g, *, width=None, col_block=0, out_dtype=BF16, rows=256):
    n = x.shape[0]
    w = x.shape[1] if width is None else width
    return pl.pallas_call(
        _rmsnorm_kernel,
        grid=(n // rows,),
        in_specs=[pl.BlockSpec((rows, w), lambda i: (i, col_block)),
                  pl.BlockSpec((1, w), lambda i: (0, 0))],
        out_specs=pl.BlockSpec((rows, w), lambda i: (i, 0)),
        out_shape=jax.ShapeDtypeStruct((n, w), out_dtype),
        compiler_params=_cparams("parallel"),
    )(x, g.reshape(1, w).astype(F32))


def _row_rstd(ss_ref, d_norm):
    return lax.rsqrt(jnp.sum(ss_ref[...], axis=-1, keepdims=True) * (1.0 / d_norm) + NORM_EPS)


def _lane_partial_sumsq(x):
    sq = x * x
    part = sq[:, :LANES]
    for c in range(1, x.shape[1] // LANES):
        part = part + sq[:, c * LANES:(c + 1) * LANES]
    return part


def _mm_kernel(*refs, act, alpha, has_res, d_norm, emit_norm):
    refs = list(refs)
    a_ref, w_ref = refs[:2]
    del refs[:2]
    ss_ref = refs.pop(0) if d_norm else None
    r_ref = refs.pop(0) if has_res else None
    gn_ref = refs.pop(0) if emit_norm else None
    o_ref = refs.pop(0)
    acc = jnp.dot(a_ref[...], w_ref[...].astype(BF16), preferred_element_type=F32)
    if d_norm:
        acc = acc * _row_rstd(ss_ref, d_norm)
    if act == "tanh":
        acc = jnp.tanh(acc)
    elif act == "sigmoid":
        acc = _sigmoid(acc)
    if has_res:
        acc = r_ref[...] + alpha * acc
    o_ref[...] = acc.astype(o_ref.dtype)
    if emit_norm:
        xb_ref, sso_ref = refs
        xb_ref[...] = (acc * gn_ref[...]).astype(BF16)
        part = _lane_partial_sumsq(acc)

        @pl.when(pl.program_id(1) == 0)
        def _():
            sso_ref[...] = part

        @pl.when(pl.program_id(1) != 0)
        def _():
            sso_ref[...] += part


def _mm_tiles(m, k, n, w_itemsize, n_out_tiles):
    bm = min(1024, m)
    a_bytes = 2 * bm * k * 2
    for bn in (1024, 512, 256, 128):
        w_bytes = bn * k * (2 * w_itemsize + (2 if w_itemsize > 2 else 0))
        o_bytes = bm * bn * 4 * (2 * n_out_tiles + 1)
        if n % bn == 0 and a_bytes + w_bytes + o_bytes <= VMEM_LIMIT * 0.85:
            return bm, bn
    return bm, n


def matmul(a, w, *, out_dtype=F32, act=None, res=None, alpha=1.0, bm=None, bn=None, w_index=None,
           row_ss=None, next_gain=None):
    m, k = a.shape
    n = w.shape[-1]
    emit_norm = next_gain is not None
    tbm, tbn = _mm_tiles(m, k, n, w.dtype.itemsize, 1 + (res is not None) + emit_norm)
    bm = tbm if bm is None else bm
    bn = tbn if bn is None else bn
    if w.ndim == 3:
        w_spec = pl.BlockSpec((None, k, bn), lambda i, j: (w_index, 0, j))
    else:
        w_spec = pl.BlockSpec((k, bn), lambda i, j: (0, j))
    row_spec = pl.BlockSpec((bm, LANES), lambda i, j: (i, 0))
    tile_spec = pl.BlockSpec((bm, bn), lambda i, j: (i, j))
    in_specs = [pl.BlockSpec((bm, k), lambda i, j: (i, 0)), w_spec]
    args = [a, w]
    if row_ss is not None:
        in_specs.append(row_spec)
        args.append(row_ss)
    if res is not None:
        in_specs.append(tile_spec)
        args.append(res)
    out_specs, out_shape = tile_spec, jax.ShapeDtypeStruct((m, n), out_dtype)
    if emit_norm:
        in_specs.append(pl.BlockSpec((1, bn), lambda i, j: (0, j)))
        args.append(next_gain.reshape(1, n).astype(F32))
        out_specs = [tile_spec, tile_spec, row_spec]
        out_shape = [out_shape, jax.ShapeDtypeStruct((m, n), BF16),
                     jax.ShapeDtypeStruct((m, LANES), F32)]
    return pl.pallas_call(
        functools.partial(_mm_kernel, act=act, alpha=alpha, has_res=res is not None,
                          d_norm=k if row_ss is not None else 0, emit_norm=emit_norm),
        grid=(m // bm, n // bn),
        in_specs=in_specs,
        out_specs=out_specs,
        out_shape=out_shape,
        compiler_params=_cparams("parallel", "arbitrary"),
    )(*args)


def _swiglu_kernel(a_ref, ss_ref, wg_ref, wu_ref, o_ref, *, d_norm):
    a = a_ref[...]
    rstd = _row_rstd(ss_ref, d_norm)
    g = jnp.dot(a, wg_ref[...].astype(BF16), preferred_element_type=F32) * rstd
    u = jnp.dot(a, wu_ref[...].astype(BF16), preferred_element_type=F32) * rstd
    o_ref[...] = (g * _sigmoid(g) * u).astype(o_ref.dtype)


def swiglu_up(a, row_ss, w_gu, half, *, bm=2048, bf=256):
    m, k = a.shape
    bm = min(bm, m)
    f = w_gu.shape[-1] // 2
    nf = f // bf
    return pl.pallas_call(
        functools.partial(_swiglu_kernel, d_norm=k),
        grid=(m // bm, nf),
        in_specs=[pl.BlockSpec((bm, k), lambda i, j: (i, 0), pipeline_mode=pl.Buffered(1)),
                  pl.BlockSpec((bm, LANES), lambda i, j: (i, 0)),
                  pl.BlockSpec((None, k, bf), lambda i, j: (half, 0, j)),
                  pl.BlockSpec((None, k, bf), lambda i, j: (half, 0, j + nf))],
        out_specs=pl.BlockSpec((bm, bf), lambda i, j: (i, j)),
        out_shape=jax.ShapeDtypeStruct((m, f), BF16),
        compiler_params=_cparams("parallel", "arbitrary"),
    )(a, row_ss, w_gu, w_gu)


def ffn_half_step(xs, w_gu, w_down, half, next_gain):
    x, xb, ss = xs
    act = swiglu_up(xb, ss, w_gu, half)
    return matmul(act, w_down, res=x, alpha=FFN_RES, bm=min(1024, x.shape[0]), bn=256,
                  w_index=half, next_gain=next_gain)


def _norm_prep_kernel(x_ref, g_ref, xb_ref, ss_ref):
    x = x_ref[...]
    xb_ref[...] = (x * g_ref[...]).astype(BF16)
    ss_ref[...] = _lane_partial_sumsq(x)


def norm_prep(x, gain, rows=256):
    n, d = x.shape
    xb, ss = pl.pallas_call(
        _norm_prep_kernel,
        grid=(n // rows,),
        in_specs=[pl.BlockSpec((rows, d), lambda i: (i, 0)),
                  pl.BlockSpec((1, d), lambda i: (0, 0))],
        out_specs=[pl.BlockSpec((rows, d), lambda i: (i, 0)),
                   pl.BlockSpec((rows, LANES), lambda i: (i, 0))],
        out_shape=[jax.ShapeDtypeStruct((n, d), BF16), jax.ShapeDtypeStruct((n, LANES), F32)],
        compiler_params=_cparams("parallel"),
    )(x, gain.reshape(1, d).astype(F32))
    return x, xb, ss


def _rope_lane_consts(layout):
    lane = np.arange(LANES)
    if layout == "half128":
        rot_dim, freq_idx, active = LANES, lane % 64, np.ones(LANES, bool)
        rolls = {64: np.where(lane < 64, -1.0, 1.0)}
    elif layout == "spread64":
        rot_dim, freq_idx, active = 64, lane % 32, (lane % 64) < 32
        rolls = {64: np.where(active, np.where(lane < 64, -1.0, 1.0), 0.0)}
    else:
        rot_dim, freq_idx, active = 64, lane % 32, lane < 64
        rolls = {32: np.where((lane >= 32) & (lane < 64), 1.0, 0.0),
                 96: np.where(lane < 32, -1.0, 0.0)}
    inv = np.where(active, ROPE_THETA ** (-(2.0 * freq_idx) / rot_dim), 0.0)
    signs = np.stack([rolls[k] for k in sorted(rolls)])
    return jnp.asarray(inv, F32).reshape(1, LANES), jnp.asarray(signs, F32), tuple(sorted(rolls))


def _rope_tab_kernel(pos_ref, inv_ref, sign_ref, c_ref, *s_refs):
    ang = pos_ref[...].astype(F32) * inv_ref[...]
    c_ref[...] = jnp.cos(ang)
    sin = jnp.sin(ang)
    for k, s_ref in enumerate(s_refs):
        s_ref[...] = sin * sign_ref[k:k + 1, :]


def rope_tables(pos_col, layout, rows=512):
    n = pos_col.shape[0]
    inv, signs, rolls = _rope_lane_consts(layout)
    tab = jax.ShapeDtypeStruct((n, LANES), F32)
    tabs = pl.pallas_call(
        _rope_tab_kernel,
        grid=(n // rows,),
        in_specs=[pl.BlockSpec((rows, 1), lambda i: (i, 0)),
                  pl.BlockSpec((1, LANES), lambda i: (0, 0)),
                  pl.BlockSpec((len(rolls), LANES), lambda i: (0, 0))],
        out_specs=[pl.BlockSpec((rows, LANES), lambda i: (i, 0))] * (1 + len(rolls)),
        out_shape=[tab] * (1 + len(rolls)),
        compiler_params=_cparams("parallel"),
    )(pos_col, inv, signs)
    return rolls, tabs


def _rope(x, c, s, rolls=(64,)):
    s = s if isinstance(s, (list, tuple)) else [s]
    y = x * c
    for shift, sk in zip(rolls, s):
        y = y + pltpu.roll(x, shift, 1) * sk
    return y


def _mla_attn_kernel(q_ref, kn_ref, v_ref, kr_ref, cq_ref, sq_ref, ck_ref, sk_ref,
                     o_ref, kfull, vfull, krs, *, tq, nq, hb, scale):
    hg = pl.program_id(1)
    qi = pl.program_id(2)

    @pl.when(jnp.logical_and(hg == 0, qi == 0))
    def _():
        krs[...] = _rope(kr_ref[...], ck_ref[...], sk_ref[...]).astype(BF16)

    @pl.when(qi == 0)
    def _():
        for hd in range(hb):
            kfull[hd, :, :LANES] = kn_ref[:, hd * LANES:(hd + 1) * LANES]
            kfull[hd, :, LANES:] = krs[...]
            vfull[hd, :, :LANES] = v_ref[:, hd * LANES:(hd + 1) * LANES]
            vfull[hd, :, LANES:] = jnp.ones((vfull.shape[1], LANES), BF16)

    cq = cq_ref[...]
    sq = sq_ref[...]
    diag_ok = ((lax.broadcasted_iota(I32, (tq, tq), 1) >> CHUNK_SHIFT)
               <= (lax.broadcasted_iota(I32, (tq, tq), 0) >> CHUNK_SHIFT))

    for i in range(nq):
        @pl.when(qi == i)
        def _(i=i):
            kv_len = (i + 1) * tq
            for hd in range(hb):
                base = hd * 2 * LANES
                qr = _rope(q_ref[:, base + LANES:base + 2 * LANES], cq, sq)
                q = jnp.concatenate([q_ref[:, base:base + LANES], qr], axis=1)
                q = (q * (scale * LOG2E)).astype(BF16)
                s = lax.dot_general(q, kfull[hd, :kv_len, :], _NT,
                                    preferred_element_type=F32)
                s_diag = jnp.where(diag_ok, s[:, kv_len - tq:], NEG_BIG)
                s = s_diag if i == 0 else jnp.concatenate([s[:, :kv_len - tq], s_diag], axis=1)
                p = jnp.exp2(s - jnp.max(s, axis=-1, keepdims=True))
                ol = jnp.dot(p.astype(BF16), vfull[hd, :kv_len, :],
                             preferred_element_type=F32)
                o_ref[:, hd * LANES:(hd + 1) * LANES] = (
                    ol[:, :LANES] / ol[:, LANES:LANES + 1]).astype(o_ref.dtype)


def mla_mixer(xs, next_gain, tabs64, bsz, seq, w_in, q_norm, w_uq, kv_norm, w_ukv, w_o):
    x_res, xb, ss = xs
    n = xb.shape[0]
    hq = MLA_HEADS
    lane = np.arange(LANES)
    rope_on = (lane % 64) < (MLA_ROPE // 2)
    rope_src = np.where(rope_on, (lane // 64) * (MLA_ROPE // 2) + lane % (MLA_ROPE // 2), 0)
    base = MLA_Q_LORA + MLA_KV_LORA
    kr_cols = w_in[:, base + rope_src] * jnp.asarray(rope_on, F32)
    w_in_p = jnp.concatenate([w_in[:, :base], kr_cols], axis=1).astype(BF16)
    hd = MLA_NOPE + MLA_ROPE
    idx = np.concatenate([np.concatenate([hh * hd + np.arange(MLA_NOPE),
                                          hh * hd + MLA_NOPE + rope_src]) for hh in range(hq)])
    msk = np.tile(np.concatenate([np.ones(MLA_NOPE, bool), rope_on]), hq)
    w_uq_p = (w_uq[:, idx] * jnp.asarray(msk, F32)).astype(BF16)
    w_ukv_p = w_ukv.reshape(MLA_KV_LORA, hq, 2, MLA_NOPE).transpose(0, 2, 1, 3)
    w_ukv_p = w_ukv_p.reshape(MLA_KV_LORA, 2 * hq * MLA_NOPE).astype(BF16)

    lat = matmul(xb, w_in_p, bm=512, bn=w_in_p.shape[1], row_ss=ss)
    cq = rmsnorm(lat, q_norm, width=MLA_Q_LORA, col_block=0)
    ckv = rmsnorm(lat, kv_norm, width=MLA_KV_LORA, col_block=MLA_Q_LORA // MLA_KV_LORA)
    q = matmul(cq, w_uq_p)
    kv = matmul(ckv, w_ukv_p, out_dtype=BF16)

    tq, hb = 256, 8
    nq = seq // tq
    ng = hq // hb
    c64, s64 = tabs64[1]
    kr_blk = base // LANES
    o = pl.pallas_call(
        functools.partial(_mla_attn_kernel, tq=tq, nq=nq, hb=hb, scale=float(hd) ** -0.5),
        grid=(bsz, ng, nq),
        in_specs=[pl.BlockSpec((tq, 2 * LANES * hb), lambda b, hh, i: (b * nq + i, hh)),
                  pl.BlockSpec((seq, LANES * hb), lambda b, hh, i: (b, hh)),
                  pl.BlockSpec((seq, LANES * hb), lambda b, hh, i: (b, ng + hh)),
                  pl.BlockSpec((seq, LANES), lambda b, hh, i: (b, kr_blk)),
                  pl.BlockSpec((tq, LANES), lambda b, hh, i: (b * nq + i, 0)),
                  pl.BlockSpec((tq, LANES), lambda b, hh, i: (b * nq + i, 0)),
                  pl.BlockSpec((seq, LANES), lambda b, hh, i: (b, 0)),
                  pl.BlockSpec((seq, LANES), lambda b, hh, i: (b, 0))],
        out_specs=pl.BlockSpec((tq, LANES * hb), lambda b, hh, i: (b * nq + i, hh)),
        out_shape=jax.ShapeDtypeStruct((n, hq * MLA_V), BF16),
        scratch_shapes=[pltpu.VMEM((hb, seq, 2 * LANES), BF16),
                        pltpu.VMEM((hb, seq, 2 * LANES), BF16),
                        pltpu.VMEM((seq, LANES), BF16)],
        compiler_params=_cparams("arbitrary", "arbitrary", "arbitrary"),
    )(q, kv, kv, lat, c64, s64, c64, s64)
    return matmul(o, w_o.astype(BF16), res=x_res, alpha=1.0, next_gain=next_gain)


ONES_ROWS = 16


def _proj_heads_kernel(a_ref, w_ref, ss_ref, *rest, d_norm, rolls, out_scale, transpose):
    tabs, o_ref = [t[...] for t in rest[:-1]], rest[-1]
    acc = lax.dot_general(a_ref[...], w_ref[...].astype(BF16), _NT, preferred_element_type=F32)
    acc = acc * _row_rstd(ss_ref, d_norm)
    for g in range(acc.shape[1] // LANES):
        blk = acc[:, g * LANES:(g + 1) * LANES]
        if transpose:
            o_ref[0, g, :LANES, :] = blk.T.astype(o_ref.dtype)
            o_ref[0, g, LANES:, :] = jnp.ones((ONES_ROWS, blk.shape[0]), o_ref.dtype)
        else:
            if tabs:
                blk = _rope(blk, tabs[0], tabs[1:], rolls)
            o_ref[0, g] = (blk * out_scale).astype(o_ref.dtype)


def project_heads(xb, row_ss, w_t, bsz, seq, *, col0, heads, tabs=None, out_scale=1.0,
                  transpose=False, out_dtype=BF16):
    m, k = xb.shape
    n = heads * LANES
    bm = min(1024, seq)
    bn = min(512, n)
    nt = seq // bm
    cb = col0 // bn
    rolls, tables = tabs if tabs is not None else ((), [])
    row_blk = lambda width: pl.BlockSpec((bm, width), lambda i, j: (i, 0))
    if transpose:
        out_spec = pl.BlockSpec((1, bn // LANES, LANES + ONES_ROWS, bm),
                                lambda i, j: (i // nt, j, 0, i % nt))
        out_shape = jax.ShapeDtypeStruct((bsz, heads, LANES + ONES_ROWS, seq), out_dtype)
    else:
        out_spec = pl.BlockSpec((1, bn // LANES, bm, LANES), lambda i, j: (i // nt, j, i % nt, 0))
        out_shape = jax.ShapeDtypeStruct((bsz, heads, seq, LANES), out_dtype)
    return pl.pallas_call(
        functools.partial(_proj_heads_kernel, d_norm=k, rolls=rolls, out_scale=out_scale,
                          transpose=transpose),
        grid=(m // bm, n // bn),
        in_specs=[row_blk(k), pl.BlockSpec((bn, k), lambda i, j: (cb + j, 0)), row_blk(LANES)]
        + [row_blk(LANES)] * len(tables),
        out_specs=out_spec,
        out_shape=out_shape,
        compiler_params=_cparams("parallel", "arbitrary"),
    )(xb, w_t, row_ss, *tables)


def _dsa_kernel(qi_ref, ki_ref, wi_ref, q_ref, k_ref, vt_ref, o_ref, key_scr, ot_scr,
                *, tq, tk, nq, topk, rep, idx_scale):
    qt = pl.program_id(1)
    q_chunk = (qt * tq + lax.broadcasted_iota(I32, (tk, tq), 1)) >> CHUNK_SHIFT
    k_iota = lax.broadcasted_iota(I32, (tk, tq), 0)
    groups = DSA_KV_HEADS

    for nb in range(nq * tq // tk):
        @pl.when((qt * tq) // tk == nb)
        def _(nb=nb):
            nblk = nb + 1
            kv_len = nblk * tk

            def idx_body(j, _):
                off = pl.multiple_of(j * tk, tk)
                kib = ki_ref[0, 0, pl.ds(off, tk), :]
                acc = jnp.zeros((tk, tq), F32)
                for hh in range(IDX_HEADS):
                    d = lax.dot_general(kib, qi_ref[0, hh], _NT, preferred_element_type=F32)
                    acc = acc + (wi_ref[0, hh:hh + 1, :] * idx_scale) * jnp.maximum(d, 0.0)
                bits = pltpu.bitcast(acc, I32)
                key = bits ^ ((bits >> 31) & 0x7FFFFFFF)
                valid = ((off + k_iota) >> CHUNK_SHIFT) <= q_chunk
                key_scr[pl.ds(off, tk), :] = jnp.where(valid, key, INT_MIN)
                return 0

            lax.fori_loop(0, nblk, idx_body, 0)

            def count_ge(cand):
                hit = jnp.where(key_scr[:kv_len, :] >= cand, 1, 0).astype(I32)
                c8 = jnp.sum(hit.reshape(kv_len // 8, 8, tq), axis=0)
                return jnp.sum(c8, axis=0, keepdims=True)

            thr = jnp.full((1, tq), INT_MIN, I32)
            thr = jnp.where(count_ge(jnp.zeros((1, tq), I32)) >= topk, 0, thr)

            def bit_body(i, t):
                cand = t + jnp.left_shift(jnp.int32(1), 30 - i)
                return jnp.where(count_ge(cand) >= topk, cand, t)

            thr = lax.fori_loop(0, 31, bit_body, thr)
            thr = jnp.maximum(thr, INT_MIN + 1)

            def group_body(g, _):
                qg = q_ref[0, pl.ds(g * rep, rep)].reshape(rep * tq, DSA_HEAD_DIM)
                s = lax.dot_general(k_ref[0, g, :kv_len, :], qg, _NT,
                                    preferred_element_type=F32)
                sel = key_scr[:kv_len, :] >= thr
                s = jnp.where(jnp.concatenate([sel] * rep, axis=1), s, NEG_BIG)
                p = jnp.exp2(s - jnp.max(s, axis=0, keepdims=True))
                acc = jnp.dot(vt_ref[0, g, :, :kv_len], p.astype(BF16),
                              preferred_element_type=F32)
                ot_scr[g] = acc[:DSA_HEAD_DIM] / acc[DSA_HEAD_DIM:DSA_HEAD_DIM + 1]
                return 0

            lax.fori_loop(0, groups, group_body, 0, unroll=4)

    for g in range(groups):
        for r in range(rep):
            hh = g * rep + r
            o_ref[:, hh * DSA_HEAD_DIM:(hh + 1) * DSA_HEAD_DIM] = (
                ot_scr[g, :, r * tq:(r + 1) * tq].T.astype(o_ref.dtype))


def dsa_mixer(xs, next_gain, tabs64, tabs128, bsz, seq, w_in, w_o):
    x_res, xb, ss = xs
    n = xb.shape[0]
    hq, g, hd = DSA_HEADS, DSA_KV_HEADS, DSA_HEAD_DIM
    rep = hq // g
    topk = min(DSA_TOPK_MAX, seq // 4)
    n_q, n_kv, n_qi = hq * hd, g * hd, IDX_HEADS * IDX_DIM
    n_qkv = n_q + 2 * n_kv
    main = n_qkv + n_qi

    proj = functools.partial(project_heads, xb, ss, w_in.T, bsz, seq)
    q_hm = proj(col0=0, heads=hq, tabs=tabs128, out_scale=float(hd) ** -0.5 * LOG2E)
    k_hm = proj(col0=n_q, heads=g, tabs=tabs128)
    vt = proj(col0=n_q + n_kv, heads=g, transpose=True)
    qi_hm = proj(col0=n_qkv, heads=IDX_HEADS, tabs=tabs64)
    ki_hm = proj(col0=main, heads=1, tabs=tabs64)
    wi = proj(col0=main + IDX_DIM, heads=1, out_dtype=F32)
    wi_t = wi[:, 0, :, :IDX_HEADS].transpose(0, 2, 1)
    tq, tk = 128, 256

    nq = seq // tq
    o = pl.pallas_call(
        functools.partial(_dsa_kernel, tq=tq, tk=tk, nq=nq, topk=topk, rep=rep,
                          idx_scale=float(IDX_HEADS) ** -0.5 * float(IDX_DIM) ** -0.5),
        grid=(bsz, nq),
        in_specs=[pl.BlockSpec((1, IDX_HEADS, tq, LANES), lambda b, i: (b, 0, i, 0)),
                  pl.BlockSpec((1, 1, seq, LANES), lambda b, i: (b, 0, 0, 0)),
                  pl.BlockSpec((1, IDX_HEADS, tq), lambda b, i: (b, 0, i)),
                  pl.BlockSpec((1, hq, tq, LANES), lambda b, i: (b, 0, i, 0)),
                  pl.BlockSpec((1, g, seq, LANES), lambda b, i: (b, 0, 0, 0)),
                  pl.BlockSpec((1, g, LANES + ONES_ROWS, seq), lambda b, i: (b, 0, 0, 0))],
        out_specs=pl.BlockSpec((tq, hq * hd), lambda b, i: (b * nq + i, 0)),
        out_shape=jax.ShapeDtypeStruct((n, hq * hd), BF16),
        scratch_shapes=[pltpu.VMEM((seq, tq), I32),
                        pltpu.VMEM((g, hd, rep * tq), F32)],
        compiler_params=_cparams("parallel", "arbitrary"),
    )(qi_hm, ki_hm, wi_t, q_hm, k_hm, vt)
    return matmul(o, w_o.astype(BF16), res=x_res, alpha=1.0, next_gain=next_gain)


SUBLANES = 8


def _rms(x, g):
    ms = jnp.mean(x * x, axis=-1, keepdims=True)
    return x * lax.rsqrt(ms + NORM_EPS) * g


def _shift_mix_kernel(x_ref, xp_ref, g_ref, mu_ref, w1_ref, a1_ref, g1_ref,
                      xr_ref, xk_ref, xv_ref, tw_ref, ta_ref, tg_ref):
    g = g_ref[...]
    h = _rms(x_ref[...], g)
    first_tile = pl.program_id(1) == 0
    last_prev = _rms(xp_ref[...], g)[SUBLANES - 1:SUBLANES, :]
    last_prev = jnp.where(first_tile, 0.0, last_prev)
    row = lax.broadcasted_iota(I32, h.shape, 0)
    h_prev = jnp.where(row == 0, last_prev, pltpu.roll(h, 1, 0))
    xx = h_prev - h

    def mix(i):
        return (h + xx * mu_ref[i:i + 1, :]).astype(BF16)

    xr_ref[...] = mix(0)
    xk_ref[...] = mix(2)
    xv_ref[...] = mix(3)
    dot = functools.partial(jnp.dot, preferred_element_type=F32)
    tw_ref[...] = jnp.tanh(dot(mix(1), w1_ref[...])).astype(tw_ref.dtype)
    ta_ref[...] = dot(mix(4), a1_ref[...]).astype(ta_ref.dtype)
    tg_ref[...] = _sigmoid(dot(mix(5), g1_ref[...])).astype(tg_ref.dtype)


def token_shift_mix(x, gain, mu, w1, a1, g1, bsz, seq, rows=256):
    n, d = x.shape
    nt = seq // rows
    per = rows // SUBLANES
    blk = lambda width: pl.BlockSpec((rows, width), lambda b, t: (b * nt + t, 0))
    whole = lambda a: pl.BlockSpec(a.shape, lambda b, t: (0, 0))
    prev = pl.BlockSpec((SUBLANES, d), lambda b, t: (jnp.maximum((b * nt + t) * per - 1, 0), 0))
    gain_row = gain.reshape(1, d).astype(F32)
    widths = [d, d, d, w1.shape[1], a1.shape[1], g1.shape[1]]
    return pl.pallas_call(
        _shift_mix_kernel,
        grid=(bsz, nt),
        in_specs=[blk(d), prev, whole(gain_row), whole(mu), whole(w1), whole(a1), whole(g1)],
        out_specs=[blk(w) for w in widths],
        out_shape=[jax.ShapeDtypeStruct((n, w), BF16) for w in widths],
        compiler_params=_cparams("parallel", "arbitrary"),
    )(x, x, gain_row, mu, w1, a1, g1)


def _split3(x):
    hi = x.astype(BF16)
    r1 = x - hi.astype(F32)
    mid = r1.astype(BF16)
    lo = (r1 - mid.astype(F32)).astype(BF16)
    return hi, mid, lo


def _dot3_left(w_bf16, x):
    hi, mid, lo = _split3(x)
    d = functools.partial(jnp.dot, preferred_element_type=F32)
    return d(w_bf16, hi) + d(w_bf16, mid) + d(w_bf16, lo)


def _wkv_kernel(r_ref, k_ref, v_ref, wl_ref, al_ref, g_ref,
                w0_ref, a0_ref, kk_ref, ka_ref, rk_ref, lnw_ref, lnb_ref,
                o_ref, state, *, L, pairs):
    c_idx = pl.program_id(2)

    @pl.when(c_idx == 0)
    def _():
        state[...] = jnp.zeros_like(state)

    hn = RWKV_HEAD
    hshift = hn.bit_length() - 1
    li = lax.broadcasted_iota(I32, (LANES, LANES), 0)
    lj = lax.broadcasted_iota(I32, (LANES, LANES), 1)
    same_head = (li >> hshift) == (lj >> hshift)
    seg_avg = jnp.where(same_head, 1.0 / hn, 0.0).astype(BF16)

    def seg_mean(x):
        lhs = jnp.concatenate([part[:, p * LANES:(p + 1) * LANES]
                               for part in _split3(x) for p in range(pairs)], axis=0)
        res = jnp.dot(lhs, seg_avg, preferred_element_type=F32)
        rows = pairs * L
        tot = res[:rows] + res[rows:2 * rows] + res[2 * rows:]
        return jnp.concatenate([tot[p * L:(p + 1) * L] for p in range(pairs)], axis=1)
    ti = lax.broadcasted_iota(I32, (L, L), 0)
    tj = lax.broadcasted_iota(I32, (L, L), 1)
    tril_incl = jnp.where(ti >= tj, 1.0, 0.0).astype(BF16)
    hr = lax.broadcasted_iota(I32, (L, LANES), 0)
    hc = lax.broadcasted_iota(I32, (L, LANES), 1)
    left = hc < hn
    strict = hr > (hc & (hn - 1))
    incl = hr >= (hc & (hn - 1))

    r_all = r_ref[...]
    k_all = k_ref[...]
    v_all = v_ref[...]
    z = w0_ref[...] + wl_ref[...]
    u = -z
    softplus = jnp.maximum(u, 0.0) + jnp.log(1.0 + jnp.exp(-jnp.abs(u)))
    w_log = -softplus - 0.5
    lw = -jnp.exp(w_log)
    a_sig = _sigmoid(a0_ref[...] + al_ref[...])
    k2_all = k_all * (1.0 + (a_sig - 1.0) * ka_ref[...])
    kk_all = k_all * kk_ref[...]
    csum = _dot3_left(tril_incl, lw)
    c_last = csum[L - 1:L, :]
    e_c = jnp.exp(csum)
    e_cprev = jnp.exp(csum - lw)
    e_neg = jnp.exp(-csum)
    e_rem = jnp.exp(c_last - csum)
    gam_all = jnp.exp(c_last)

    bdot = functools.partial(jnp.dot, preferred_element_type=F32)
    cat = jnp.concatenate
    prs = range(pairs)
    sls = [slice(p * LANES, (p + 1) * LANES) for p in prs]
    ksq = seg_mean(kk_all * kk_all)
    kk_n = kk_all / jnp.maximum(jnp.sqrt(hn * ksq), 1e-12)
    bvec_all = kk_n * a_sig
    at_all = -kk_n * e_cprev
    rt_all = r_all * e_c
    bt = [(bvec_all[:, sl] * e_neg[:, sl]).astype(BF16) for sl in sls]
    kt = [(k2_all[:, sl] * e_neg[:, sl]).astype(BF16) for sl in sls]
    a0 = [jnp.where(left, at_all[:, sl], 0.0).astype(BF16) for sl in sls]
    a1 = [jnp.where(left, 0.0, at_all[:, sl]).astype(BF16) for sl in sls]
    r0 = [jnp.where(left, rt_all[:, sl], 0.0).astype(BF16) for sl in sls]
    r1 = [jnp.where(left, 0.0, rt_all[:, sl]).astype(BF16) for sl in sls]
    vsw = [cat([jnp.where(left, 0.0, v_all[:, sl]), jnp.where(left, v_all[:, sl], 0.0)],
               axis=0).astype(BF16) for sl in sls]

    lhs = [cat([a0[p], r0[p], a1[p], r1[p]], axis=0) for p in prs]
    o01 = [lax.dot_general(lhs[p], cat([bt[p], kt[p]], axis=0), _NT,
                           preferred_element_type=F32) for p in prs]
    o0 = [o01[p][:2 * L] for p in prs]
    o1 = [pltpu.roll(o01[p][2 * L:], hn, 1) for p in prs]
    st = [state[p] for p in prs]
    ars = [lax.dot_general(lhs[p], st[p].astype(BF16), _NT,
                           preferred_element_type=F32) for p in prs]
    u0 = [jnp.where(strict, o0[p][:L], 0.0) for p in prs]
    u1 = [jnp.where(strict, o1[p][:L], 0.0) for p in prs]
    m_ab = [cat([jnp.where(left, u0[p], 0.0), jnp.where(left, 0.0, u1[p])], axis=0) for p in prs]
    m_ak = [cat([jnp.where(left, 0.0, u0[p]), jnp.where(left, u1[p], 0.0)], axis=0) for p in prs]
    xt = [(cat([ars[p][:L], ars[p][2 * L:3 * L]], axis=0)
           + bdot(m_ak[p].astype(BF16), vsw[p])).T for p in prs]
    mt = [m_ab[p].T for p in prs]
    steps = (L - 1).bit_length()
    for it in range(steps):
        mb = [mt[p].astype(BF16) for p in prs]
        if it + 1 < steps:
            prod = [bdot(cat([xt[p].astype(BF16), mb[p]], axis=0), mb[p]) for p in prs]
            xt = [xt[p] + prod[p][:LANES] for p in prs]
            mt = [prod[p][LANES:] for p in prs]
        else:
            xt = [xt[p] + bdot(xt[p].astype(BF16), mb[p]) for p in prs]
    x = [xt[p].T for p in prs]
    y0 = [jnp.where(incl, o0[p][L:], 0.0) for p in prs]
    y1 = [jnp.where(incl, o1[p][L:], 0.0) for p in prs]
    m_rb = [cat([jnp.where(left, y0[p], 0.0), jnp.where(left, 0.0, y1[p])], axis=0) for p in prs]
    m_rk = [cat([jnp.where(left, 0.0, y0[p]), jnp.where(left, y1[p], 0.0)], axis=0) for p in prs]
    ys = [cat([ars[p][L:2 * L], ars[p][3 * L:]], axis=0)
          + bdot(cat([m_rb[p], m_rk[p]], axis=1).astype(BF16),
                 cat([x[p].astype(BF16), vsw[p]], axis=0)) for p in prs]
    y_all = cat([ys[p][:L] + ys[p][L:] for p in prs], axis=1)
    sv_t = [cat([x[p][:L] + x[p][L:], v_all[:, sl]], axis=0).T.astype(BF16)
            for p, sl in enumerate(sls)]
    bh_all = (bvec_all * e_rem).astype(BF16)
    kh_all = (k2_all * e_rem).astype(BF16)
    bkh = [cat([bh_all[:, sl], kh_all[:, sl]], axis=0) for sl in sls]
    upd = [bdot(sv_t[p], bkh[p]) for p in prs]
    for p, sl in enumerate(sls):
        state[p] = st[p] * gam_all[:, sl] + jnp.where(same_head, upd[p], 0.0)

    yc = y_all - seg_mean(y_all)
    var = seg_mean(yc * yc)
    rk_sum = hn * seg_mean(r_all * k2_all * rk_ref[...])
    yn = yc * lax.rsqrt(var + GN_EPS) * lnw_ref[...] + lnb_ref[...]
    o_ref[...] = ((yn + rk_sum * v_all) * g_ref[...]).astype(o_ref.dtype)


def rwkv7_mixer(xs, gain, next_gain, bsz, seq, mu, w_r, w_k, w_v, w_o, w0, w1, w2, a0, a1, a2,
                g1, g2, k_k, k_a, r_k, lnx_w, lnx_b):
    x_res = xs[0]
    n, d = x_res.shape
    gpad = (-GATE_LORA) % LANES
    g1p = jnp.pad(g1, ((0, 0), (0, gpad))).astype(BF16)
    g2p = jnp.pad(g2, ((0, gpad), (0, 0))).astype(BF16)
    xr, xk, xv, tw, ta, tg = token_shift_mix(x_res, gain, mu, w1.astype(BF16), a1.astype(BF16),
                                             g1p, bsz, seq)
    r = matmul(xr, w_r)
    k = matmul(xk, w_k)
    v = matmul(xv, w_v)
    wl = matmul(tw, w2.astype(BF16))
    al = matmul(ta, a2.astype(BF16))
    gate = matmul(tg, g2p)

    L, pairs = CHUNK, 32
    nc = seq // L
    wide = LANES * pairs
    row = lambda a: a.reshape(1, d).astype(F32)
    dat = pl.BlockSpec((L, wide), lambda b, j, c: (b * nc + c, j))
    par = pl.BlockSpec((1, wide), lambda b, j, c: (0, j))
    yg = pl.pallas_call(
        functools.partial(_wkv_kernel, L=L, pairs=pairs),
        grid=(bsz, d // wide, nc),
        in_specs=[dat] * 6 + [par] * 7,
        out_specs=dat,
        out_shape=jax.ShapeDtypeStruct((n, d), BF16),
        scratch_shapes=[pltpu.VMEM((pairs, LANES, LANES), F32)],
        compiler_params=_cparams("parallel", "parallel", "arbitrary"),
    )(r, k, v, wl, al, gate, row(w0), row(a0), row(k_k), row(k_a), row(r_k), row(lnx_w), row(lnx_b))
    return matmul(yg, w_o.astype(BF16), res=x_res, alpha=1.0, next_gain=next_gain)


def kernel(x, positions, ffn_norm_0, ffn_w_gu_0, ffn_w_down_0, mix_norm_0, mla0_w_in, mla0_q_norm, mla0_w_uq, mla0_kv_norm, mla0_w_ukv, mla0_w_o, ffn_norm_1, ffn_w_gu_1, ffn_w_down_1, mix_norm_1, dsa1_w_in, dsa1_w_o, ffn_norm_2, ffn_w_gu_2, ffn_w_down_2, mix_norm_2, rwkv2_mu, rwkv2_w_r, rwkv2_w_k, rwkv2_w_v, rwkv2_w_o, rwkv2_w0, rwkv2_w1, rwkv2_w2, rwkv2_a0, rwkv2_a1, rwkv2_a2, rwkv2_g1, rwkv2_g2, rwkv2_k_k, rwkv2_k_a, rwkv2_r_k, rwkv2_lnx_w, rwkv2_lnx_b, ffn_norm_3, ffn_w_gu_3, ffn_w_down_3, mix_norm_3, mla3_w_in, mla3_q_norm, mla3_w_uq, mla3_kv_norm, mla3_w_ukv, mla3_w_o, final_norm):
    bsz, seq, d = x.shape
    n = bsz * seq
    pos_col = positions.reshape(n, 1)
    tabs_mla = rope_tables(pos_col, "spread64")
    tabs128 = rope_tables(pos_col, "half128")
    tabs64 = rope_tables(pos_col, "lead64")

    def mla(xs, nxt, w_in, q_norm, w_uq, kv_norm, w_ukv, w_o):
        return mla_mixer(xs, nxt, tabs_mla, bsz, seq, w_in, q_norm, w_uq, kv_norm, w_ukv, w_o)

    def rwkv(xs, gain, nxt):
        return rwkv7_mixer(xs, gain, nxt, bsz, seq, rwkv2_mu, rwkv2_w_r, rwkv2_w_k, rwkv2_w_v,
                           rwkv2_w_o, rwkv2_w0, rwkv2_w1, rwkv2_w2, rwkv2_a0, rwkv2_a1, rwkv2_a2,
                           rwkv2_g1, rwkv2_g2, rwkv2_k_k, rwkv2_k_a,
                           rwkv2_r_k.reshape(-1), rwkv2_lnx_w, rwkv2_lnx_b)

    mixers = [
        lambda xs, gn, nxt: mla(xs, nxt, mla0_w_in, mla0_q_norm, mla0_w_uq, mla0_kv_norm,
                                mla0_w_ukv, mla0_w_o),
        lambda xs, gn, nxt: dsa_mixer(xs, nxt, tabs64, tabs128, bsz, seq, dsa1_w_in, dsa1_w_o),
        rwkv,
        lambda xs, gn, nxt: mla(xs, nxt, mla3_w_in, mla3_q_norm, mla3_w_uq, mla3_kv_norm,
                                mla3_w_ukv, mla3_w_o),
    ]
    ffns = [(ffn_norm_0, ffn_w_gu_0, ffn_w_down_0, mix_norm_0),
            (ffn_norm_1, ffn_w_gu_1, ffn_w_down_1, mix_norm_1),
            (ffn_norm_2, ffn_w_gu_2, ffn_w_down_2, mix_norm_2),
            (ffn_norm_3, ffn_w_gu_3, ffn_w_down_3, mix_norm_3)]
    xs = norm_prep(x.reshape(n, d), ffn_norm_0[0])
    for i in range(4):
        f_norm, w_gu, w_down, m_norm = ffns[i]
        after_layer = ffns[i + 1][0][0] if i + 1 < len(ffns) else final_norm
        xs = ffn_half_step(xs, w_gu, w_down, 0, m_norm)
        xs = mixers[i](xs, m_norm, f_norm[1])
        xs = ffn_half_step(xs, w_gu, w_down, 1, after_layer)
    return rmsnorm(xs[0], final_norm, out_dtype=F32).reshape(bsz, seq, d)
```

```python
import functools

import numpy as np
import jax
import jax.numpy as jnp
from jax import lax
from jax.experimental import pallas as pl
from jax.experimental.pallas import tpu as pltpu

F32 = jnp.float32
BF16 = jnp.bfloat16
I32 = jnp.int32

CHUNK = 64
CHUNK_SHIFT = CHUNK.bit_length() - 1
ROPE_THETA = 10000.0
NORM_EPS = 1e-6
FFN_RES = 0.5
MLA_HEADS = 32
MLA_Q_LORA = 1024
MLA_KV_LORA = 512
MLA_NOPE = 128
MLA_ROPE = 64
MLA_V = 128
DSA_HEADS = 32
DSA_KV_HEADS = 8
DSA_HEAD_DIM = 128
IDX_HEADS = 32
IDX_DIM = 128
IDX_ROPE = 64
DSA_TOPK_MAX = 256
RWKV_HEAD = 64
GATE_LORA = 480
GN_EPS = 64e-5

LANES = 128
VMEM_LIMIT = 56 * 1024 * 1024
INT_MIN = -(2 ** 31)
NEG_BIG = -1e30
LOG2E = 1.4426950408889634

_NT = (((1,), (1,)), ((), ()))


def _cparams(*sem):
    return pltpu.CompilerParams(dimension_semantics=sem, vmem_limit_bytes=VMEM_LIMIT)


def _sigmoid(x):
    return 1.0 / (1.0 + jnp.exp(-x))


def _rmsnorm_kernel(x_ref, g_ref, o_ref):
    x = x_ref[...].astype(F32)
    ms = jnp.mean(x * x, axis=-1, keepdims=True)
    o_ref[...] = (x * lax.rsqrt(ms + NORM_EPS) * g_ref[...]).astype(o_ref.dtype)


def rmsnorm(x, g, *, width=None, col_block=0, out_dtype=BF16, rows=256):
    n = x.shape[0]
    w = x.shape[1] if width is None else width
    return pl.pallas_call(
        _rmsnorm_kernel,
        grid=(n // rows,),
        in_specs=[pl.BlockSpec((rows, w), lambda i: (i, col_block)),
                  pl.BlockSpec((1, w), lambda i: (0, 0))],
        out_specs=pl.BlockSpec((rows, w), lambda i: (i, 0)),
        out_shape=jax.ShapeDtypeStruct((n, w), out_dtype),
        compiler_params=_cparams("parallel"),
    )(x, g.reshape(1, w).astype(F32))


def _row_rstd(ss_ref, d_norm):
    return lax.rsqrt(jnp.sum(ss_ref[...], axis=-1, keepdims=True) * (1.0 / d_norm) + NORM_EPS)


def _lane_partial_sumsq(x):
    sq = x * x
    part = sq[:, :LANES]
    for c in range(1, x.shape[1] // LANES):
        part = part + sq[:, c * LANES:(c + 1) * LANES]
    return part


def _mm_kernel(*refs, act, alpha, has_res, d_norm, emit_norm):
    refs = list(refs)
    a_ref, w_ref = refs[:2]
    del refs[:2]
    ss_ref = refs.pop(0) if d_norm else None
    r_ref = refs.pop(0) if has_res else None
    gn_ref = refs.pop(0) if emit_norm else None
    o_ref = refs.pop(0)
    acc = jnp.dot(a_ref[...], w_ref[...].astype(BF16), preferred_element_type=F32)
    if d_norm:
        acc = acc * _row_rstd(ss_ref, d_norm)
    if act == "tanh":
        acc = jnp.tanh(acc)
    elif act == "sigmoid":
        acc = _sigmoid(acc)
    if has_res:
        acc = r_ref[...] + alpha * acc
    o_ref[...] = acc.astype(o_ref.dtype)
    if emit_norm:
        xb_ref, sso_ref = refs
        xb_ref[...] = (acc * gn_ref[...]).astype(BF16)
        part = _lane_partial_sumsq(acc)

        @pl.when(pl.program_id(1) == 0)
        def _():
            sso_ref[...] = part

        @pl.when(pl.program_id(1) != 0)
        def _():
            sso_ref[...] += part


def _mm_tiles(m, k, n, w_itemsize, n_out_tiles):
    bm = min(1024, m)
    a_bytes = 2 * bm * k * 2
    for bn in (1024, 512, 256, 128):
        w_bytes = bn * k * (2 * w_itemsize + (2 if w_itemsize > 2 else 0))
        o_bytes = bm * bn * 4 * (2 * n_out_tiles + 1)
        if n % bn == 0 and a_bytes + w_bytes + o_bytes <= VMEM_LIMIT * 0.85:
            return bm, bn
    return bm, n


def matmul(a, w, *, out_dtype=F32, act=None, res=None, alpha=1.0, bm=None, bn=None, w_index=None,
           row_ss=None, next_gain=None):
    m, k = a.shape
    n = w.shape[-1]
    emit_norm = next_gain is not None
    tbm, tbn = _mm_tiles(m, k, n, w.dtype.itemsize, 1 + (res is not None) + emit_norm)
    bm = tbm if bm is None else bm
    bn = tbn if bn is None else bn
    if w.ndim == 3:
        w_spec = pl.BlockSpec((None, k, bn), lambda i, j: (w_index, 0, j))
    else:
        w_spec = pl.BlockSpec((k, bn), lambda i, j: (0, j))
    row_spec = pl.BlockSpec((bm, LANES), lambda i, j: (i, 0))
    tile_spec = pl.BlockSpec((bm, bn), lambda i, j: (i, j))
    in_specs = [pl.BlockSpec((bm, k), lambda i, j: (i, 0)), w_spec]
    args = [a, w]
    if row_ss is not None:
        in_specs.append(row_spec)
        args.append(row_ss)
    if res is not None:
        in_specs.append(tile_spec)
        args.append(res)
    out_specs, out_shape = tile_spec, jax.ShapeDtypeStruct((m, n), out_dtype)
    if emit_norm:
        in_specs.append(pl.BlockSpec((1, bn), lambda i, j: (0, j)))
        args.append(next_gain.reshape(1, n).astype(F32))
        out_specs = [tile_spec, tile_spec, row_spec]
        out_shape = [out_shape, jax.ShapeDtypeStruct((m, n), BF16),
                     jax.ShapeDtypeStruct((m, LANES), F32)]
    return pl.pallas_call(
        functools.partial(_mm_kernel, act=act, alpha=alpha, has_res=res is not None,
                          d_norm=k if row_ss is not None else 0, emit_norm=emit_norm),
        grid=(m // bm, n // bn),
        in_specs=in_specs,
        out_specs=out_specs,
        out_shape=out_shape,
        compiler_params=_cparams("parallel", "arbitrary"),
    )(*args)


def _swiglu_kernel(a_ref, ss_ref, wg_ref, wu_ref, o_ref, *, d_norm):
    a = a_ref[...]
    rstd = _row_rstd(ss_ref, d_norm)
    g = jnp.dot(a, wg_ref[...].astype(BF16), preferred_element_type=F32) * rstd
    u = jnp.dot(a, wu_ref[...].astype(BF16), preferred_element_type=F32) * rstd
    o_ref[...] = (g * _sigmoid(g) * u).astype(o_ref.dtype)


def swiglu_up(a, row_ss, w_gu, half, *, bm=2048, bf=256):
    m, k = a.shape
    bm = min(bm, m)
    f = w_gu.shape[-1] // 2
    nf = f // bf
    return pl.pallas_call(
        functools.partial(_swiglu_kernel, d_norm=k),
        grid=(m // bm, nf),
        in_specs=[pl.BlockSpec((bm, k), lambda i, j: (i, 0), pipeline_mode=pl.Buffered(1)),
                  pl.BlockSpec((bm, LANES), lambda i, j: (i, 0)),
                  pl.BlockSpec((None, k, bf), lambda i, j: (half, 0, j)),
                  pl.BlockSpec((None, k, bf), lambda i, j: (half, 0, j + nf))],
        out_specs=pl.BlockSpec((bm, bf), lambda i, j: (i, j)),
        out_shape=jax.ShapeDtypeStruct((m, f), BF16),
        compiler_params=_cparams("parallel", "arbitrary"),
    )(a, row_ss, w_gu, w_gu)


def ffn_half_step(xs, w_gu, w_down, half, next_gain):
    x, xb, ss = xs
    act = swiglu_up(xb, ss, w_gu, half)
    return matmul(act, w_down, res=x, alpha=FFN_RES, bm=min(1024, x.shape[0]), bn=256,
                  w_index=half, next_gain=next_gain)


def _norm_prep_kernel(x_ref, g_ref, xb_ref, ss_ref):
    x = x_ref[...]
    xb_ref[...] = (x * g_ref[...]).astype(BF16)
    ss_ref[...] = _lane_partial_sumsq(x)


def norm_prep(x, gain, rows=256):
    n, d = x.shape
    xb, ss = pl.pallas_call(
        _norm_prep_kernel,
        grid=(n // rows,),
        in_specs=[pl.BlockSpec((rows, d), lambda i: (i, 0)),
                  pl.BlockSpec((1, d), lambda i: (0, 0))],
        out_specs=[pl.BlockSpec((rows, d), lambda i: (i, 0)),
                   pl.BlockSpec((rows, LANES), lambda i: (i, 0))],
        out_shape=[jax.ShapeDtypeStruct((n, d), BF16), jax.ShapeDtypeStruct((n, LANES), F32)],
        compiler_params=_cparams("parallel"),
    )(x, gain.reshape(1, d).astype(F32))
    return x, xb, ss


def _rope_lane_consts(layout):
    lane = np.arange(LANES)
    if layout == "half128":
        rot_dim, freq_idx, active = LANES, lane % 64, np.ones(LANES, bool)
        rolls = {64: np.where(lane < 64, -1.0, 1.0)}
    elif layout == "spread64":
        rot_dim, freq_idx, active = 64, lane % 32, (lane % 64) < 32
        rolls = {64: np.where(active, np.where(lane < 64, -1.0, 1.0), 0.0)}
    else:
        rot_dim, freq_idx, active = 64, lane % 32, lane < 64
        rolls = {32: np.where((lane >= 32) & (lane < 64), 1.0, 0.0),
                 96: np.where(lane < 32, -1.0, 0.0)}
    inv = np.where(active, ROPE_THETA ** (-(2.0 * freq_idx) / rot_dim), 0.0)
    signs = np.stack([rolls[k] for k in sorted(rolls)])
    return jnp.asarray(inv, F32).reshape(1, LANES), jnp.asarray(signs, F32), tuple(sorted(rolls))


def _rope_tab_kernel(pos_ref, inv_ref, sign_ref, c_ref, *s_refs):
    ang = pos_ref[...].astype(F32) * inv_ref[...]
    c_ref[...] = jnp.cos(ang)
    sin = jnp.sin(ang)
    for k, s_ref in enumerate(s_refs):
        s_ref[...] = sin * sign_ref[k:k + 1, :]


def rope_tables(pos_col, layout, rows=512):
    n = pos_col.shape[0]
    inv, signs, rolls = _rope_lane_consts(layout)
    tab = jax.ShapeDtypeStruct((n, LANES), F32)
    tabs = pl.pallas_call(
        _rope_tab_kernel,
        grid=(n // rows,),
        in_specs=[pl.BlockSpec((rows, 1), lambda i: (i, 0)),
                  pl.BlockSpec((1, LANES), lambda i: (0, 0)),
                  pl.BlockSpec((len(rolls), LANES), lambda i: (0, 0))],
        out_specs=[pl.BlockSpec((rows, LANES), lambda i: (i, 0))] * (1 + len(rolls)),
        out_shape=[tab] * (1 + len(rolls)),
        compiler_params=_cparams("parallel"),
    )(pos_col, inv, signs)
    return rolls, tabs


def _rope(x, c, s, rolls=(64,)):
    s = s if isinstance(s, (list, tuple)) else [s]
    y = x * c
    for shift, sk in zip(rolls, s):
        y = y + pltpu.roll(x, shift, 1) * sk
    return y


def _mla_attn_kernel(q_ref, kn_ref, v_ref, kr_ref, cq_ref, sq_ref, ck_ref, sk_ref,
                     o_ref, kfull, vfull, krs, *, tq, nq, hb, scale):
    hg = pl.program_id(1)
    qi = pl.program_id(2)

    @pl.when(jnp.logical_and(hg == 0, qi == 0))
    def _():
        krs[...] = _rope(kr_ref[...], ck_ref[...], sk_ref[...]).astype(BF16)

    @pl.when(qi == 0)
    def _():
        for hd in range(hb):
            kfull[hd, :, :LANES] = kn_ref[:, hd * LANES:(hd + 1) * LANES]
            kfull[hd, :, LANES:] = krs[...]
            vfull[hd, :, :LANES] = v_ref[:, hd * LANES:(hd + 1) * LANES]
            vfull[hd, :, LANES:] = jnp.ones((vfull.shape[1], LANES), BF16)

    cq = cq_ref[...]
    sq = sq_ref[...]
    diag_ok = ((lax.broadcasted_iota(I32, (tq, tq), 1) >> CHUNK_SHIFT)
               <= (lax.broadcasted_iota(I32, (tq, tq), 0) >> CHUNK_SHIFT))

    for i in range(nq):
        @pl.when(qi == i)
        def _(i=i):
            kv_len = (i + 1) * tq
            for h0 in range(0, hb, 2):
                pair = (h0, h0 + 1)
                qs = []
                for hd in pair:
                    base = hd * 2 * LANES
                    qr = _rope(q_ref[:, base + LANES:base + 2 * LANES], cq, sq)
                    q = jnp.concatenate([q_ref[:, base:base + LANES], qr], axis=1)
                    qs.append((q * (scale * LOG2E)).astype(BF16))
                ss = [lax.dot_general(q, kfull[hd, :kv_len, :], _NT, preferred_element_type=F32)
                      for q, hd in zip(qs, pair)]
                ps = []
                for s in ss:
                    s_diag = jnp.where(diag_ok, s[:, kv_len - tq:], NEG_BIG)
                    s = s_diag if i == 0 else jnp.concatenate([s[:, :kv_len - tq], s_diag], axis=1)
                    ps.append(jnp.exp2(s - jnp.max(s, axis=-1, keepdims=True)).astype(BF16))
                ols = [jnp.dot(p, vfull[hd, :kv_len, :], preferred_element_type=F32)
                       for p, hd in zip(ps, pair)]
                o_ref[:, h0 * LANES:(h0 + 2) * LANES] = jnp.concatenate(
                    [(ol[:, :LANES] / ol[:, LANES:LANES + 1]).astype(o_ref.dtype) for ol in ols],
                    axis=1)


def mla_mixer(xs, next_gain, tabs64, bsz, seq, w_in, q_norm, w_uq, kv_norm, w_ukv, w_o):
    x_res, xb, ss = xs
    n = xb.shape[0]
    hq = MLA_HEADS
    lane = np.arange(LANES)
    rope_on = (lane % 64) < (MLA_ROPE // 2)
    rope_src = np.where(rope_on, (lane // 64) * (MLA_ROPE // 2) + lane % (MLA_ROPE // 2), 0)
    base = MLA_Q_LORA + MLA_KV_LORA
    kr_cols = w_in[:, base + rope_src] * jnp.asarray(rope_on, F32)
    w_in_p = jnp.concatenate([w_in[:, :base], kr_cols], axis=1).astype(BF16)
    hd = MLA_NOPE + MLA_ROPE
    idx = np.concatenate([np.concatenate([hh * hd + np.arange(MLA_NOPE),
                                          hh * hd + MLA_NOPE + rope_src]) for hh in range(hq)])
    msk = np.tile(np.concatenate([np.ones(MLA_NOPE, bool), rope_on]), hq)
    w_uq_p = (w_uq[:, idx] * jnp.asarray(msk, F32)).astype(BF16)
    w_ukv_p = w_ukv.reshape(MLA_KV_LORA, hq, 2, MLA_NOPE).transpose(0, 2, 1, 3)
    w_ukv_p = w_ukv_p.reshape(MLA_KV_LORA, 2 * hq * MLA_NOPE).astype(BF16)

    lat = matmul(xb, w_in_p, bm=512, bn=w_in_p.shape[1], row_ss=ss)
    cq = rmsnorm(lat, q_norm, width=MLA_Q_LORA, col_block=0)
    ckv = rmsnorm(lat, kv_norm, width=MLA_KV_LORA, col_block=MLA_Q_LORA // MLA_KV_LORA)
    q = matmul(cq, w_uq_p)
    kv = matmul(ckv, w_ukv_p, out_dtype=BF16)

    tq, hb = 256, 8
    nq = seq // tq
    ng = hq // hb
    c64, s64 = tabs64[1]
    kr_blk = base // LANES
    o = pl.pallas_call(
        functools.partial(_mla_attn_kernel, tq=tq, nq=nq, hb=hb, scale=float(hd) ** -0.5),
        grid=(bsz, ng, nq),
        in_specs=[pl.BlockSpec((tq, 2 * LANES * hb), lambda b, hh, i: (b * nq + i, hh)),
                  pl.BlockSpec((seq, LANES * hb), lambda b, hh, i: (b, hh)),
                  pl.BlockSpec((seq, LANES * hb), lambda b, hh, i: (b, ng + hh)),
                  pl.BlockSpec((seq, LANES), lambda b, hh, i: (b, kr_blk)),
                  pl.BlockSpec((tq, LANES), lambda b, hh, i: (b * nq + i, 0)),
                  pl.BlockSpec((tq, LANES), lambda b, hh, i: (b * nq + i, 0)),
                  pl.BlockSpec((seq, LANES), lambda b, hh, i: (b, 0)),
                  pl.BlockSpec((seq, LANES), lambda b, hh, i: (b, 0))],
        out_specs=pl.BlockSpec((tq, LANES * hb), lambda b, hh, i: (b * nq + i, hh)),
        out_shape=jax.ShapeDtypeStruct((n, hq * MLA_V), BF16),
        scratch_shapes=[pltpu.VMEM((hb, seq, 2 * LANES), BF16),
                        pltpu.VMEM((hb, seq, 2 * LANES), BF16),
                        pltpu.VMEM((seq, LANES), BF16)],
        compiler_params=_cparams("arbitrary", "arbitrary", "arbitrary"),
    )(q, kv, kv, lat, c64, s64, c64, s64)
    return matmul(o, w_o.astype(BF16), res=x_res, alpha=1.0, next_gain=next_gain)


ONES_ROWS = 16


def _proj_heads_kernel(a_ref, w_ref, ss_ref, *rest, d_norm, rolls, out_scale, transpose):
    tabs, o_ref = [t[...] for t in rest[:-1]], rest[-1]
    acc = lax.dot_general(a_ref[...], w_ref[...].astype(BF16), _NT, preferred_element_type=F32)
    acc = acc * _row_rstd(ss_ref, d_norm)
    for g in range(acc.shape[1] // LANES):
        blk = acc[:, g * LANES:(g + 1) * LANES]
        if transpose:
            o_ref[0, g, :LANES, :] = blk.T.astype(o_ref.dtype)
            o_ref[0, g, LANES:, :] = jnp.ones((ONES_ROWS, blk.shape[0]), o_ref.dtype)
        else:
            if tabs:
                blk = _rope(blk, tabs[0], tabs[1:], rolls)
            o_ref[0, g] = (blk * out_scale).astype(o_ref.dtype)


def project_heads(xb, row_ss, w_t, bsz, seq, *, col0, heads, tabs=None, out_scale=1.0,
                  transpose=False, out_dtype=BF16):
    m, k = xb.shape
    n = heads * LANES
    bm = min(1024, seq)
    bn = min(512, n)
    nt = seq // bm
    cb = col0 // bn
    rolls, tables = tabs if tabs is not None else ((), [])
    row_blk = lambda width: pl.BlockSpec((bm, width), lambda i, j: (i, 0))
    if transpose:
        out_spec = pl.BlockSpec((1, bn // LANES, LANES + ONES_ROWS, bm),
                                lambda i, j: (i // nt, j, 0, i % nt))
        out_shape = jax.ShapeDtypeStruct((bsz, heads, LANES + ONES_ROWS, seq), out_dtype)
    else:
        out_spec = pl.BlockSpec((1, bn // LANES, bm, LANES), lambda i, j: (i // nt, j, i % nt, 0))
        out_shape = jax.ShapeDtypeStruct((bsz, heads, seq, LANES), out_dtype)
    return pl.pallas_call(
        functools.partial(_proj_heads_kernel, d_norm=k, rolls=rolls, out_scale=out_scale,
                          transpose=transpose),
        grid=(m // bm, n // bn),
        in_specs=[row_blk(k), pl.BlockSpec((bn, k), lambda i, j: (cb + j, 0)), row_blk(LANES)]
        + [row_blk(LANES)] * len(tables),
        out_specs=out_spec,
        out_shape=out_shape,
        compiler_params=_cparams("parallel", "arbitrary"),
    )(xb, w_t, row_ss, *tables)


def _dsa_kernel(qi_ref, ki_ref, wi_ref, q_ref, k_ref, vt_ref, o_ref, key_scr, ot_scr,
                *, tq, tk, nq, topk, rep, idx_scale):
    qt = pl.program_id(1)
    q_chunk = (qt * tq + lax.broadcasted_iota(I32, (tk, tq), 1)) >> CHUNK_SHIFT
    k_iota = lax.broadcasted_iota(I32, (tk, tq), 0)
    groups = DSA_KV_HEADS

    for nb in range(nq * tq // tk):
        @pl.when((qt * tq) // tk == nb)
        def _(nb=nb):
            nblk = nb + 1
            kv_len = nblk * tk

            def idx_body(j, _):
                off = pl.multiple_of(j * tk, tk)
                kib = ki_ref[0, 0, pl.ds(off, tk), :]
                acc = jnp.zeros((tk, tq), F32)
                for hh in range(IDX_HEADS):
                    d = lax.dot_general(kib, qi_ref[0, hh], _NT, preferred_element_type=F32)
                    acc = acc + (wi_ref[0, hh:hh + 1, :] * idx_scale) * jnp.maximum(d, 0.0)
                bits = pltpu.bitcast(acc, I32)
                key = bits ^ ((bits >> 31) & 0x7FFFFFFF)
                valid = ((off + k_iota) >> CHUNK_SHIFT) <= q_chunk
                key_scr[pl.ds(off, tk), :] = jnp.where(valid, key, INT_MIN)
                return 0

            lax.fori_loop(0, nblk, idx_body, 0)

            def count_ge(cand):
                hit = jnp.where(key_scr[:kv_len, :] >= cand, 1, 0).astype(I32)
                c8 = jnp.sum(hit.reshape(kv_len // 8, 8, tq), axis=0)
                return jnp.sum(c8, axis=0, keepdims=True)

            thr = jnp.full((1, tq), INT_MIN, I32)
            thr = jnp.where(count_ge(jnp.zeros((1, tq), I32)) >= topk, 0, thr)

            def bit_body(i, t):
                cand = t + jnp.left_shift(jnp.int32(1), 30 - i)
                return jnp.where(count_ge(cand) >= topk, cand, t)

            thr = lax.fori_loop(0, 31, bit_body, thr)
            thr = jnp.maximum(thr, INT_MIN + 1)

            def group_body(g, _):
                qg = q_ref[0, pl.ds(g * rep, rep)].reshape(rep * tq, DSA_HEAD_DIM)
                s = lax.dot_general(k_ref[0, g, :kv_len, :], qg, _NT,
                                    preferred_element_type=F32)
                sel = key_scr[:kv_len, :] >= thr
                s = jnp.where(jnp.concatenate([sel] * rep, axis=1), s, NEG_BIG)
                p = jnp.exp2(s - jnp.max(s, axis=0, keepdims=True))
                acc = jnp.dot(vt_ref[0, g, :, :kv_len], p.astype(BF16),
                              preferred_element_type=F32)
                ot_scr[g] = acc[:DSA_HEAD_DIM] / acc[DSA_HEAD_DIM:DSA_HEAD_DIM + 1]
                return 0

            lax.fori_loop(0, groups, group_body, 0, unroll=2)

    for g in range(groups):
        for r in range(rep):
            hh = g * rep + r
            o_ref[:, hh * DSA_HEAD_DIM:(hh + 1) * DSA_HEAD_DIM] = (
                ot_scr[g, :, r * tq:(r + 1) * tq].T.astype(o_ref.dtype))


def dsa_mixer(xs, next_gain, tabs64, tabs128, bsz, seq, w_in, w_o):
    x_res, xb, ss = xs
    n = xb.shape[0]
    hq, g, hd = DSA_HEADS, DSA_KV_HEADS, DSA_HEAD_DIM
    rep = hq // g
    topk = min(DSA_TOPK_MAX, seq // 4)
    n_q, n_kv, n_qi = hq * hd, g * hd, IDX_HEADS * IDX_DIM
    n_qkv = n_q + 2 * n_kv
    main = n_qkv + n_qi

    proj = functools.partial(project_heads, xb, ss, w_in.T, bsz, seq)
    q_hm = proj(col0=0, heads=hq, tabs=tabs128, out_scale=float(hd) ** -0.5 * LOG2E)
    k_hm = proj(col0=n_q, heads=g, tabs=tabs128)
    vt = proj(col0=n_q + n_kv, heads=g, transpose=True)
    qi_hm = proj(col0=n_qkv, heads=IDX_HEADS, tabs=tabs64)
    ki_hm = proj(col0=main, heads=1, tabs=tabs64)
    wi = proj(col0=main + IDX_DIM, heads=1, out_dtype=F32)
    wi_t = wi[:, 0, :, :IDX_HEADS].transpose(0, 2, 1)
    tq, tk = 128, 256

    nq = seq // tq
    o = pl.pallas_call(
        functools.partial(_dsa_kernel, tq=tq, tk=tk, nq=nq, topk=topk, rep=rep,
                          idx_scale=float(IDX_HEADS) ** -0.5 * float(IDX_DIM) ** -0.5),
        grid=(bsz, nq),
        in_specs=[pl.BlockSpec((1, IDX_HEADS, tq, LANES), lambda b, i: (b, 0, i, 0)),
                  pl.BlockSpec((1, 1, seq, LANES), lambda b, i: (b, 0, 0, 0)),
                  pl.BlockSpec((1, IDX_HEADS, tq), lambda b, i: (b, 0, i)),
                  pl.BlockSpec((1, hq, tq, LANES), lambda b, i: (b, 0, i, 0)),
                  pl.BlockSpec((1, g, seq, LANES), lambda b, i: (b, 0, 0, 0)),
                  pl.BlockSpec((1, g, LANES + ONES_ROWS, seq), lambda b, i: (b, 0, 0, 0))],
        out_specs=pl.BlockSpec((tq, hq * hd), lambda b, i: (b * nq + i, 0)),
        out_shape=jax.ShapeDtypeStruct((n, hq * hd), BF16),
        scratch_shapes=[pltpu.VMEM((seq, tq), I32),
                        pltpu.VMEM((g, hd, rep * tq), F32)],
        compiler_params=_cparams("parallel", "arbitrary"),
    )(qi_hm, ki_hm, wi_t, q_hm, k_hm, vt)
    return matmul(o, w_o.astype(BF16), res=x_res, alpha=1.0, next_gain=next_gain)


SUBLANES = 8


def _rms(x, g):
    ms = jnp.mean(x * x, axis=-1, keepdims=True)
    return x * lax.rsqrt(ms + NORM_EPS) * g


def _shift_mix_kernel(x_ref, xp_ref, g_ref, mu_ref, w1_ref, a1_ref, g1_ref,
                      xr_ref, xk_ref, xv_ref, tw_ref, ta_ref, tg_ref):
    g = g_ref[...]
    h = _rms(x_ref[...], g)
    first_tile = pl.program_id(1) == 0
    last_prev = _rms(xp_ref[...], g)[SUBLANES - 1:SUBLANES, :]
    last_prev = jnp.where(first_tile, 0.0, last_prev)
    row = lax.broadcasted_iota(I32, h.shape, 0)
    h_prev = jnp.where(row == 0, last_prev, pltpu.roll(h, 1, 0))
    xx = h_prev - h

    def mix(i):
        return (h + xx * mu_ref[i:i + 1, :]).astype(BF16)

    xr_ref[...] = mix(0)
    xk_ref[...] = mix(2)
    xv_ref[...] = mix(3)
    dot = functools.partial(jnp.dot, preferred_element_type=F32)
    tw_ref[...] = jnp.tanh(dot(mix(1), w1_ref[...])).astype(tw_ref.dtype)
    ta_ref[...] = dot(mix(4), a1_ref[...]).astype(ta_ref.dtype)
    tg_ref[...] = _sigmoid(dot(mix(5), g1_ref[...])).astype(tg_ref.dtype)


def token_shift_mix(x, gain, mu, w1, a1, g1, bsz, seq, rows=256):
    n, d = x.shape
    nt = seq // rows
    per = rows // SUBLANES
    blk = lambda width: pl.BlockSpec((rows, width), lambda b, t: (b * nt + t, 0))
    whole = lambda a: pl.BlockSpec(a.shape, lambda b, t: (0, 0))
    prev = pl.BlockSpec((SUBLANES, d), lambda b, t: (jnp.maximum((b * nt + t) * per - 1, 0), 0))
    gain_row = gain.reshape(1, d).astype(F32)
    widths = [d, d, d, w1.shape[1], a1.shape[1], g1.shape[1]]
    return pl.pallas_call(
        _shift_mix_kernel,
        grid=(bsz, nt),
        in_specs=[blk(d), prev, whole(gain_row), whole(mu), whole(w1), whole(a1), whole(g1)],
        out_specs=[blk(w) for w in widths],
        out_shape=[jax.ShapeDtypeStruct((n, w), BF16) for w in widths],
        compiler_params=_cparams("parallel", "arbitrary"),
    )(x, x, gain_row, mu, w1, a1, g1)


def _split3(x):
    hi = x.astype(BF16)
    r1 = x - hi.astype(F32)
    mid = r1.astype(BF16)
    lo = (r1 - mid.astype(F32)).astype(BF16)
    return hi, mid, lo


def _dot3_right(x, w_bf16):
    hi, mid, lo = _split3(x)
    d = functools.partial(jnp.dot, preferred_element_type=F32)
    return d(hi, w_bf16) + d(mid, w_bf16) + d(lo, w_bf16)


def _dot3_left(w_bf16, x):
    hi, mid, lo = _split3(x)
    d = functools.partial(jnp.dot, preferred_element_type=F32)
    return d(w_bf16, hi) + d(w_bf16, mid) + d(w_bf16, lo)


def _wkv_kernel(r_ref, k_ref, v_ref, wl_ref, al_ref, g_ref,
                w0_ref, a0_ref, kk_ref, ka_ref, rk_ref, lnw_ref, lnb_ref,
                o_ref, state, *, L, pairs):
    c_idx = pl.program_id(2)

    @pl.when(c_idx == 0)
    def _():
        state[...] = jnp.zeros_like(state)

    hn = RWKV_HEAD
    hshift = hn.bit_length() - 1
    li = lax.broadcasted_iota(I32, (LANES, LANES), 0)
    lj = lax.broadcasted_iota(I32, (LANES, LANES), 1)
    same_head = (li >> hshift) == (lj >> hshift)
    seg_avg = jnp.where(same_head, 1.0 / hn, 0.0).astype(BF16)

    def seg_mean(x):
        lhs = jnp.concatenate([part[:, p * LANES:(p + 1) * LANES]
                               for part in _split3(x) for p in range(pairs)], axis=0)
        res = jnp.dot(lhs, seg_avg, preferred_element_type=F32)
        rows = pairs * L
        tot = res[:rows] + res[rows:2 * rows] + res[2 * rows:]
        return jnp.concatenate([tot[p * L:(p + 1) * L] for p in range(pairs)], axis=1)
    ti = lax.broadcasted_iota(I32, (L, L), 0)
    tj = lax.broadcasted_iota(I32, (L, L), 1)
    tril_incl = jnp.where(ti >= tj, 1.0, 0.0).astype(BF16)
    hr = lax.broadcasted_iota(I32, (L, LANES), 0)
    hc = lax.broadcasted_iota(I32, (L, LANES), 1)
    left = hc < hn
    strict = hr > (hc & (hn - 1))
    incl = hr >= (hc & (hn - 1))

    r_all = r_ref[...]
    k_all = k_ref[...]
    v_all = v_ref[...]
    z = w0_ref[...] + wl_ref[...]
    u = -z
    softplus = jnp.maximum(u, 0.0) + jnp.log(1.0 + jnp.exp(-jnp.abs(u)))
    w_log = -softplus - 0.5
    lw = -jnp.exp(w_log)
    a_sig = _sigmoid(a0_ref[...] + al_ref[...])
    k2_all = k_all * (1.0 + (a_sig - 1.0) * ka_ref[...])
    kk_all = k_all * kk_ref[...]
    csum = _dot3_left(tril_incl, lw)
    c_last = csum[L - 1:L, :]
    e_c = jnp.exp(csum)
    e_cprev = jnp.exp(csum - lw)
    e_neg = jnp.exp(-csum)
    e_rem = jnp.exp(c_last - csum)
    gam_all = jnp.exp(c_last)

    bdot = functools.partial(jnp.dot, preferred_element_type=F32)
    cat = jnp.concatenate
    prs = range(pairs)
    sls = [slice(p * LANES, (p + 1) * LANES) for p in prs]
    ksq = seg_mean(kk_all * kk_all)
    kk_n = kk_all / jnp.maximum(jnp.sqrt(hn * ksq), 1e-12)
    bvec_all = kk_n * a_sig
    at_all = -kk_n * e_cprev
    rt_all = r_all * e_c
    bt = [(bvec_all[:, sl] * e_neg[:, sl]).astype(BF16) for sl in sls]
    kt = [(k2_all[:, sl] * e_neg[:, sl]).astype(BF16) for sl in sls]
    a0 = [jnp.where(left, at_all[:, sl], 0.0).astype(BF16) for sl in sls]
    a1 = [jnp.where(left, 0.0, at_all[:, sl]).astype(BF16) for sl in sls]
    r0 = [jnp.where(left, rt_all[:, sl], 0.0).astype(BF16) for sl in sls]
    r1 = [jnp.where(left, 0.0, rt_all[:, sl]).astype(BF16) for sl in sls]
    vsw = [cat([jnp.where(left, 0.0, v_all[:, sl]), jnp.where(left, v_all[:, sl], 0.0)],
               axis=0).astype(BF16) for sl in sls]

    lhs = [cat([a0[p], r0[p], a1[p], r1[p]], axis=0) for p in prs]
    o01 = [lax.dot_general(lhs[p], cat([bt[p], kt[p]], axis=0), _NT,
                           preferred_element_type=F32) for p in prs]
    o0 = [o01[p][:2 * L] for p in prs]
    o1 = [pltpu.roll(o01[p][2 * L:], hn, 1) for p in prs]
    st = [state[p] for p in prs]
    ars = [lax.dot_general(lhs[p], st[p].astype(BF16), _NT,
                           preferred_element_type=F32) for p in prs]
    u0 = [jnp.where(strict, o0[p][:L], 0.0) for p in prs]
    u1 = [jnp.where(strict, o1[p][:L], 0.0) for p in prs]
    m_ab = [cat([jnp.where(left, u0[p], 0.0), jnp.where(left, 0.0, u1[p])], axis=0) for p in prs]
    m_ak = [cat([jnp.where(left, 0.0, u0[p]), jnp.where(left, u1[p], 0.0)], axis=0) for p in prs]
    xt = [(cat([ars[p][:L], ars[p][2 * L:3 * L]], axis=0)
           + bdot(m_ak[p].astype(BF16), vsw[p])).T for p in prs]
    mt = [m_ab[p].T for p in prs]
    steps = (L - 1).bit_length()
    for it in range(steps):
        mb = [mt[p].astype(BF16) for p in prs]
        if it + 1 < steps:
            prod = [bdot(cat([xt[p].astype(BF16), mb[p]], axis=0), mb[p]) for p in prs]
            xt = [xt[p] + prod[p][:LANES] for p in prs]
            mt = [prod[p][LANES:] for p in prs]
        else:
            xt = [xt[p] + bdot(xt[p].astype(BF16), mb[p]) for p in prs]
    x = [xt[p].T for p in prs]
    y0 = [jnp.where(incl, o0[p][L:], 0.0) for p in prs]
    y1 = [jnp.where(incl, o1[p][L:], 0.0) for p in prs]
    m_rb = [cat([jnp.where(left, y0[p], 0.0), jnp.where(left, 0.0, y1[p])], axis=0) for p in prs]
    m_rk = [cat([jnp.where(left, 0.0, y0[p]), jnp.where(left, y1[p], 0.0)], axis=0) for p in prs]
    ys = [cat([ars[p][L:2 * L], ars[p][3 * L:]], axis=0)
          + bdot(cat([m_rb[p], m_rk[p]], axis=1).astype(BF16),
                 cat([x[p].astype(BF16), vsw[p]], axis=0)) for p in prs]
    y_all = cat([ys[p][:L] + ys[p][L:] for p in prs], axis=1)
    sv_t = [cat([x[p][:L] + x[p][L:], v_all[:, sl]], axis=0).T.astype(BF16)
            for p, sl in enumerate(sls)]
    bh_all = (bvec_all * e_rem).astype(BF16)
    kh_all = (k2_all * e_rem).astype(BF16)
    bkh = [cat([bh_all[:, sl], kh_all[:, sl]], axis=0) for sl in sls]
    upd = [bdot(sv_t[p], bkh[p]) for p in prs]
    for p, sl in enumerate(sls):
        state[p] = st[p] * gam_all[:, sl] + jnp.where(same_head, upd[p], 0.0)

    yc = y_all - seg_mean(y_all)
    var = seg_mean(yc * yc)
    rk_sum = hn * seg_mean(r_all * k2_all * rk_ref[...])
    yn = yc * lax.rsqrt(var + GN_EPS) * lnw_ref[...] + lnb_ref[...]
    o_ref[...] = ((yn + rk_sum * v_all) * g_ref[...]).astype(o_ref.dtype)


def rwkv7_mixer(xs, gain, next_gain, bsz, seq, mu, w_r, w_k, w_v, w_o, w0, w1, w2, a0, a1, a2,
                g1, g2, k_k, k_a, r_k, lnx_w, lnx_b):
    x_res = xs[0]
    n, d = x_res.shape
    gpad = (-GATE_LORA) % LANES
    g1p = jnp.pad(g1, ((0, 0), (0, gpad))).astype(BF16)
    g2p = jnp.pad(g2, ((0, gpad), (0, 0))).astype(BF16)
    xr, xk, xv, tw, ta, tg = token_shift_mix(x_res, gain, mu, w1.astype(BF16), a1.astype(BF16),
                                             g1p, bsz, seq)
    r = matmul(xr, w_r)
    k = matmul(xk, w_k)
    v = matmul(xv, w_v)
    wl = matmul(tw, w2.astype(BF16))
    al = matmul(ta, a2.astype(BF16))
    gate = matmul(tg, g2p)

    L, pairs = CHUNK, 32
    nc = seq // L
    wide = LANES * pairs
    row = lambda a: a.reshape(1, d).astype(F32)
    dat = pl.BlockSpec((L, wide), lambda b, j, c: (b * nc + c, j))
    par = pl.BlockSpec((1, wide), lambda b, j, c: (0, j))
    yg = pl.pallas_call(
        functools.partial(_wkv_kernel, L=L, pairs=pairs),
        grid=(bsz, d // wide, nc),
        in_specs=[dat] * 6 + [par] * 7,
        out_specs=dat,
        out_shape=jax.ShapeDtypeStruct((n, d), BF16),
        scratch_shapes=[pltpu.VMEM((pairs, LANES, LANES), F32)],
        compiler_params=_cparams("parallel", "parallel", "arbitrary"),
    )(r, k, v, wl, al, gate, row(w0), row(a0), row(k_k), row(k_a), row(r_k), row(lnx_w), row(lnx_b))
    return matmul(yg, w_o.astype(BF16), res=x_res, alpha=1.0, next_gain=next_gain)


def kernel(x, positions, ffn_norm_0, ffn_w_gu_0, ffn_w_down_0, mix_norm_0, mla0_w_in, mla0_q_norm, mla0_w_uq, mla0_kv_norm, mla0_w_ukv, mla0_w_o, ffn_norm_1, ffn_w_gu_1, ffn_w_down_1, mix_norm_1, dsa1_w_in, dsa1_w_o, ffn_norm_2, ffn_w_gu_2, ffn_w_down_2, mix_norm_2, rwkv2_mu, rwkv2_w_r, rwkv2_w_k, rwkv2_w_v, rwkv2_w_o, rwkv2_w0, rwkv2_w1, rwkv2_w2, rwkv2_a0, rwkv2_a1, rwkv2_a2, rwkv2_g1, rwkv2_g2, rwkv2_k_k, rwkv2_k_a, rwkv2_r_k, rwkv2_lnx_w, rwkv2_lnx_b, ffn_norm_3, ffn_w_gu_3, ffn_w_down_3, mix_norm_3, mla3_w_in, mla3_q_norm, mla3_w_uq, mla3_kv_norm, mla3_w_ukv, mla3_w_o, final_norm):
    bsz, seq, d = x.shape
    n = bsz * seq
    pos_col = positions.reshape(n, 1)
    tabs_mla = rope_tables(pos_col, "spread64")
    tabs128 = rope_tables(pos_col, "half128")
    tabs64 = rope_tables(pos_col, "lead64")

    def mla(xs, nxt, w_in, q_norm, w_uq, kv_norm, w_ukv, w_o):
        return mla_mixer(xs, nxt, tabs_mla, bsz, seq, w_in, q_norm, w_uq, kv_norm, w_ukv, w_o)

    def rwkv(xs, gain, nxt):
        return rwkv7_mixer(xs, gain, nxt, bsz, seq, rwkv2_mu, rwkv2_w_r, rwkv2_w_k, rwkv2_w_v,
                           rwkv2_w_o, rwkv2_w0, rwkv2_w1, rwkv2_w2, rwkv2_a0, rwkv2_a1, rwkv2_a2,
                           rwkv2_g1, rwkv2_g2, rwkv2_k_k, rwkv2_k_a,
                           rwkv2_r_k.reshape(-1), rwkv2_lnx_w, rwkv2_lnx_b)

    mixers = [
        lambda xs, gn, nxt: mla(xs, nxt, mla0_w_in, mla0_q_norm, mla0_w_uq, mla0_kv_norm,
                                mla0_w_ukv, mla0_w_o),
        lambda xs, gn, nxt: dsa_mixer(xs, nxt, tabs64, tabs128, bsz, seq, dsa1_w_in, dsa1_w_o),
        rwkv,
        lambda xs, gn, nxt: mla(xs, nxt, mla3_w_in, mla3_q_norm, mla3_w_uq, mla3_kv_norm,
                                mla3_w_ukv, mla3_w_o),
    ]
    ffns = [(ffn_norm_0, ffn_w_gu_0, ffn_w_down_0, mix_norm_0),
            (ffn_norm_1, ffn_w_gu_1, ffn_w_down_1, mix_norm_1),
            (ffn_norm_2, ffn_w_gu_2, ffn_w_down_2, mix_norm_2),
            (ffn_norm_3, ffn_w_gu_3, ffn_w_down_3, mix_norm_3)]
    xs = norm_prep(x.reshape(n, d), ffn_norm_0[0])
    for i in range(4):
        f_norm, w_gu, w_down, m_norm = ffns[i]
        after_layer = ffns[i + 1][0][0] if i + 1 < len(ffns) else final_norm
        xs = ffn_half_step(xs, w_gu, w_down, 0, m_norm)
        xs = mixers[i](xs, m_norm, f_norm[1])
        xs = ffn_half_step(xs, w_gu, w_down, 1, after_layer)
    return rmsnorm(xs[0], final_norm, out_dtype=F32).reshape(bsz, seq, d)
```
